```python
import math
import jax, jax.numpy as jnp
from jax import lax
import numpy as np

D_MODEL = 2048
BATCH = 1
SEQ = 16384
DEPTH = 1

D_MIX = D_MODEL
CONV_CH = D_MIX // 2
CONV_GROUPS = 8
CONV_WIDTH = 31
N_HEADS = 8
QK_NOPE_DIM = 128
QK_ROPE_DIM = 64
QK_DIM = QK_NOPE_DIM + QK_ROPE_DIM
V_DIM = 128
Q_RANK = 512
KV_RANK = 256
D_IN_PROJ = 2 * CONV_CH + Q_RANK + KV_RANK + QK_ROPE_DIM
D_FF = 5504
ROPE_THETA = 10000.0
Q_BLOCK = 128
EPS = 1e-6

kernel_name = "hybrid_conformer_mla_parallel_layer"


def rms_norm(x, g):
    xf = x.astype(jnp.float32)
    y = xf * lax.rsqrt(jnp.mean(xf * xf, axis=-1, keepdims=True) + EPS)
    return (y * g.astype(jnp.float32)).astype(x.dtype)


def swiglu(x, w_gate, w_up, w_down):
    return (jax.nn.silu(x @ w_gate) * (x @ w_up)) @ w_down


def rope(x, positions):
    r = x.shape[-1]
    half = r // 2
    inv_freq = ROPE_THETA ** (-(jnp.arange(half, dtype=jnp.float32) * 2.0 / r))
    ang = positions.astype(jnp.float32)[..., None] * inv_freq
    cos = jnp.cos(ang)[:, :, None, :]
    sin = jnp.sin(ang)[:, :, None, :]
    xf = x.astype(jnp.float32)
    x1, x2 = xf[..., :half], xf[..., half:]
    out = jnp.concatenate([x1 * cos - x2 * sin, x2 * cos + x1 * sin], axis=-1)
    return out.astype(x.dtype)


def block_attention(q, k, v):
    b, s, h, d = q.shape
    nblk = s // Q_BLOCK
    scale = 1.0 / math.sqrt(d)
    qb = q.reshape(b, nblk, Q_BLOCK, h, d).transpose(1, 0, 2, 3, 4)

    def one_block(qi):
        sc = jnp.einsum('bqhd,bkhd->bhqk', qi, k).astype(jnp.float32) * scale
        p = jax.nn.softmax(sc, axis=-1).astype(v.dtype)
        return jnp.einsum('bhqk,bkhd->bqhd', p, v)

    o = lax.map(one_block, qb)
    return o.transpose(1, 0, 2, 3, 4).reshape(b, s, h, v.shape[-1])


def conv_module(z_conv, dw_kernel, dw_bias, conv_norm):
    u = z_conv[..., :CONV_CH] * jax.nn.sigmoid(z_conv[..., CONV_CH:])
    pad = CONV_WIDTH // 2
    y = lax.conv_general_dilated(
        u, dw_kernel.astype(u.dtype)[:, None, :],
        window_strides=(1,), padding=[(pad, pad)],
        dimension_numbers=('NWC', 'WIO', 'NWC'),
        feature_group_count=CONV_CH)
    y = y + dw_bias
    return jax.nn.silu(rms_norm(y, conv_norm))


def mla(z_q, z_kv, z_kpe, positions, q_norm, w_uq, kv_norm, w_ukv, q_head_norm, k_head_norm):
    b, s = z_q.shape[0], z_q.shape[1]
    q = (rms_norm(z_q, q_norm) @ w_uq).reshape(b, s, N_HEADS, QK_DIM)
    kv = (rms_norm(z_kv, kv_norm) @ w_ukv).reshape(b, s, N_HEADS, QK_NOPE_DIM + V_DIM)
    k_nope, v = kv[..., :QK_NOPE_DIM], kv[..., QK_NOPE_DIM:]
    k_pe = jnp.broadcast_to(z_kpe[:, :, None, :], (b, s, N_HEADS, QK_ROPE_DIM))
    k = jnp.concatenate([k_nope, k_pe], axis=-1)
    q = rms_norm(q, q_head_norm)
    k = rms_norm(k, k_head_norm)
    q = jnp.concatenate([q[..., :QK_NOPE_DIM], rope(q[..., QK_NOPE_DIM:], positions)], axis=-1)
    k = jnp.concatenate([k[..., :QK_NOPE_DIM], rope(k[..., QK_NOPE_DIM:], positions)], axis=-1)
    o = block_attention(q, k, v)
    return o.reshape(b, s, N_HEADS * V_DIM)


def setup_inputs(seed: int = 0) -> dict:
    key = jax.random.key(seed)
    ks = jax.random.split(key, 24)
    f32 = jnp.float32

    def w(k, shape, fan_in):
        return jax.random.normal(k, shape, f32) * (fan_in ** -0.5)

    def gain(k, shape):
        return 1.0 + 0.02 * jax.random.normal(k, shape, f32)

    L = DEPTH
    x = jax.random.normal(ks[0], (BATCH, SEQ, D_MODEL), f32)
    positions = jnp.broadcast_to(jnp.arange(SEQ, dtype=jnp.int32)[None, :], (BATCH, SEQ))
    return {
        "x": x,
        "positions": positions,
        "ffn1_norm": gain(ks[1], (L, D_MODEL)),
        "ffn1_w_gate": w(ks[2], (L, D_MODEL, D_FF), D_MODEL),
        "ffn1_w_up": w(ks[3], (L, D_MODEL, D_FF), D_MODEL),
        "ffn1_w_down": w(ks[4], (L, D_FF, D_MODEL), D_FF),
        "mix_norm": gain(ks[5], (L, D_MODEL)),
        "w_in": w(ks[6], (L, D_MODEL, D_IN_PROJ), D_MODEL),
        "q_norm": gain(ks[7], (L, Q_RANK)),
        "w_uq": w(ks[8], (L, Q_RANK, N_HEADS * QK_DIM), Q_RANK),
        "kv_norm": gain(ks[9], (L, KV_RANK)),
        "w_ukv": w(ks[10], (L, KV_RANK, N_HEADS * (QK_NOPE_DIM + V_DIM)), KV_RANK),
        "q_head_norm": gain(ks[11], (L, QK_DIM)),
        "k_head_norm": gain(ks[12], (L, QK_DIM)),
        "dw_kernel": w(ks[13], (L, CONV_WIDTH, CONV_CH), CONV_WIDTH),
        "dw_bias": 0.02 * jax.random.normal(ks[14], (L, CONV_CH), f32),
        "conv_norm": gain(ks[15], (L, CONV_CH)),
        "w_out": w(ks[16], (L, D_MIX, D_MODEL), D_MIX),
        "ffn2_norm": gain(ks[17], (L, D_MODEL)),
        "ffn2_w_gate": w(ks[18], (L, D_MODEL, D_FF), D_MODEL),
        "ffn2_w_up": w(ks[19], (L, D_MODEL, D_FF), D_MODEL),
        "ffn2_w_down": w(ks[20], (L, D_FF, D_MODEL), D_FF),
        "final_norm": gain(ks[21], (L, D_MODEL)),
    }


def reference(x, positions, ffn1_norm, ffn1_w_gate, ffn1_w_up, ffn1_w_down, mix_norm, w_in,
              q_norm, w_uq, kv_norm, w_ukv, q_head_norm, k_head_norm, dw_kernel, dw_bias,
              conv_norm, w_out, ffn2_norm, ffn2_w_gate, ffn2_w_up, ffn2_w_down, final_norm):
    c0 = 2 * CONV_CH
    c1 = c0 + Q_RANK
    c2 = c1 + KV_RANK
    for l in range(DEPTH):
        x = x + 0.5 * swiglu(rms_norm(x, ffn1_norm[l]), ffn1_w_gate[l], ffn1_w_up[l], ffn1_w_down[l])
        h = rms_norm(x, mix_norm[l])
        z = h @ w_in[l]
        a_out = conv_module(z[..., :c0], dw_kernel[l], dw_bias[l], conv_norm[l])
        b_out = mla(z[..., c0:c1], z[..., c1:c2], z[..., c2:], positions,
                    q_norm[l], w_uq[l], kv_norm[l], w_ukv[l], q_head_norm[l], k_head_norm[l])
        x = x + jnp.concatenate([a_out, b_out], axis=-1) @ w_out[l]
        x = x + 0.5 * swiglu(rms_norm(x, ffn2_norm[l]), ffn2_w_gate[l], ffn2_w_up[l], ffn2_w_down[l])
        x = rms_norm(x, final_norm[l])
    return x
```

```python
import functools
import math

import jax
import jax.numpy as jnp
from jax import lax
from jax.experimental import pallas as pl
from jax.experimental.pallas import tpu as pltpu

F32 = jnp.float32
BF16 = jnp.bfloat16

EPS = 1e-6
ROPE_THETA = 10000.0
CONV_CH = 1024
CONV_WIDTH = 31
CONV_PAD = CONV_WIDTH // 2
N_HEADS = 8
NOPE = 128
ROPE = 64
QK_DIM = NOPE + ROPE
V_DIM = 128
Q_RANK = 512
KV_RANK = 256

LANES = 128
SUBLANES_BF16 = 16
HEAD_PAD = 2 * LANES
VMEM_LIMIT = 56 * 1024 * 1024

FF_TILE = 512
FFN_ROWS = 512
PROJ_ROWS = 256
ATTN_Q = 512
ATTN_K = 512
CONV_ROWS = 256
CONV_SUB = 64
OUT_ROWS = 512
HALO = 16


def _rms(x, g):
    ms = jnp.mean(x * x, axis=-1, keepdims=True)
    return x * lax.rsqrt(ms + EPS) * g


def _params(sem):
    return pltpu.CompilerParams(dimension_semantics=sem, vmem_limit_bytes=VMEM_LIMIT)


def _resident(shape):
    nd = len(shape)
    return pl.BlockSpec(shape, lambda *_: (0,) * nd, pipeline_mode=pl.Buffered(1))


def _ffn_kernel(*refs, final_norm):
    if final_norm:
        x_ref, g_ref, wg_ref, wu_ref, wd_ref, fg_ref, o_ref, xn_ref = refs
    else:
        x_ref, g_ref, wg_ref, wu_ref, wd_ref, o_ref, xn_ref = refs
    f = pl.program_id(1)

    @pl.when(f == 0)
    def _():
        xn_ref[...] = _rms(x_ref[...], g_ref[...]).astype(BF16)
        o_ref[...] = jnp.zeros_like(o_ref)

    xn = xn_ref[...]
    gate = jnp.dot(xn, wg_ref[...], preferred_element_type=F32)
    up = jnp.dot(xn, wu_ref[...], preferred_element_type=F32)
    h = (gate * jax.nn.sigmoid(gate) * up).astype(BF16)
    o_ref[...] += jnp.dot(h, wd_ref[...], preferred_element_type=F32)

    @pl.when(f == pl.num_programs(1) - 1)
    def _():
        y = x_ref[...] + 0.5 * o_ref[...]
        if final_norm:
            y = _rms(y, fg_ref[...])
        o_ref[...] = y


def _ffn(x, g, wg, wu, wd, final_g, *, rows):
    s, d = x.shape
    fp = wg.shape[1]
    tf = min(FF_TILE, fp)
    in_specs = [
        pl.BlockSpec((rows, d), lambda i, f: (i, 0)),
        pl.BlockSpec((1, d), lambda i, f: (0, 0)),
        pl.BlockSpec((d, tf), lambda i, f: (0, f)),
        pl.BlockSpec((d, tf), lambda i, f: (0, f)),
        pl.BlockSpec((tf, d), lambda i, f: (f, 0)),
    ]
    args = [x, g, wg, wu, wd]
    if final_g is not None:
        in_specs.append(pl.BlockSpec((1, d), lambda i, f: (0, 0)))
        args.append(final_g)
    return pl.pallas_call(
        functools.partial(_ffn_kernel, final_norm=final_g is not None),
        grid=(s // rows, fp // tf),
        in_specs=in_specs,
        out_specs=pl.BlockSpec((rows, d), lambda i, f: (i, 0)),
        out_shape=jax.ShapeDtypeStruct((s, d), F32),
        scratch_shapes=[pltpu.VMEM((rows, d), BF16)],
        compiler_params=_params(("parallel", "arbitrary")),
        name="ffn_final" if final_g is not None else "ffn",
    )(*args)


def _proj_kernel(x_ref, pos_ref, mixg_ref, win_ref, qg_ref, wuq_ref, kvg_ref, wuk_ref, wuv_ref,
                 qhg_ref, khg_ref, invf_ref, u_ref, q_ref, k_ref, vt_ref):
    hn = _rms(x_ref[...], mixg_ref[...]).astype(BF16)
    z = jnp.dot(hn, win_ref[...], preferred_element_type=F32)
    c0 = 2 * CONV_CH
    c1 = c0 + Q_RANK
    c2 = c1 + KV_RANK
    u_ref[...] = z[:, :CONV_CH] * jax.nn.sigmoid(z[:, CONV_CH:c0])
    qn = _rms(z[:, c0:c1], qg_ref[...]).astype(BF16)
    kvn = _rms(z[:, c1:c2], kvg_ref[...]).astype(BF16)
    kpe = z[:, c2:c2 + LANES]
    q = jnp.dot(qn, wuq_ref[...], preferred_element_type=F32)
    kn = jnp.dot(kvn, wuk_ref[...], preferred_element_type=F32)
    v = jnp.dot(kvn, wuv_ref[...], preferred_element_type=F32)

    ang = pos_ref[...].astype(F32) * invf_ref[...]
    cos_t = jnp.cos(ang)
    sin_t = jnp.sin(ang)
    lane = lax.broadcasted_iota(jnp.int32, ang.shape, 1)
    half = ROPE // 2
    sin_lo = jnp.where(lane < half, -sin_t, 0.0)
    sin_hi = jnp.where((lane >= half) & (lane < ROPE), sin_t, 0.0)

    def rope(t):
        return (t * cos_t + pltpu.roll(t, LANES - half, 1) * sin_lo
                + pltpu.roll(t, half, 1) * sin_hi)

    qhg = qhg_ref[...]
    khg = khg_ref[...]
    inv_d = 1.0 / QK_DIM
    scale = 1.0 / math.sqrt(QK_DIM)
    kpe_ss = jnp.sum(kpe * kpe, axis=-1, keepdims=True)
    kpe_rot = rope(kpe * khg[:, NOPE:])
    for h in range(N_HEADS):
        qa = q[:, h * HEAD_PAD:h * HEAD_PAD + NOPE]
        qb = q[:, h * HEAD_PAD + NOPE:(h + 1) * HEAD_PAD]
        ss = jnp.sum(qa * qa, axis=-1, keepdims=True) + jnp.sum(qb * qb, axis=-1, keepdims=True)
        r = lax.rsqrt(ss * inv_d + EPS) * scale
        q_ref[h, :, :NOPE] = (qa * r * qhg[:, :NOPE]).astype(BF16)
        q_ref[h, :, NOPE:] = (rope(qb * qhg[:, NOPE:]) * r).astype(BF16)
        ka = kn[:, h * NOPE:(h + 1) * NOPE]
        rk = lax.rsqrt((jnp.sum(ka * ka, axis=-1, keepdims=True) + kpe_ss) * inv_d + EPS)
        k_ref[h, :, :NOPE] = (ka * rk * khg[:, :NOPE]).astype(BF16)
        k_ref[h, :, NOPE:] = (kpe_rot * rk).astype(BF16)
        vt_ref[h] = v[:, h * V_DIM:(h + 1) * V_DIM].T.astype(BF16)


def _proj(x1, pos, mixg, win, qg, wuq, kvg, wuk, wuv, qhg, khg, invf, *, rows):
    s, d = x1.shape
    row = lambda w: pl.BlockSpec((rows, w), lambda i: (i, 0))
    in_specs = [row(d), row(1), _resident(mixg.shape), _resident(win.shape), _resident(qg.shape),
                _resident(wuq.shape), _resident(kvg.shape), _resident(wuk.shape),
                _resident(wuv.shape), _resident(qhg.shape), _resident(khg.shape),
                _resident(invf.shape)]
    out_shape = [
        jax.ShapeDtypeStruct((s, CONV_CH), F32),
        jax.ShapeDtypeStruct((N_HEADS, s, HEAD_PAD), BF16),
        jax.ShapeDtypeStruct((N_HEADS, s, HEAD_PAD), BF16),
        jax.ShapeDtypeStruct((N_HEADS, V_DIM, s), BF16),
    ]
    out_specs = [
        row(CONV_CH),
        pl.BlockSpec((N_HEADS, rows, HEAD_PAD), lambda i: (0, i, 0)),
        pl.BlockSpec((N_HEADS, rows, HEAD_PAD), lambda i: (0, i, 0)),
        pl.BlockSpec((N_HEADS, V_DIM, rows), lambda i: (0, 0, i)),
    ]
    return pl.pallas_call(
        _proj_kernel,
        grid=(s // rows,),
        in_specs=in_specs,
        out_specs=out_specs,
        out_shape=out_shape,
        compiler_params=_params(("parallel",)),
        name="proj",
    )(x1, pos, mixg, win, qg, wuq, kvg, wuk, wuv, qhg, khg, invf)


def _attn_kernel(q_ref, k_ref, vt_ref, o_ref, m_ref, l_ref, acc_ref, *, tk):
    nk = k_ref.shape[0] // tk
    m_ref[...] = jnp.full(m_ref.shape, -jnp.inf, F32)
    l_ref[...] = jnp.zeros_like(l_ref)
    acc_ref[...] = jnp.zeros_like(acc_ref)
    q = q_ref[...]

    def body(j, carry):
        off = pl.multiple_of(j * tk, tk)
        st = lax.dot_general(k_ref[pl.ds(off, tk), :], q, (((1,), (1,)), ((), ())),
                             preferred_element_type=F32)
        m_prev = m_ref[...]
        m_new = jnp.maximum(m_prev, jnp.max(st, axis=0, keepdims=True))
        alpha = jnp.exp(m_prev - m_new)
        p = jnp.exp(st - m_new)
        l_ref[...] = alpha * l_ref[...] + jnp.sum(p, axis=0, keepdims=True)
        pv = jnp.dot(vt_ref[:, pl.ds(off, tk)], p.astype(BF16), preferred_element_type=F32)
        acc_ref[...] = alpha * acc_ref[...] + pv
        m_ref[...] = m_new
        return carry

    lax.fori_loop(0, nk, body, 0)
    o = acc_ref[...] * (1.0 / l_ref[...])
    o_ref[...] = o.T.astype(o_ref.dtype)


def _attn(q, k, vt, *, tq, tk):
    h, s, dp = q.shape
    return pl.pallas_call(
        functools.partial(_attn_kernel, tk=tk),
        grid=(h, s // tq),
        in_specs=[
            pl.BlockSpec((None, tq, dp), lambda hh, i: (hh, i, 0)),
            pl.BlockSpec((None, s, dp), lambda hh, i: (hh, 0, 0)),
            pl.BlockSpec((None, V_DIM, s), lambda hh, i: (hh, 0, 0)),
        ],
        out_specs=pl.BlockSpec((tq, V_DIM), lambda hh, i: (i, hh)),
        out_shape=jax.ShapeDtypeStruct((s, h * V_DIM), BF16),
        scratch_shapes=[pltpu.VMEM((1, tq), F32), pltpu.VMEM((1, tq), F32),
                        pltpu.VMEM((V_DIM, tq), F32)],
        compiler_params=_params(("parallel", "arbitrary")),
        name="attn",
    )(q, k, vt)


def _conv_kernel(prev_ref, u_ref, next_ref, w_ref, b_ref, g_ref, o_ref, buf_ref, y_ref, *, sub):
    i = pl.program_id(0)
    rows = u_ref.shape[0]
    buf_ref[:HALO, :] = jnp.where(i > 0, prev_ref[...], 0.0)
    buf_ref[HALO:HALO + rows, :] = u_ref[...]
    buf_ref[HALO + rows:, :] = jnp.where(i < pl.num_programs(0) - 1, next_ref[...], 0.0)
    base = HALO - CONV_PAD
    for r0 in range(0, rows, sub):
        for c in range(0, CONV_CH, LANES):
            acc = jnp.broadcast_to(b_ref[:, c:c + LANES], (sub, LANES))
            for t in range(CONV_WIDTH):
                acc = acc + buf_ref[r0 + base + t:r0 + base + t + sub, c:c + LANES] * w_ref[t:t + 1, c:c + LANES]
            y_ref[r0:r0 + sub, c:c + LANES] = acc
    y = _rms(y_ref[...], g_ref[...])
    o_ref[...] = (y * jax.nn.sigmoid(y)).astype(o_ref.dtype)


def _conv(u, w, b, g, *, rows, sub):
    s, c = u.shape
    hb = rows // HALO
    last = s // HALO - 1
    return pl.pallas_call(
        functools.partial(_conv_kernel, sub=sub),
        grid=(s // rows,),
        in_specs=[
            pl.BlockSpec((HALO, c), lambda i: (jnp.maximum(i * hb - 1, 0), 0)),
            pl.BlockSpec((rows, c), lambda i: (i, 0)),
            pl.BlockSpec((HALO, c), lambda i: (jnp.minimum((i + 1) * hb, last), 0)),
            _resident(w.shape), _resident(b.shape), _resident(g.shape),
        ],
        out_specs=pl.BlockSpec((rows, c), lambda i: (i, 0)),
        out_shape=jax.ShapeDtypeStruct((s, c), BF16),
        scratch_shapes=[pltpu.VMEM((rows + 2 * HALO, c), F32), pltpu.VMEM((rows, c), F32)],
        compiler_params=_params(("parallel",)),
        name="conv",
    )(u, u, u, w, b, g)


def _outproj_kernel(x_ref, a_ref, b_ref, wa_ref, wb_ref, o_ref):
    o_ref[...] = (x_ref[...]
                  + jnp.dot(a_ref[...], wa_ref[...], preferred_element_type=F32)
                  + jnp.dot(b_ref[...], wb_ref[...], preferred_element_type=F32))


def _outproj(x1, a, b, wa, wb, *, rows):
    s, d = x1.shape
    row = lambda w: pl.BlockSpec((rows, w), lambda i: (i, 0))
    return pl.pallas_call(
        _outproj_kernel,
        grid=(s // rows,),
        in_specs=[row(d), row(a.shape[1]), row(b.shape[1]), _resident(wa.shape), _resident(wb.shape)],
        out_specs=row(d),
        out_shape=jax.ShapeDtypeStruct((s, d), F32),
        compiler_params=_params(("parallel",)),
        name="outproj",
    )(x1, a, b, wa, wb)


def _pad_cols(w, n):
    return jnp.pad(w, ((0, 0), (0, n - w.shape[1])))


def _ffn_weights(wg, wu, wd):
    f = wg.shape[1]
    fp = -(-f // FF_TILE) * FF_TILE
    return (_pad_cols(wg, fp).astype(BF16), _pad_cols(wu, fp).astype(BF16),
            jnp.pad(wd, ((0, fp - f), (0, 0))).astype(BF16))


def _head_gain(g):
    return jnp.pad(g, (0, HEAD_PAD - QK_DIM)).reshape(1, HEAD_PAD)


def _layer(x, positions, tiles, ffn1_norm, ffn1_w_gate, ffn1_w_up, ffn1_w_down, mix_norm, w_in,
           q_norm, w_uq, kv_norm, w_ukv, q_head_norm, k_head_norm, dw_kernel, dw_bias, conv_norm,
           w_out, ffn2_norm, ffn2_w_gate, ffn2_w_up, ffn2_w_down, final_norm):
    s, d = x.shape
    row = lambda g: g.reshape(1, -1)

    x1 = _ffn(x, row(ffn1_norm), *_ffn_weights(ffn1_w_gate, ffn1_w_up, ffn1_w_down), None,
              rows=tiles["ffn"])

    c2 = 2 * CONV_CH + Q_RANK + KV_RANK
    win = _pad_cols(w_in, c2 + LANES).astype(BF16)
    wuq = jnp.pad(w_uq.reshape(Q_RANK, N_HEADS, QK_DIM),
                  ((0, 0), (0, 0), (0, HEAD_PAD - QK_DIM))).reshape(Q_RANK, N_HEADS * HEAD_PAD)
    wkv = w_ukv.reshape(KV_RANK, N_HEADS, NOPE + V_DIM)
    wuk = wkv[:, :, :NOPE].reshape(KV_RANK, N_HEADS * NOPE)
    wuv = wkv[:, :, NOPE:].reshape(KV_RANK, N_HEADS * V_DIM)
    half = ROPE // 2
    inv_freq = ROPE_THETA ** (-(jnp.arange(half, dtype=F32) * 2.0 / ROPE))
    invf = jnp.pad(jnp.concatenate([inv_freq, inv_freq]), (0, LANES - ROPE)).reshape(1, LANES)
    u, q, k, vt = _proj(x1, positions.reshape(s, 1), row(mix_norm), win, row(q_norm),
                        wuq.astype(BF16), row(kv_norm), wuk.astype(BF16), wuv.astype(BF16),
                        _head_gain(q_head_norm), _head_gain(k_head_norm), invf, rows=tiles["proj"])

    b_out = _attn(q, k, vt, tq=tiles["attn_q"], tk=tiles["attn_k"])
    a_out = _conv(u, jnp.pad(dw_kernel, ((0, 1), (0, 0))), row(dw_bias), row(conv_norm),
                  rows=tiles["conv"], sub=tiles["conv_sub"])
    wo = w_out.astype(BF16)
    x2 = _outproj(x1, a_out, b_out, wo[:CONV_CH], wo[CONV_CH:], rows=tiles["out"])

    return _ffn(x2, row(ffn2_norm), *_ffn_weights(ffn2_w_gate, ffn2_w_up, ffn2_w_down),
                row(final_norm), rows=tiles["ffn"])


def _tiles(s):
    return {
        "ffn": min(FFN_ROWS, s), "proj": min(PROJ_ROWS, s), "attn_q": min(ATTN_Q, s),
        "attn_k": min(ATTN_K, s), "conv": min(CONV_ROWS, s), "conv_sub": CONV_SUB,
        "out": min(OUT_ROWS, s),
    }


def kernel(x, positions, ffn1_norm, ffn1_w_gate, ffn1_w_up, ffn1_w_down, mix_norm, w_in, q_norm, w_uq, kv_norm, w_ukv, q_head_norm, k_head_norm, dw_kernel, dw_bias, conv_norm, w_out, ffn2_norm, ffn2_w_gate, ffn2_w_up, ffn2_w_down, final_norm):
    batch, s, d = x.shape
    assert ffn1_norm.shape[0] == 1, "single-layer stack expected"
    outs = []
    for bi in range(batch):
        outs.append(_layer(
            x[bi], positions[bi], _tiles(s), ffn1_norm[0], ffn1_w_gate[0], ffn1_w_up[0],
            ffn1_w_down[0], mix_norm[0], w_in[0], q_norm[0], w_uq[0], kv_norm[0], w_ukv[0],
            q_head_norm[0], k_head_norm[0], dw_kernel[0], dw_bias[0], conv_norm[0], w_out[0],
            ffn2_norm[0], ffn2_w_gate[0], ffn2_w_up[0], ffn2_w_down[0], final_norm[0]))
    return jnp.stack(outs)
```

```python
import functools
import math

import jax
import jax.numpy as jnp
from jax import lax
from jax.experimental import pallas as pl
from jax.experimental.pallas import tpu as pltpu

F32 = jnp.float32
BF16 = jnp.bfloat16

EPS = 1e-6
ROPE_THETA = 10000.0
CONV_CH = 1024
CONV_WIDTH = 31
CONV_PAD = CONV_WIDTH // 2
N_HEADS = 8
NOPE = 128
ROPE = 64
QK_DIM = NOPE + ROPE
V_DIM = 128
Q_RANK = 512
KV_RANK = 256

LANES = 128
SUBLANES_BF16 = 16
HEAD_PAD = 2 * LANES
VMEM_LIMIT = 56 * 1024 * 1024

FF_TILE = 512
FFN_ROWS = 512
PROJ_ROWS = 256
ATTN_Q = 512
ATTN_K = 1024
CONV_ROWS = 256
CONV_SUB = 64
OUT_ROWS = 512
HALO = 16


def _rms(x, g):
    ms = jnp.mean(x * x, axis=-1, keepdims=True)
    return x * lax.rsqrt(ms + EPS) * g


def _params(sem):
    return pltpu.CompilerParams(dimension_semantics=sem, vmem_limit_bytes=VMEM_LIMIT)


def _resident(shape):
    nd = len(shape)
    return pl.BlockSpec(shape, lambda *_: (0,) * nd, pipeline_mode=pl.Buffered(1))


def _ffn_kernel(*refs, final_norm):
    if final_norm:
        x_ref, g_ref, wg_ref, wu_ref, wd_ref, fg_ref, o_ref, xn_ref = refs
    else:
        x_ref, g_ref, wg_ref, wu_ref, wd_ref, o_ref, xn_ref = refs
    f = pl.program_id(1)

    @pl.when(f == 0)
    def _():
        xn_ref[...] = _rms(x_ref[...], g_ref[...]).astype(BF16)
        o_ref[...] = jnp.zeros_like(o_ref)

    xn = xn_ref[...]
    gate = jnp.dot(xn, wg_ref[...], preferred_element_type=F32)
    up = jnp.dot(xn, wu_ref[...], preferred_element_type=F32)
    h = (gate * jax.nn.sigmoid(gate) * up).astype(BF16)
    o_ref[...] += jnp.dot(h, wd_ref[...], preferred_element_type=F32)

    @pl.when(f == pl.num_programs(1) - 1)
    def _():
        y = x_ref[...] + 0.5 * o_ref[...]
        if final_norm:
            y = _rms(y, fg_ref[...])
        o_ref[...] = y


def _ffn(x, g, wg, wu, wd, final_g, *, rows):
    s, d = x.shape
    fp = wg.shape[1]
    tf = min(FF_TILE, fp)
    in_specs = [
        pl.BlockSpec((rows, d), lambda i, f: (i, 0)),
        pl.BlockSpec((1, d), lambda i, f: (0, 0)),
        pl.BlockSpec((d, tf), lambda i, f: (0, f)),
        pl.BlockSpec((d, tf), lambda i, f: (0, f)),
        pl.BlockSpec((tf, d), lambda i, f: (f, 0)),
    ]
    args = [x, g, wg, wu, wd]
    if final_g is not None:
        in_specs.append(pl.BlockSpec((1, d), lambda i, f: (0, 0)))
        args.append(final_g)
    return pl.pallas_call(
        functools.partial(_ffn_kernel, final_norm=final_g is not None),
        grid=(s // rows, fp // tf),
        in_specs=in_specs,
        out_specs=pl.BlockSpec((rows, d), lambda i, f: (i, 0)),
        out_shape=jax.ShapeDtypeStruct((s, d), F32),
        scratch_shapes=[pltpu.VMEM((rows, d), BF16)],
        compiler_params=_params(("parallel", "arbitrary")),
        name="ffn_final" if final_g is not None else "ffn",
    )(*args)


def _proj_kernel(x_ref, pos_ref, mixg_ref, win_ref, qg_ref, wuq_ref, kvg_ref, wuk_ref, wuv_ref,
                 qhg_ref, khg_ref, invf_ref, u_ref, q_ref, k_ref, vt_ref):
    hn = _rms(x_ref[...], mixg_ref[...]).astype(BF16)
    z = jnp.dot(hn, win_ref[...], preferred_element_type=F32)
    c0 = 2 * CONV_CH
    c1 = c0 + Q_RANK
    c2 = c1 + KV_RANK
    u_ref[...] = z[:, :CONV_CH] * jax.nn.sigmoid(z[:, CONV_CH:c0])
    qn = _rms(z[:, c0:c1], qg_ref[...]).astype(BF16)
    kvn = _rms(z[:, c1:c2], kvg_ref[...]).astype(BF16)
    kpe = z[:, c2:c2 + LANES]
    q = jnp.dot(qn, wuq_ref[...], preferred_element_type=F32)
    kn = jnp.dot(kvn, wuk_ref[...], preferred_element_type=F32)
    v = jnp.dot(kvn, wuv_ref[...], preferred_element_type=F32)

    ang = pos_ref[...].astype(F32) * invf_ref[...]
    cos_t = jnp.cos(ang)
    sin_t = jnp.sin(ang)
    lane = lax.broadcasted_iota(jnp.int32, ang.shape, 1)
    half = ROPE // 2
    sin_lo = jnp.where(lane < half, -sin_t, 0.0)
    sin_hi = jnp.where((lane >= half) & (lane < ROPE), sin_t, 0.0)

    def rope(t):
        return (t * cos_t + pltpu.roll(t, LANES - half, 1) * sin_lo
                + pltpu.roll(t, half, 1) * sin_hi)

    qhg = qhg_ref[...]
    khg = khg_ref[...]
    inv_d = 1.0 / QK_DIM
    scale = math.log2(math.e) / math.sqrt(QK_DIM)
    kpe_ss = jnp.sum(kpe * kpe, axis=-1, keepdims=True)
    kpe_rot = rope(kpe * khg[:, NOPE:])
    for h in range(N_HEADS):
        qa = q[:, h * HEAD_PAD:h * HEAD_PAD + NOPE]
        qb = q[:, h * HEAD_PAD + NOPE:(h + 1) * HEAD_PAD]
        ss = jnp.sum(qa * qa, axis=-1, keepdims=True) + jnp.sum(qb * qb, axis=-1, keepdims=True)
        r = lax.rsqrt(ss * inv_d + EPS) * scale
        q_ref[h, :, :NOPE] = (qa * r * qhg[:, :NOPE]).astype(BF16)
        q_ref[h, :, NOPE:] = (rope(qb * qhg[:, NOPE:]) * r).astype(BF16)
        ka = kn[:, h * NOPE:(h + 1) * NOPE]
        rk = lax.rsqrt((jnp.sum(ka * ka, axis=-1, keepdims=True) + kpe_ss) * inv_d + EPS)
        k_ref[h, :, :NOPE] = (ka * rk * khg[:, :NOPE]).astype(BF16)
        k_ref[h, :, NOPE:] = (kpe_rot * rk).astype(BF16)
        vt_ref[h] = v[:, h * V_DIM:(h + 1) * V_DIM].T.astype(BF16)


def _proj(x1, pos, mixg, win, qg, wuq, kvg, wuk, wuv, qhg, khg, invf, *, rows):
    s, d = x1.shape
    row = lambda w: pl.BlockSpec((rows, w), lambda i: (i, 0))
    in_specs = [row(d), row(1), _resident(mixg.shape), _resident(win.shape), _resident(qg.shape),
                _resident(wuq.shape), _resident(kvg.shape), _resident(wuk.shape),
                _resident(wuv.shape), _resident(qhg.shape), _resident(khg.shape),
                _resident(invf.shape)]
    out_shape = [
        jax.ShapeDtypeStruct((s, CONV_CH), F32),
        jax.ShapeDtypeStruct((N_HEADS, s, HEAD_PAD), BF16),
        jax.ShapeDtypeStruct((N_HEADS, s, HEAD_PAD), BF16),
        jax.ShapeDtypeStruct((N_HEADS, V_DIM, s), BF16),
    ]
    out_specs = [
        row(CONV_CH),
        pl.BlockSpec((N_HEADS, rows, HEAD_PAD), lambda i: (0, i, 0)),
        pl.BlockSpec((N_HEADS, rows, HEAD_PAD), lambda i: (0, i, 0)),
        pl.BlockSpec((N_HEADS, V_DIM, rows), lambda i: (0, 0, i)),
    ]
    return pl.pallas_call(
        _proj_kernel,
        grid=(s // rows,),
        in_specs=in_specs,
        out_specs=out_specs,
        out_shape=out_shape,
        compiler_params=_params(("parallel",)),
        name="proj",
    )(x1, pos, mixg, win, qg, wuq, kvg, wuk, wuv, qhg, khg, invf)


def _attn_kernel(q_ref, k_ref, vt_ref, o_ref, sa_ref, sb_ref, m_ref, l_ref, acc_ref, *, tk):
    nk = k_ref.shape[0] // tk
    m_ref[...] = jnp.full(m_ref.shape, -jnp.inf, F32)
    l_ref[...] = jnp.zeros_like(l_ref)
    acc_ref[...] = jnp.zeros_like(acc_ref)
    q = q_ref[...]

    def scores(c):
        off = pl.multiple_of(c * tk, tk)
        return lax.dot_general(k_ref[pl.ds(off, tk), :], q, (((1,), (1,)), ((), ())),
                               preferred_element_type=F32)

    def update(s_ref, c):
        st = s_ref[...]
        m_prev = m_ref[...]
        m_new = jnp.maximum(m_prev, jnp.max(st, axis=0, keepdims=True))
        alpha = jnp.exp2(m_prev - m_new)
        p = jnp.exp2(st - m_new)
        l_ref[...] = alpha * l_ref[...] + jnp.sum(p, axis=0, keepdims=True)
        off = pl.multiple_of(c * tk, tk)
        pv = jnp.dot(vt_ref[:, pl.ds(off, tk)], p.astype(BF16), preferred_element_type=F32)
        acc_ref[...] = alpha * acc_ref[...] + pv
        m_ref[...] = m_new

    sa_ref[...] = scores(0)

    def body(jj, carry):
        c0 = 2 * jj
        sb_ref[...] = scores(c0 + 1)
        update(sa_ref, c0)
        sa_ref[...] = scores(jnp.minimum(c0 + 2, nk - 1))
        update(sb_ref, c0 + 1)
        return carry

    lax.fori_loop(0, nk // 2, body, 0)
    o = acc_ref[...] * (1.0 / l_ref[...])
    o_ref[...] = o.T.astype(o_ref.dtype)


def _attn(q, k, vt, *, tq, tk):
    h, s, dp = q.shape
    assert (s // tk) % 2 == 0
    return pl.pallas_call(
        functools.partial(_attn_kernel, tk=tk),
        grid=(h, s // tq),
        in_specs=[
            pl.BlockSpec((None, tq, dp), lambda hh, i: (hh, i, 0)),
            pl.BlockSpec((None, s, dp), lambda hh, i: (hh, 0, 0)),
            pl.BlockSpec((None, V_DIM, s), lambda hh, i: (hh, 0, 0)),
        ],
        out_specs=pl.BlockSpec((tq, V_DIM), lambda hh, i: (i, hh)),
        out_shape=jax.ShapeDtypeStruct((s, h * V_DIM), BF16),
        scratch_shapes=[pltpu.VMEM((tk, tq), F32), pltpu.VMEM((tk, tq), F32),
                        pltpu.VMEM((1, tq), F32), pltpu.VMEM((1, tq), F32),
                        pltpu.VMEM((V_DIM, tq), F32)],
        compiler_params=_params(("parallel", "arbitrary")),
        name="attn",
    )(q, k, vt)


def _conv_kernel(prev_ref, u_ref, next_ref, w_ref, b_ref, g_ref, o_ref, buf_ref, y_ref, *, sub):
    i = pl.program_id(0)
    rows = u_ref.shape[0]
    buf_ref[:HALO, :] = jnp.where(i > 0, prev_ref[...], 0.0)
    buf_ref[HALO:HALO + rows, :] = u_ref[...]
    buf_ref[HALO + rows:, :] = jnp.where(i < pl.num_programs(0) - 1, next_ref[...], 0.0)
    base = HALO - CONV_PAD
    for r0 in range(0, rows, sub):
        for c in range(0, CONV_CH, LANES):
            acc = jnp.broadcast_to(b_ref[:, c:c + LANES], (sub, LANES))
            for t in range(CONV_WIDTH):
                acc = acc + buf_ref[r0 + base + t:r0 + base + t + sub, c:c + LANES] * w_ref[t:t + 1, c:c + LANES]
            y_ref[r0:r0 + sub, c:c + LANES] = acc
    y = _rms(y_ref[...], g_ref[...])
    o_ref[...] = (y * jax.nn.sigmoid(y)).astype(o_ref.dtype)


def _conv(u, w, b, g, *, rows, sub):
    s, c = u.shape
    hb = rows // HALO
    last = s // HALO - 1
    return pl.pallas_call(
        functools.partial(_conv_kernel, sub=sub),
        grid=(s // rows,),
        in_specs=[
            pl.BlockSpec((HALO, c), lambda i: (jnp.maximum(i * hb - 1, 0), 0)),
            pl.BlockSpec((rows, c), lambda i: (i, 0)),
            pl.BlockSpec((HALO, c), lambda i: (jnp.minimum((i + 1) * hb, last), 0)),
            _resident(w.shape), _resident(b.shape), _resident(g.shape),
        ],
        out_specs=pl.BlockSpec((rows, c), lambda i: (i, 0)),
        out_shape=jax.ShapeDtypeStruct((s, c), BF16),
        scratch_shapes=[pltpu.VMEM((rows + 2 * HALO, c), F32), pltpu.VMEM((rows, c), F32)],
        compiler_params=_params(("parallel",)),
        name="conv",
    )(u, u, u, w, b, g)


def _outproj_kernel(x_ref, a_ref, b_ref, wa_ref, wb_ref, o_ref):
    o_ref[...] = (x_ref[...]
                  + jnp.dot(a_ref[...], wa_ref[...], preferred_element_type=F32)
                  + jnp.dot(b_ref[...], wb_ref[...], preferred_element_type=F32))


def _outproj(x1, a, b, wa, wb, *, rows):
    s, d = x1.shape
    row = lambda w: pl.BlockSpec((rows, w), lambda i: (i, 0))
    return pl.pallas_call(
        _outproj_kernel,
        grid=(s // rows,),
        in_specs=[row(d), row(a.shape[1]), row(b.shape[1]), _resident(wa.shape), _resident(wb.shape)],
        out_specs=row(d),
        out_shape=jax.ShapeDtypeStruct((s, d), F32),
        compiler_params=_params(("parallel",)),
        name="outproj",
    )(x1, a, b, wa, wb)


def _pad_cols(w, n):
    return jnp.pad(w, ((0, 0), (0, n - w.shape[1])))


def _ffn_weights(wg, wu, wd):
    f = wg.shape[1]
    fp = -(-f // FF_TILE) * FF_TILE
    return (_pad_cols(wg, fp).astype(BF16), _pad_cols(wu, fp).astype(BF16),
            jnp.pad(wd, ((0, fp - f), (0, 0))).astype(BF16))


def _head_gain(g):
    return jnp.pad(g, (0, HEAD_PAD - QK_DIM)).reshape(1, HEAD_PAD)


def _layer(x, positions, tiles, ffn1_norm, ffn1_w_gate, ffn1_w_up, ffn1_w_down, mix_norm, w_in,
           q_norm, w_uq, kv_norm, w_ukv, q_head_norm, k_head_norm, dw_kernel, dw_bias, conv_norm,
           w_out, ffn2_norm, ffn2_w_gate, ffn2_w_up, ffn2_w_down, final_norm):
    s, d = x.shape
    row = lambda g: g.reshape(1, -1)

    x1 = _ffn(x, row(ffn1_norm), *_ffn_weights(ffn1_w_gate, ffn1_w_up, ffn1_w_down), None,
              rows=tiles["ffn"])

    c2 = 2 * CONV_CH + Q_RANK + KV_RANK
    win = _pad_cols(w_in, c2 + LANES).astype(BF16)
    wuq = jnp.pad(w_uq.reshape(Q_RANK, N_HEADS, QK_DIM),
                  ((0, 0), (0, 0), (0, HEAD_PAD - QK_DIM))).reshape(Q_RANK, N_HEADS * HEAD_PAD)
    wkv = w_ukv.reshape(KV_RANK, N_HEADS, NOPE + V_DIM)
    wuk = wkv[:, :, :NOPE].reshape(KV_RANK, N_HEADS * NOPE)
    wuv = wkv[:, :, NOPE:].reshape(KV_RANK, N_HEADS * V_DIM)
    half = ROPE // 2
    inv_freq = ROPE_THETA ** (-(jnp.arange(half, dtype=F32) * 2.0 / ROPE))
    invf = jnp.pad(jnp.concatenate([inv_freq, inv_freq]), (0, LANES - ROPE)).reshape(1, LANES)
    u, q, k, vt = _proj(x1, positions.reshape(s, 1), row(mix_norm), win, row(q_norm),
                        wuq.astype(BF16), row(kv_norm), wuk.astype(BF16), wuv.astype(BF16),
                        _head_gain(q_head_norm), _head_gain(k_head_norm), invf, rows=tiles["proj"])

    b_out = _attn(q, k, vt, tq=tiles["attn_q"], tk=tiles["attn_k"])
    a_out = _conv(u, jnp.pad(dw_kernel, ((0, 1), (0, 0))), row(dw_bias), row(conv_norm),
                  rows=tiles["conv"], sub=tiles["conv_sub"])
    wo = w_out.astype(BF16)
    x2 = _outproj(x1, a_out, b_out, wo[:CONV_CH], wo[CONV_CH:], rows=tiles["out"])

    return _ffn(x2, row(ffn2_norm), *_ffn_weights(ffn2_w_gate, ffn2_w_up, ffn2_w_down),
                row(final_norm), rows=tiles["ffn"])


def _tiles(s):
    return {
        "ffn": min(FFN_ROWS, s), "proj": min(PROJ_ROWS, s), "attn_q": min(ATTN_Q, s),
        "attn_k": min(ATTN_K, s), "conv": min(CONV_ROWS, s), "conv_sub": CONV_SUB,
        "out": min(OUT_ROWS, s),
    }


def kernel(x, positions, ffn1_norm, ffn1_w_gate, ffn1_w_up, ffn1_w_down, mix_norm, w_in, q_norm, w_uq, kv_norm, w_ukv, q_head_norm, k_head_norm, dw_kernel, dw_bias, conv_norm, w_out, ffn2_norm, ffn2_w_gate, ffn2_w_up, ffn2_w_down, final_norm):
    batch, s, d = x.shape
    assert ffn1_norm.shape[0] == 1, "single-layer stack expected"
    outs = []
    for bi in range(batch):
        outs.append(_layer(
            x[bi], positions[bi], _tiles(s), ffn1_norm[0], ffn1_w_gate[0], ffn1_w_up[0],
            ffn1_w_down[0], mix_norm[0], w_in[0], q_norm[0], w_uq[0], kv_norm[0], w_ukv[0],
            q_head_norm[0], k_head_norm[0], dw_kernel[0], dw_bias[0], conv_norm[0], w_out[0],
            ffn2_norm[0], ffn2_w_gate[0], ffn2_w_up[0], ffn2_w_down[0], final_norm[0]))
    return jnp.stack(outs)
```

```python
import functools
import math

import jax
import jax.numpy as jnp
from jax import lax
from jax.experimental import pallas as pl
from jax.experimental.pallas import tpu as pltpu

F32 = jnp.float32
BF16 = jnp.bfloat16

EPS = 1e-6
ROPE_THETA = 10000.0
CONV_CH = 1024
CONV_WIDTH = 31
CONV_PAD = CONV_WIDTH // 2
N_HEADS = 8
NOPE = 128
ROPE = 64
QK_DIM = NOPE + ROPE
V_DIM = 128
Q_RANK = 512
KV_RANK = 256

LANES = 128
SUBLANES_BF16 = 16
HEAD_PAD = 2 * LANES
V_PAD = V_DIM + SUBLANES_BF16
VMEM_LIMIT = 56 * 1024 * 1024

FF_TILE = 512
FFN_ROWS = 512
PROJ_ROWS = 256
ATTN_Q = 1024
ATTN_K = 512
CONV_ROWS = 256
CONV_SUB = 64
OUT_ROWS = 512
HALO = 16
MAX_JUMP = 64.0


def _rms(x, g):
    ms = jnp.mean(x * x, axis=-1, keepdims=True)
    return x * lax.rsqrt(ms + EPS) * g


def _params(sem):
    return pltpu.CompilerParams(dimension_semantics=sem, vmem_limit_bytes=VMEM_LIMIT)


def _resident(shape):
    nd = len(shape)
    return pl.BlockSpec(shape, lambda *_: (0,) * nd, pipeline_mode=pl.Buffered(1))


def _ffn_kernel(*refs, final_norm):
    if final_norm:
        x_ref, g_ref, wg_ref, wu_ref, wd_ref, fg_ref, o_ref, xn_ref = refs
    else:
        x_ref, g_ref, wg_ref, wu_ref, wd_ref, o_ref, xn_ref = refs
    f = pl.program_id(1)

    @pl.when(f == 0)
    def _():
        xn_ref[...] = _rms(x_ref[...], g_ref[...]).astype(BF16)
        o_ref[...] = jnp.zeros_like(o_ref)

    xn = xn_ref[...]
    gate = jnp.dot(xn, wg_ref[...], preferred_element_type=F32)
    up = jnp.dot(xn, wu_ref[...], preferred_element_type=F32)
    h = (gate * jax.nn.sigmoid(gate) * up).astype(BF16)
    o_ref[...] += jnp.dot(h, wd_ref[...], preferred_element_type=F32)

    @pl.when(f == pl.num_programs(1) - 1)
    def _():
        y = x_ref[...] + 0.5 * o_ref[...]
        if final_norm:
            y = _rms(y, fg_ref[...])
        o_ref[...] = y


def _ffn(x, g, wg, wu, wd, final_g, *, rows):
    s, d = x.shape
    fp = wg.shape[1]
    tf = min(FF_TILE, fp)
    in_specs = [
        pl.BlockSpec((rows, d), lambda i, f: (i, 0)),
        pl.BlockSpec((1, d), lambda i, f: (0, 0)),
        pl.BlockSpec((d, tf), lambda i, f: (0, f)),
        pl.BlockSpec((d, tf), lambda i, f: (0, f)),
        pl.BlockSpec((tf, d), lambda i, f: (f, 0)),
    ]
    args = [x, g, wg, wu, wd]
    if final_g is not None:
        in_specs.append(pl.BlockSpec((1, d), lambda i, f: (0, 0)))
        args.append(final_g)
    return pl.pallas_call(
        functools.partial(_ffn_kernel, final_norm=final_g is not None),
        grid=(s // rows, fp // tf),
        in_specs=in_specs,
        out_specs=pl.BlockSpec((rows, d), lambda i, f: (i, 0)),
        out_shape=jax.ShapeDtypeStruct((s, d), F32),
        scratch_shapes=[pltpu.VMEM((rows, d), BF16)],
        compiler_params=_params(("parallel", "arbitrary")),
        name="ffn_final" if final_g is not None else "ffn",
    )(*args)


def _proj_kernel(x_ref, pos_ref, mixg_ref, win_ref, qg_ref, wuq_ref, kvg_ref, wuk_ref, wuv_ref,
                 qhg_ref, khg_ref, invf_ref, u_ref, qt_ref, k_ref, vt_ref):
    hn = _rms(x_ref[...], mixg_ref[...]).astype(BF16)
    z = jnp.dot(hn, win_ref[...], preferred_element_type=F32)
    c0 = 2 * CONV_CH
    c1 = c0 + Q_RANK
    c2 = c1 + KV_RANK
    u_ref[...] = z[:, :CONV_CH] * jax.nn.sigmoid(z[:, CONV_CH:c0])
    qn = _rms(z[:, c0:c1], qg_ref[...]).astype(BF16)
    kvn = _rms(z[:, c1:c2], kvg_ref[...]).astype(BF16)
    kpe = z[:, c2:c2 + LANES]
    q = jnp.dot(qn, wuq_ref[...], preferred_element_type=F32)
    kn = jnp.dot(kvn, wuk_ref[...], preferred_element_type=F32)
    v = jnp.dot(kvn, wuv_ref[...], preferred_element_type=F32)

    ang = pos_ref[...].astype(F32) * invf_ref[...]
    cos_t = jnp.cos(ang)
    sin_t = jnp.sin(ang)
    lane = lax.broadcasted_iota(jnp.int32, ang.shape, 1)
    half = ROPE // 2
    sin_lo = jnp.where(lane < half, -sin_t, 0.0)
    sin_hi = jnp.where((lane >= half) & (lane < ROPE), sin_t, 0.0)

    def rope(t):
        return (t * cos_t + pltpu.roll(t, LANES - half, 1) * sin_lo
                + pltpu.roll(t, half, 1) * sin_hi)

    qhg = qhg_ref[...]
    khg = khg_ref[...]
    inv_d = 1.0 / QK_DIM
    scale = math.log2(math.e) / math.sqrt(QK_DIM)
    kpe_ss = jnp.sum(kpe * kpe, axis=-1, keepdims=True)
    kpe_rot = rope(kpe * khg[:, NOPE:])
    one_col = jnp.where(lane == ROPE, 1.0, 0.0)
    rows = x_ref.shape[0]
    ones_rows = jnp.where(lax.broadcasted_iota(jnp.int32, (V_PAD - V_DIM, rows), 0) == 0, 1.0, 0.0)
    for h in range(N_HEADS):
        qa = q[:, h * HEAD_PAD:h * HEAD_PAD + NOPE]
        qb = q[:, h * HEAD_PAD + NOPE:(h + 1) * HEAD_PAD]
        ss = jnp.sum(qa * qa, axis=-1, keepdims=True) + jnp.sum(qb * qb, axis=-1, keepdims=True)
        r = lax.rsqrt(ss * inv_d + EPS) * scale
        qt_ref[h, :NOPE, :] = (qa * r * qhg[:, :NOPE]).T.astype(BF16)
        qt_ref[h, NOPE:, :] = (rope(qb * qhg[:, NOPE:]) * r).T.astype(BF16)
        ka = kn[:, h * NOPE:(h + 1) * NOPE]
        rk = lax.rsqrt((jnp.sum(ka * ka, axis=-1, keepdims=True) + kpe_ss) * inv_d + EPS)
        k_ref[h, :, :NOPE] = (ka * rk * khg[:, :NOPE]).astype(BF16)
        k_ref[h, :, NOPE:] = (kpe_rot * rk + one_col).astype(BF16)
        vt_ref[h, :V_DIM, :] = v[:, h * V_DIM:(h + 1) * V_DIM].T.astype(BF16)
        vt_ref[h, V_DIM:, :] = ones_rows.astype(BF16)


def _proj(x1, pos, mixg, win, qg, wuq, kvg, wuk, wuv, qhg, khg, invf, *, rows):
    s, d = x1.shape
    row = lambda w: pl.BlockSpec((rows, w), lambda i: (i, 0))
    in_specs = [row(d), row(1), _resident(mixg.shape), _resident(win.shape), _resident(qg.shape),
                _resident(wuq.shape), _resident(kvg.shape), _resident(wuk.shape),
                _resident(wuv.shape), _resident(qhg.shape), _resident(khg.shape),
                _resident(invf.shape)]
    out_shape = [
        jax.ShapeDtypeStruct((s, CONV_CH), F32),
        jax.ShapeDtypeStruct((N_HEADS, HEAD_PAD, s), BF16),
        jax.ShapeDtypeStruct((N_HEADS, s, HEAD_PAD), BF16),
        jax.ShapeDtypeStruct((N_HEADS, V_PAD, s), BF16),
    ]
    out_specs = [
        row(CONV_CH),
        pl.BlockSpec((N_HEADS, HEAD_PAD, rows), lambda i: (0, 0, i)),
        pl.BlockSpec((N_HEADS, rows, HEAD_PAD), lambda i: (0, i, 0)),
        pl.BlockSpec((N_HEADS, V_PAD, rows), lambda i: (0, 0, i)),
    ]
    return pl.pallas_call(
        _proj_kernel,
        grid=(s // rows,),
        in_specs=in_specs,
        out_specs=out_specs,
        out_shape=out_shape,
        compiler_params=_params(("parallel",)),
        name="proj",
    )(x1, pos, mixg, win, qg, wuq, kvg, wuk, wuv, qhg, khg, invf)


def _attn_kernel(qt_ref, k_ref, vt_ref, o_ref, qa_ref, p0_ref, p1_ref, r_ref, pm_ref, a_ref,
                 bad_ref, acc_ref, *, tk):
    nk = k_ref.shape[0] // tk
    tq = qt_ref.shape[1]
    p_refs = (p0_ref, p1_ref)
    first_row = lax.broadcasted_iota(jnp.int32, (SUBLANES_BF16, tq), 0) == 0

    def set_reference(r):
        qa_ref[QK_DIM:QK_DIM + SUBLANES_BF16, :] = jnp.where(first_row, -r, 0.0).astype(BF16)

    def scores(c):
        off = pl.multiple_of(c * tk, tk)
        return jnp.dot(k_ref[pl.ds(off, tk), :], qa_ref[...], preferred_element_type=F32)

    def values(c, slot, alpha):
        off = pl.multiple_of(c * tk, tk)
        pv = jnp.dot(vt_ref[:, pl.ds(off, tk)], p_refs[slot][...], preferred_element_type=F32)
        acc_ref[...] = alpha * acc_ref[...] + pv

    def emit(p, slot):
        pb = p.astype(BF16)
        p_refs[slot][...] = pb
        return jnp.max(pb, axis=0, keepdims=True).astype(F32)

    def step(c, slot):
        r_prev = r_ref[...]
        pm_prev = pm_ref[...]
        r = jnp.maximum(r_prev, r_prev + jnp.log2(pm_prev)).astype(BF16).astype(F32)
        alpha = jnp.exp2(r_prev - r)
        set_reference(r)
        pm_ref[...] = emit(jnp.exp2(scores(c)), slot)
        values(c - 1, 1 - slot, a_ref[...])
        bad_ref[...] = jnp.maximum(bad_ref[...], jnp.where(pm_prev <= 2.0 ** MAX_JUMP, 0.0, 1.0))
        r_ref[...] = r
        a_ref[...] = alpha

    qa_ref[...] = qt_ref[...]
    st = scores(0)
    r = jnp.max(st, axis=0, keepdims=True).astype(BF16).astype(F32)
    pm_ref[...] = emit(jnp.exp2(st - r), 0)
    r_ref[...] = r
    a_ref[...] = jnp.zeros_like(a_ref)
    bad_ref[...] = jnp.zeros_like(bad_ref)
    acc_ref[...] = jnp.zeros_like(acc_ref)

    def body(t, carry):
        c = 2 * t + 1
        step(c, 1)
        step(c + 1, 0)
        return carry

    lax.fori_loop(0, (nk - 2) // 2, body, 0)
    step(nk - 1, 1)
    values(nk - 1, 1, a_ref[...])
    bad = jnp.maximum(bad_ref[...], jnp.where(pm_ref[...] <= 2.0 ** MAX_JUMP, 0.0, 1.0))

    @pl.when(jnp.max(bad) > 0.0)
    def _():
        set_reference(jnp.zeros_like(bad))
        r_ref[...] = jnp.full(r_ref.shape, -jnp.inf, F32)
        acc_ref[...] = jnp.zeros_like(acc_ref)

        def exact(c, carry):
            st = scores(c)
            m_prev = r_ref[...]
            m_new = jnp.maximum(m_prev, jnp.max(st, axis=0, keepdims=True))
            p0_ref[...] = jnp.exp2(st - m_new).astype(BF16)
            values(c, 0, jnp.exp2(m_prev - m_new))
            r_ref[...] = m_new
            return carry

        lax.fori_loop(0, nk, exact, 0)

    o = acc_ref[:V_DIM, :] * (1.0 / acc_ref[V_DIM:V_DIM + 1, :])
    o_ref[...] = o.T.astype(o_ref.dtype)


def _attn(qt, k, vt, *, tq, tk):
    h, s, dp = k.shape
    assert (s // tk) % 2 == 0
    stat = pltpu.VMEM((1, tq), F32)
    return pl.pallas_call(
        functools.partial(_attn_kernel, tk=tk),
        grid=(h, s // tq),
        in_specs=[
            pl.BlockSpec((None, dp, tq), lambda hh, i: (hh, 0, i)),
            pl.BlockSpec((None, s, dp), lambda hh, i: (hh, 0, 0)),
            pl.BlockSpec((None, V_PAD, s), lambda hh, i: (hh, 0, 0)),
        ],
        out_specs=pl.BlockSpec((tq, V_DIM), lambda hh, i: (i, hh)),
        out_shape=jax.ShapeDtypeStruct((s, h * V_DIM), BF16),
        scratch_shapes=[pltpu.VMEM((dp, tq), BF16),
                        pltpu.VMEM((tk, tq), BF16), pltpu.VMEM((tk, tq), BF16),
                        stat, stat, stat, stat, pltpu.VMEM((V_PAD, tq), F32)],
        compiler_params=_params(("parallel", "arbitrary")),
        name="attn",
    )(qt, k, vt)


def _conv_kernel(prev_ref, u_ref, next_ref, w_ref, b_ref, g_ref, o_ref, buf_ref, y_ref, *, sub):
    i = pl.program_id(0)
    rows = u_ref.shape[0]
    buf_ref[:HALO, :] = jnp.where(i > 0, prev_ref[...], 0.0)
    buf_ref[HALO:HALO + rows, :] = u_ref[...]
    buf_ref[HALO + rows:, :] = jnp.where(i < pl.num_programs(0) - 1, next_ref[...], 0.0)
    base = HALO - CONV_PAD
    for r0 in range(0, rows, sub):
        for c in range(0, CONV_CH, LANES):
            acc = jnp.broadcast_to(b_ref[:, c:c + LANES], (sub, LANES))
            for t in range(CONV_WIDTH):
                acc = acc + buf_ref[r0 + base + t:r0 + base + t + sub, c:c + LANES] * w_ref[t:t + 1, c:c + LANES]
            y_ref[r0:r0 + sub, c:c + LANES] = acc
    y = _rms(y_ref[...], g_ref[...])
    o_ref[...] = (y * jax.nn.sigmoid(y)).astype(o_ref.dtype)


def _conv(u, w, b, g, *, rows, sub):
    s, c = u.shape
    hb = rows // HALO
    last = s // HALO - 1
    return pl.pallas_call(
        functools.partial(_conv_kernel, sub=sub),
        grid=(s // rows,),
        in_specs=[
            pl.BlockSpec((HALO, c), lambda i: (jnp.maximum(i * hb - 1, 0), 0)),
            pl.BlockSpec((rows, c), lambda i: (i, 0)),
            pl.BlockSpec((HALO, c), lambda i: (jnp.minimum((i + 1) * hb, last), 0)),
            _resident(w.shape), _resident(b.shape), _resident(g.shape),
        ],
        out_specs=pl.BlockSpec((rows, c), lambda i: (i, 0)),
        out_shape=jax.ShapeDtypeStruct((s, c), BF16),
        scratch_shapes=[pltpu.VMEM((rows + 2 * HALO, c), F32), pltpu.VMEM((rows, c), F32)],
        compiler_params=_params(("parallel",)),
        name="conv",
    )(u, u, u, w, b, g)


def _outproj_kernel(x_ref, a_ref, b_ref, wa_ref, wb_ref, o_ref):
    o_ref[...] = (x_ref[...]
                  + jnp.dot(a_ref[...], wa_ref[...], preferred_element_type=F32)
                  + jnp.dot(b_ref[...], wb_ref[...], preferred_element_type=F32))


def _outproj(x1, a, b, wa, wb, *, rows):
    s, d = x1.shape
    row = lambda w: pl.BlockSpec((rows, w), lambda i: (i, 0))
    return pl.pallas_call(
        _outproj_kernel,
        grid=(s // rows,),
        in_specs=[row(d), row(a.shape[1]), row(b.shape[1]), _resident(wa.shape), _resident(wb.shape)],
        out_specs=row(d),
        out_shape=jax.ShapeDtypeStruct((s, d), F32),
        compiler_params=_params(("parallel",)),
        name="outproj",
    )(x1, a, b, wa, wb)


def _pad_cols(w, n):
    return jnp.pad(w, ((0, 0), (0, n - w.shape[1])))


def _ffn_weights(wg, wu, wd):
    f = wg.shape[1]
    fp = -(-f // FF_TILE) * FF_TILE
    return (_pad_cols(wg.astype(BF16), fp), _pad_cols(wu.astype(BF16), fp),
            jnp.pad(wd.astype(BF16), ((0, fp - f), (0, 0))))


def _head_gain(g):
    return jnp.pad(g, (0, HEAD_PAD - QK_DIM)).reshape(1, HEAD_PAD)


def _layer(x, positions, tiles, ffn1_norm, ffn1_w_gate, ffn1_w_up, ffn1_w_down, mix_norm, w_in,
           q_norm, w_uq, kv_norm, w_ukv, q_head_norm, k_head_norm, dw_kernel, dw_bias, conv_norm,
           w_out, ffn2_norm, ffn2_w_gate, ffn2_w_up, ffn2_w_down, final_norm):
    s, d = x.shape
    row = lambda g: g.reshape(1, -1)

    x1 = _ffn(x, row(ffn1_norm), *_ffn_weights(ffn1_w_gate, ffn1_w_up, ffn1_w_down), None,
              rows=tiles["ffn"])

    c2 = 2 * CONV_CH + Q_RANK + KV_RANK
    win = _pad_cols(w_in.astype(BF16), c2 + LANES)
    wuq = jnp.pad(w_uq.reshape(Q_RANK, N_HEADS, QK_DIM),
                  ((0, 0), (0, 0), (0, HEAD_PAD - QK_DIM))).reshape(Q_RANK, N_HEADS * HEAD_PAD)
    wkv = w_ukv.reshape(KV_RANK, N_HEADS, NOPE + V_DIM)
    wuk = wkv[:, :, :NOPE].reshape(KV_RANK, N_HEADS * NOPE)
    wuv = wkv[:, :, NOPE:].reshape(KV_RANK, N_HEADS * V_DIM)
    half = ROPE // 2
    inv_freq = ROPE_THETA ** (-(jnp.arange(half, dtype=F32) * 2.0 / ROPE))
    invf = jnp.pad(jnp.concatenate([inv_freq, inv_freq]), (0, LANES - ROPE)).reshape(1, LANES)
    u, q, k, vt = _proj(x1, positions.reshape(s, 1), row(mix_norm), win, row(q_norm),
                        wuq.astype(BF16), row(kv_norm), wuk.astype(BF16), wuv.astype(BF16),
                        _head_gain(q_head_norm), _head_gain(k_head_norm), invf, rows=tiles["proj"])

    b_out = _attn(q, k, vt, tq=tiles["attn_q"], tk=tiles["attn_k"])
    a_out = _conv(u, jnp.pad(dw_kernel, ((0, 1), (0, 0))), row(dw_bias), row(conv_norm),
                  rows=tiles["conv"], sub=tiles["conv_sub"])
    wo = w_out.astype(BF16)
    x2 = _outproj(x1, a_out, b_out, wo[:CONV_CH], wo[CONV_CH:], rows=tiles["out"])

    return _ffn(x2, row(ffn2_norm), *_ffn_weights(ffn2_w_gate, ffn2_w_up, ffn2_w_down),
                row(final_norm), rows=tiles["ffn"])


def _tiles(s):
    return {
        "ffn": min(FFN_ROWS, s), "proj": min(PROJ_ROWS, s), "attn_q": min(ATTN_Q, s),
        "attn_k": min(ATTN_K, s), "conv": min(CONV_ROWS, s), "conv_sub": CONV_SUB,
        "out": min(OUT_ROWS, s),
    }


def kernel(x, positions, ffn1_norm, ffn1_w_gate, ffn1_w_up, ffn1_w_down, mix_norm, w_in, q_norm, w_uq, kv_norm, w_ukv, q_head_norm, k_head_norm, dw_kernel, dw_bias, conv_norm, w_out, ffn2_norm, ffn2_w_gate, ffn2_w_up, ffn2_w_down, final_norm):
    batch, s, d = x.shape
    assert ffn1_norm.shape[0] == 1, "single-layer stack expected"
    outs = []
    for bi in range(batch):
        outs.append(_layer(
            x[bi], positions[bi], _tiles(s), ffn1_norm[0], ffn1_w_gate[0], ffn1_w_up[0],
            ffn1_w_down[0], mix_norm[0], w_in[0], q_norm[0], w_uq[0], kv_norm[0], w_ukv[0],
            q_head_norm[0], k_head_norm[0], dw_kernel[0], dw_bias[0], conv_norm[0], w_out[0],
            ffn2_norm[0], ffn2_w_gate[0], ffn2_w_up[0], ffn2_w_down[0], final_norm[0]))
    return jnp.stack(outs)
```

```python
import functools
import math

import jax
import jax.numpy as jnp
from jax import lax
from jax.experimental import pallas as pl
from jax.experimental.pallas import tpu as pltpu

F32 = jnp.float32
BF16 = jnp.bfloat16

EPS = 1e-6
ROPE_THETA = 10000.0
CONV_CH = 1024
CONV_WIDTH = 31
CONV_PAD = CONV_WIDTH // 2
N_HEADS = 8
NOPE = 128
ROPE = 64
QK_DIM = NOPE + ROPE
V_DIM = 128
Q_RANK = 512
KV_RANK = 256

LANES = 128
SUBLANES = 8
SUBLANES_BF16 = 16
HEAD_PAD = 2 * LANES
V_PAD = V_DIM + SUBLANES_BF16
VMEM_LIMIT = 56 * 1024 * 1024

FF_TILE = 512
FFN_ROWS = 512
PROJ_ROWS = 512
PROJ_SUB = 256
ATTN_Q = 1024
ATTN_K = 512
CONV_ROWS = 256
CONV_SUB = 256
OUT_ROWS = 512
HALO = 16
MAX_JUMP = 64.0


def _rms(x, g):
    ms = jnp.mean(x * x, axis=-1, keepdims=True)
    return x * lax.rsqrt(ms + EPS) * g


def _params(sem):
    return pltpu.CompilerParams(dimension_semantics=sem, vmem_limit_bytes=VMEM_LIMIT)


def _resident(shape):
    nd = len(shape)
    return pl.BlockSpec(shape, lambda *_: (0,) * nd, pipeline_mode=pl.Buffered(1))


def _ffn_kernel(*refs, final_norm):
    if final_norm:
        x_ref, g_ref, wg_ref, wu_ref, wd_ref, fg_ref, o_ref, xn_ref = refs
    else:
        x_ref, g_ref, wg_ref, wu_ref, wd_ref, o_ref, xn_ref = refs
    f = pl.program_id(1)

    @pl.when(f == 0)
    def _():
        xn_ref[...] = _rms(x_ref[...], g_ref[...]).astype(BF16)
        o_ref[...] = jnp.zeros_like(o_ref)

    xn = xn_ref[...]
    gate = jnp.dot(xn, wg_ref[...], preferred_element_type=F32)
    up = jnp.dot(xn, wu_ref[...], preferred_element_type=F32)
    h = (gate * jax.nn.sigmoid(gate) * up).astype(BF16)
    o_ref[...] += jnp.dot(h, wd_ref[...], preferred_element_type=F32)

    @pl.when(f == pl.num_programs(1) - 1)
    def _():
        y = x_ref[...] + 0.5 * o_ref[...]
        if final_norm:
            y = _rms(y, fg_ref[...])
        o_ref[...] = y


def _ffn(x, g, wg, wu, wd, final_g, *, rows):
    s, d = x.shape
    fp = wg.shape[1]
    tf = min(FF_TILE, fp)
    in_specs = [
        pl.BlockSpec((rows, d), lambda i, f: (i, 0)),
        pl.BlockSpec((1, d), lambda i, f: (0, 0)),
        pl.BlockSpec((d, tf), lambda i, f: (0, f)),
        pl.BlockSpec((d, tf), lambda i, f: (0, f)),
        pl.BlockSpec((tf, d), lambda i, f: (f, 0)),
    ]
    args = [x, g, wg, wu, wd]
    if final_g is not None:
        in_specs.append(pl.BlockSpec((1, d), lambda i, f: (0, 0)))
        args.append(final_g)
    return pl.pallas_call(
        functools.partial(_ffn_kernel, final_norm=final_g is not None),
        grid=(s // rows, fp // tf),
        in_specs=in_specs,
        out_specs=pl.BlockSpec((rows, d), lambda i, f: (i, 0)),
        out_shape=jax.ShapeDtypeStruct((s, d), F32),
        scratch_shapes=[pltpu.VMEM((rows, d), BF16)],
        compiler_params=_params(("parallel", "arbitrary")),
        name="ffn_final" if final_g is not None else "ffn",
    )(*args)


def _proj_kernel(x_ref, pos_ref, mixg_ref, win_ref, qg_ref, wuqt_ref, kvg_ref, wuk_ref, wuvt_ref,
                 qhgt_ref, khg_ref, invft_ref, u_ref, qt_ref, k_ref, vt_ref, *, sub):
    c0 = 2 * CONV_CH
    c1 = c0 + Q_RANK
    c2 = c1 + KV_RANK
    half = ROPE // 2
    inv_d = 1.0 / QK_DIM
    scale = math.log2(math.e) / math.sqrt(QK_DIM)
    qhgt = qhgt_ref[...]
    khg = khg_ref[...]
    lane = lax.broadcasted_iota(jnp.int32, (sub, LANES), 1)
    one_col = jnp.where(lane == ROPE, 1.0, 0.0)
    ones_rows = jnp.where(
        lax.broadcasted_iota(jnp.int32, (V_PAD - V_DIM, sub), 0) == 0, 1.0, 0.0).astype(BF16)
    q_tail = jnp.zeros((HEAD_PAD - QK_DIM, sub), BF16)

    def part(r0):
        rs = slice(r0, r0 + sub)
        hn = _rms(x_ref[rs, :], mixg_ref[...]).astype(BF16)
        z = jnp.dot(hn, win_ref[...], preferred_element_type=F32)
        u_ref[rs, :] = z[:, :CONV_CH] * jax.nn.sigmoid(z[:, CONV_CH:c0])
        qnt = _rms(z[:, c0:c1], qg_ref[...]).T.astype(BF16)
        kvn = _rms(z[:, c1:c2], kvg_ref[...])
        kvnt = kvn.T.astype(BF16)
        kpe = z[:, c2:c2 + LANES]
        qt = jnp.dot(wuqt_ref[...], qnt, preferred_element_type=F32)
        vt = jnp.dot(wuvt_ref[...], kvnt, preferred_element_type=F32)
        kn = jnp.dot(kvn.astype(BF16), wuk_ref[...], preferred_element_type=F32)

        ang = invft_ref[...] * pos_ref[:, rs].astype(F32)
        cos_t = jnp.cos(ang)
        sin_t = jnp.sin(ang)
        zeros = jnp.zeros((LANES - ROPE, sub), F32)
        cos_k = jnp.concatenate([cos_t, cos_t, zeros], axis=0).T
        sin_k = jnp.concatenate([-sin_t, sin_t, zeros], axis=0).T

        kpe_g = kpe * khg[:, NOPE:]
        kpe_rot = kpe_g * cos_k + jnp.where(lane < half, pltpu.roll(kpe_g, LANES - half, 1),
                                            pltpu.roll(kpe_g, half, 1)) * sin_k
        kpe_ss = jnp.sum(kpe * kpe, axis=-1, keepdims=True)
        for h in range(N_HEADS):
            qh = qt[h * HEAD_PAD:h * HEAD_PAD + QK_DIM, :]
            r = lax.rsqrt(jnp.sum(qh * qh, axis=0, keepdims=True) * inv_d + EPS) * scale
            qh = qh * qhgt[:QK_DIM, :] * r
            x1 = qh[NOPE:NOPE + half, :]
            x2 = qh[NOPE + half:, :]
            qt_ref[h, :NOPE, rs] = qh[:NOPE, :].astype(BF16)
            qt_ref[h, NOPE:NOPE + half, rs] = (x1 * cos_t - x2 * sin_t).astype(BF16)
            qt_ref[h, NOPE + half:QK_DIM, rs] = (x2 * cos_t + x1 * sin_t).astype(BF16)
            qt_ref[h, QK_DIM:, rs] = q_tail
            ka = kn[:, h * NOPE:(h + 1) * NOPE]
            rk = lax.rsqrt((jnp.sum(ka * ka, axis=-1, keepdims=True) + kpe_ss) * inv_d + EPS)
            k_ref[h, rs, :NOPE] = (ka * rk * khg[:, :NOPE]).astype(BF16)
            k_ref[h, rs, NOPE:] = (kpe_rot * rk + one_col).astype(BF16)
            vt_ref[h, :V_DIM, rs] = vt[h * V_DIM:(h + 1) * V_DIM, :].astype(BF16)
            vt_ref[h, V_DIM:, rs] = ones_rows

    for r0 in range(0, x_ref.shape[0], sub):
        part(r0)


def _proj(x1, pos, mixg, win, qg, wuqt, kvg, wuk, wuvt, qhg, khg, inv_freq, *, rows):
    s, d = x1.shape
    sub = min(PROJ_SUB, rows)
    qhgt = jnp.broadcast_to(qhg.reshape(HEAD_PAD, 1), (HEAD_PAD, sub))
    invft = jnp.broadcast_to(inv_freq.reshape(-1, 1), (ROPE // 2, sub))
    row = lambda w: pl.BlockSpec((rows, w), lambda i: (i, 0))
    in_specs = [row(d), pl.BlockSpec((1, rows), lambda i: (0, i)), _resident(mixg.shape),
                _resident(win.shape), _resident(qg.shape), _resident(wuqt.shape),
                _resident(kvg.shape), _resident(wuk.shape), _resident(wuvt.shape),
                _resident(qhgt.shape), _resident(khg.shape), _resident(invft.shape)]
    out_shape = [
        jax.ShapeDtypeStruct((s, CONV_CH), F32),
        jax.ShapeDtypeStruct((N_HEADS, HEAD_PAD, s), BF16),
        jax.ShapeDtypeStruct((N_HEADS, s, HEAD_PAD), BF16),
        jax.ShapeDtypeStruct((N_HEADS, V_PAD, s), BF16),
    ]
    out_specs = [
        row(CONV_CH),
        pl.BlockSpec((N_HEADS, HEAD_PAD, rows), lambda i: (0, 0, i)),
        pl.BlockSpec((N_HEADS, rows, HEAD_PAD), lambda i: (0, i, 0)),
        pl.BlockSpec((N_HEADS, V_PAD, rows), lambda i: (0, 0, i)),
    ]
    return pl.pallas_call(
        functools.partial(_proj_kernel, sub=sub),
        grid=(s // rows,),
        in_specs=in_specs,
        out_specs=out_specs,
        out_shape=out_shape,
        compiler_params=_params(("parallel",)),
        name="proj",
    )(x1, pos, mixg, win, qg, wuqt, kvg, wuk, wuvt, qhgt, khg, invft)


def _attn_kernel(qt_ref, k_ref, vt_ref, o_ref, qa_ref, p0_ref, p1_ref, r_ref, pm_ref, a_ref,
                 bad_ref, acc_ref, *, tk):
    nk = k_ref.shape[0] // tk
    tq = qt_ref.shape[1]
    p_refs = (p0_ref, p1_ref)
    first_row = lax.broadcasted_iota(jnp.int32, (SUBLANES_BF16, tq), 0) == 0

    def set_reference(r):
        qa_ref[QK_DIM:QK_DIM + SUBLANES_BF16, :] = jnp.where(first_row, -r, 0.0).astype(BF16)

    def scores(c):
        off = pl.multiple_of(c * tk, tk)
        return jnp.dot(k_ref[pl.ds(off, tk), :], qa_ref[...], preferred_element_type=F32)

    def values(c, slot, alpha):
        off = pl.multiple_of(c * tk, tk)
        pv = jnp.dot(vt_ref[:, pl.ds(off, tk)], p_refs[slot][...], preferred_element_type=F32)
        acc_ref[...] = alpha * acc_ref[...] + pv

    def emit(p, slot):
        pb = p.astype(BF16)
        p_refs[slot][...] = pb
        return jnp.max(pb, axis=0, keepdims=True).astype(F32)

    def step(c, slot):
        r_prev = r_ref[...]
        pm_prev = pm_ref[...]
        r = jnp.maximum(r_prev, r_prev + jnp.log2(pm_prev)).astype(BF16).astype(F32)
        alpha = jnp.exp2(r_prev - r)
        set_reference(r)
        pm_ref[...] = emit(jnp.exp2(scores(c)), slot)
        values(c - 1, 1 - slot, a_ref[...])
        bad_ref[...] = jnp.maximum(bad_ref[...], jnp.where(pm_prev <= 2.0 ** MAX_JUMP, 0.0, 1.0))
        r_ref[...] = r
        a_ref[...] = alpha

    qa_ref[...] = qt_ref[...]
    st = scores(0)
    r = jnp.max(st, axis=0, keepdims=True).astype(BF16).astype(F32)
    pm_ref[...] = emit(jnp.exp2(st - r), 0)
    r_ref[...] = r
    a_ref[...] = jnp.zeros_like(a_ref)
    bad_ref[...] = jnp.zeros_like(bad_ref)
    acc_ref[...] = jnp.zeros_like(acc_ref)

    def body(t, carry):
        c = 2 * t + 1
        step(c, 1)
        step(c + 1, 0)
        return carry

    lax.fori_loop(0, (nk - 2) // 2, body, 0)
    step(nk - 1, 1)
    values(nk - 1, 1, a_ref[...])
    bad = jnp.maximum(bad_ref[...], jnp.where(pm_ref[...] <= 2.0 ** MAX_JUMP, 0.0, 1.0))

    @pl.when(jnp.max(bad) > 0.0)
    def _():
        set_reference(jnp.zeros_like(bad))
        r_ref[...] = jnp.full(r_ref.shape, -jnp.inf, F32)
        acc_ref[...] = jnp.zeros_like(acc_ref)

        def exact(c, carry):
            st = scores(c)
            m_prev = r_ref[...]
            m_new = jnp.maximum(m_prev, jnp.max(st, axis=0, keepdims=True))
            p0_ref[...] = jnp.exp2(st - m_new).astype(BF16)
            values(c, 0, jnp.exp2(m_prev - m_new))
            r_ref[...] = m_new
            return carry

        lax.fori_loop(0, nk, exact, 0)

    o = acc_ref[:V_DIM, :] * (1.0 / acc_ref[V_DIM:V_DIM + 1, :])
    o_ref[...] = o.T.astype(o_ref.dtype)


def _attn(qt, k, vt, *, tq, tk):
    h, s, dp = k.shape
    assert (s // tk) % 2 == 0
    stat = pltpu.VMEM((1, tq), F32)
    return pl.pallas_call(
        functools.partial(_attn_kernel, tk=tk),
        grid=(h, s // tq),
        in_specs=[
            pl.BlockSpec((None, dp, tq), lambda hh, i: (hh, 0, i)),
            pl.BlockSpec((None, s, dp), lambda hh, i: (hh, 0, 0)),
            pl.BlockSpec((None, V_PAD, s), lambda hh, i: (hh, 0, 0)),
        ],
        out_specs=pl.BlockSpec((tq, V_DIM), lambda hh, i: (i, hh)),
        out_shape=jax.ShapeDtypeStruct((s, h * V_DIM), BF16),
        scratch_shapes=[pltpu.VMEM((dp, tq), BF16),
                        pltpu.VMEM((tk, tq), BF16), pltpu.VMEM((tk, tq), BF16),
                        stat, stat, stat, stat, pltpu.VMEM((V_PAD, tq), F32)],
        compiler_params=_params(("parallel", "arbitrary")),
        name="attn",
    )(qt, k, vt)


def _conv_kernel(prev_ref, u_ref, next_ref, w_ref, b_ref, g_ref, o_ref, sh_ref, y_ref, *, sub):
    i = pl.program_id(0)
    rows = u_ref.shape[0]
    sh_ref[0, :HALO, :] = jnp.where(i > 0, prev_ref[...], 0.0)
    sh_ref[0, HALO:HALO + rows, :] = u_ref[...]
    sh_ref[0, HALO + rows:, :] = jnp.where(i < pl.num_programs(0) - 1, next_ref[...], 0.0)
    n = rows + 2 * HALO - SUBLANES
    base = HALO - CONV_PAD

    def lane_block(ci, carry):
        cs = pl.ds(pl.multiple_of(ci * LANES, LANES), LANES)
        for b in range(1, SUBLANES):
            sh_ref[b, :n, cs] = sh_ref[0, b:b + n, cs]
        for r0 in range(0, rows, sub):
            acc = jnp.broadcast_to(b_ref[:, cs], (sub, LANES))
            for t in range(CONV_WIDTH):
                a, b = divmod(base + t, SUBLANES)
                lo = r0 + a * SUBLANES
                acc = acc + sh_ref[b, lo:lo + sub, cs] * w_ref[t:t + 1, cs]
            y_ref[r0:r0 + sub, cs] = acc
        return carry

    lax.fori_loop(0, CONV_CH // LANES, lane_block, 0)
    y = _rms(y_ref[...], g_ref[...])
    o_ref[...] = (y * jax.nn.sigmoid(y)).astype(o_ref.dtype)


def _conv(u, w, b, g, *, rows, sub):
    s, c = u.shape
    hb = rows // HALO
    last = s // HALO - 1
    return pl.pallas_call(
        functools.partial(_conv_kernel, sub=sub),
        grid=(s // rows,),
        in_specs=[
            pl.BlockSpec((HALO, c), lambda i: (jnp.maximum(i * hb - 1, 0), 0)),
            pl.BlockSpec((rows, c), lambda i: (i, 0)),
            pl.BlockSpec((HALO, c), lambda i: (jnp.minimum((i + 1) * hb, last), 0)),
            _resident(w.shape), _resident(b.shape), _resident(g.shape),
        ],
        out_specs=pl.BlockSpec((rows, c), lambda i: (i, 0)),
        out_shape=jax.ShapeDtypeStruct((s, c), BF16),
        scratch_shapes=[pltpu.VMEM((SUBLANES, rows + 2 * HALO, c), F32), pltpu.VMEM((rows, c), F32)],
        compiler_params=_params(("parallel",)),
        name="conv",
    )(u, u, u, w, b, g)


def _outproj_kernel(x_ref, a_ref, b_ref, wa_ref, wb_ref, o_ref):
    o_ref[...] = (x_ref[...]
                  + jnp.dot(a_ref[...], wa_ref[...], preferred_element_type=F32)
                  + jnp.dot(b_ref[...], wb_ref[...], preferred_element_type=F32))


def _outproj(x1, a, b, wa, wb, *, rows):
    s, d = x1.shape
    row = lambda w: pl.BlockSpec((rows, w), lambda i: (i, 0))
    return pl.pallas_call(
        _outproj_kernel,
        grid=(s // rows,),
        in_specs=[row(d), row(a.shape[1]), row(b.shape[1]), _resident(wa.shape), _resident(wb.shape)],
        out_specs=row(d),
        out_shape=jax.ShapeDtypeStruct((s, d), F32),
        compiler_params=_params(("parallel",)),
        name="outproj",
    )(x1, a, b, wa, wb)


def _pad_cols(w, n):
    return jnp.pad(w, ((0, 0), (0, n - w.shape[1])))


def _ffn_weights(wg, wu, wd):
    f = wg.shape[1]
    fp = -(-f // FF_TILE) * FF_TILE
    return (_pad_cols(wg.astype(BF16), fp), _pad_cols(wu.astype(BF16), fp),
            jnp.pad(wd.astype(BF16), ((0, fp - f), (0, 0))))


def _head_gain(g):
    return jnp.pad(g, (0, HEAD_PAD - QK_DIM)).reshape(1, HEAD_PAD)


def _layer(x, positions, tiles, ffn1_norm, ffn1_w_gate, ffn1_w_up, ffn1_w_down, mix_norm, w_in,
           q_norm, w_uq, kv_norm, w_ukv, q_head_norm, k_head_norm, dw_kernel, dw_bias, conv_norm,
           w_out, ffn2_norm, ffn2_w_gate, ffn2_w_up, ffn2_w_down, final_norm):
    s, d = x.shape
    row = lambda g: g.reshape(1, -1)

    x1 = _ffn(x, row(ffn1_norm), *_ffn_weights(ffn1_w_gate, ffn1_w_up, ffn1_w_down), None,
              rows=tiles["ffn"])

    c2 = 2 * CONV_CH + Q_RANK + KV_RANK
    win = _pad_cols(w_in.astype(BF16), c2 + LANES)
    wuq = jnp.pad(w_uq.reshape(Q_RANK, N_HEADS, QK_DIM),
                  ((0, 0), (0, 0), (0, HEAD_PAD - QK_DIM))).reshape(Q_RANK, N_HEADS * HEAD_PAD)
    wkv = w_ukv.reshape(KV_RANK, N_HEADS, NOPE + V_DIM)
    wuk = wkv[:, :, :NOPE].reshape(KV_RANK, N_HEADS * NOPE)
    wuv = wkv[:, :, NOPE:].reshape(KV_RANK, N_HEADS * V_DIM)
    inv_freq = ROPE_THETA ** (-(jnp.arange(ROPE // 2, dtype=F32) * 2.0 / ROPE))
    u, qt, k, vt = _proj(x1, positions.reshape(1, s), row(mix_norm), win, row(q_norm),
                         wuq.T.astype(BF16), row(kv_norm), wuk.astype(BF16), wuv.T.astype(BF16),
                         _head_gain(q_head_norm), _head_gain(k_head_norm), inv_freq,
                         rows=tiles["proj"])

    b_out = _attn(qt, k, vt, tq=tiles["attn_q"], tk=tiles["attn_k"])
    a_out = _conv(u, jnp.pad(dw_kernel, ((0, 1), (0, 0))), row(dw_bias), row(conv_norm),
                  rows=tiles["conv"], sub=tiles["conv_sub"])
    wo = w_out.astype(BF16)
    x2 = _outproj(x1, a_out, b_out, wo[:CONV_CH], wo[CONV_CH:], rows=tiles["out"])

    return _ffn(x2, row(ffn2_norm), *_ffn_weights(ffn2_w_gate, ffn2_w_up, ffn2_w_down),
                row(final_norm), rows=tiles["ffn"])


def _tiles(s):
    return {
        "ffn": min(FFN_ROWS, s), "proj": min(PROJ_ROWS, s), "attn_q": min(ATTN_Q, s),
        "attn_k": min(ATTN_K, s), "conv": min(CONV_ROWS, s), "conv_sub": CONV_SUB,
        "out": min(OUT_ROWS, s),
    }


def kernel(x, positions, ffn1_norm, ffn1_w_gate, ffn1_w_up, ffn1_w_down, mix_norm, w_in, q_norm, w_uq, kv_norm, w_ukv, q_head_norm, k_head_norm, dw_kernel, dw_bias, conv_norm, w_out, ffn2_norm, ffn2_w_gate, ffn2_w_up, ffn2_w_down, final_norm):
    batch, s, d = x.shape
    assert ffn1_norm.shape[0] == 1, "single-layer stack expected"
    outs = []
    for bi in range(batch):
        outs.append(_layer(
            x[bi], positions[bi], _tiles(s), ffn1_norm[0], ffn1_w_gate[0], ffn1_w_up[0],
            ffn1_w_down[0], mix_norm[0], w_in[0], q_norm[0], w_uq[0], kv_norm[0], w_ukv[0],
            q_head_norm[0], k_head_norm[0], dw_kernel[0], dw_bias[0], conv_norm[0], w_out[0],
            ffn2_norm[0], ffn2_w_gate[0], ffn2_w_up[0], ffn2_w_down[0], final_norm[0]))
    return jnp.stack(outs)
```

```python
import functools
import math

import jax
import jax.numpy as jnp
from jax import lax
from jax.experimental import pallas as pl
from jax.experimental.pallas import tpu as pltpu

F32 = jnp.float32
BF16 = jnp.bfloat16

EPS = 1e-6
ROPE_THETA = 10000.0
CONV_CH = 1024
CONV_WIDTH = 31
CONV_PAD = CONV_WIDTH // 2
N_HEADS = 8
NOPE = 128
ROPE = 64
QK_DIM = NOPE + ROPE
V_DIM = 128
Q_RANK = 512
KV_RANK = 256

LANES = 128
SUBLANES = 8
SUBLANES_BF16 = 16
HEAD_PAD = 2 * LANES
V_PAD = V_DIM + SUBLANES_BF16
VMEM_LIMIT = 56 * 1024 * 1024

FF_TILE = 512
FFN_ROWS = 512
PROJ_ROWS = 512
PROJ_SUB = 256
ATTN_Q = 1024
ATTN_K = 1024
CONV_ROWS = 256
CONV_SUB = 256
OUT_ROWS = 512
CAST_ROWS = 256
HALO = 16
MAX_JUMP = 64.0


def _rms(x, g):
    ms = jnp.mean(x * x, axis=-1, keepdims=True)
    return x * lax.rsqrt(ms + EPS) * g


def _params(sem):
    return pltpu.CompilerParams(dimension_semantics=sem, vmem_limit_bytes=VMEM_LIMIT)


def _resident(shape):
    nd = len(shape)
    return pl.BlockSpec(shape, lambda *_: (0,) * nd, pipeline_mode=pl.Buffered(1))


def _ffn_kernel(*refs, final_norm):
    if final_norm:
        x_ref, g_ref, wg_ref, wu_ref, wd_ref, fg_ref, o_ref, xn_ref = refs
    else:
        x_ref, g_ref, wg_ref, wu_ref, wd_ref, o_ref, xn_ref = refs
    f = pl.program_id(1)

    @pl.when(f == 0)
    def _():
        xn_ref[...] = _rms(x_ref[...], g_ref[...]).astype(BF16)
        o_ref[...] = jnp.zeros_like(o_ref)

    xn = xn_ref[...]
    gate = jnp.dot(xn, wg_ref[...], preferred_element_type=F32)
    up = jnp.dot(xn, wu_ref[...], preferred_element_type=F32)
    h = (gate * jax.nn.sigmoid(gate) * up).astype(BF16)
    o_ref[...] += jnp.dot(h, wd_ref[...], preferred_element_type=F32)

    @pl.when(f == pl.num_programs(1) - 1)
    def _():
        y = x_ref[...] + 0.5 * o_ref[...]
        if final_norm:
            y = _rms(y, fg_ref[...])
        o_ref[...] = y


def _ffn(x, g, wg, wu, wd, final_g, *, rows):
    s, d = x.shape
    fp = wg.shape[1]
    tf = min(FF_TILE, fp)
    in_specs = [
        pl.BlockSpec((rows, d), lambda i, f: (i, 0)),
        pl.BlockSpec((1, d), lambda i, f: (0, 0)),
        pl.BlockSpec((d, tf), lambda i, f: (0, f)),
        pl.BlockSpec((d, tf), lambda i, f: (0, f)),
        pl.BlockSpec((tf, d), lambda i, f: (f, 0)),
    ]
    args = [x, g, wg, wu, wd]
    if final_g is not None:
        in_specs.append(pl.BlockSpec((1, d), lambda i, f: (0, 0)))
        args.append(final_g)
    return pl.pallas_call(
        functools.partial(_ffn_kernel, final_norm=final_g is not None),
        grid=(s // rows, fp // tf),
        in_specs=in_specs,
        out_specs=pl.BlockSpec((rows, d), lambda i, f: (i, 0)),
        out_shape=jax.ShapeDtypeStruct((s, d), F32),
        scratch_shapes=[pltpu.VMEM((rows, d), BF16)],
        compiler_params=_params(("parallel", "arbitrary")),
        name="ffn_final" if final_g is not None else "ffn",
    )(*args)


def _proj_kernel(x_ref, pos_ref, mixg_ref, win_ref, qg_ref, wuqt_ref, kvg_ref, wuk_ref, wuvt_ref,
                 qhgt_ref, khg_ref, invft_ref, u_ref, qt_ref, k_ref, vt_ref, *, sub):
    c0 = 2 * CONV_CH
    c1 = c0 + Q_RANK
    c2 = c1 + KV_RANK
    half = ROPE // 2
    inv_d = 1.0 / QK_DIM
    scale = math.log2(math.e) / math.sqrt(QK_DIM)
    qhgt = qhgt_ref[...]
    khg = khg_ref[...]
    lane = lax.broadcasted_iota(jnp.int32, (sub, LANES), 1)
    one_col = jnp.where(lane == ROPE, 1.0, 0.0)
    ones_rows = jnp.where(
        lax.broadcasted_iota(jnp.int32, (V_PAD - V_DIM, sub), 0) == 0, 1.0, 0.0).astype(BF16)
    q_tail = jnp.zeros((HEAD_PAD - QK_DIM, sub), BF16)

    def part(r0):
        rs = slice(r0, r0 + sub)
        hn = _rms(x_ref[rs, :], mixg_ref[...]).astype(BF16)
        z = jnp.dot(hn, win_ref[...], preferred_element_type=F32)
        u_ref[rs, :] = z[:, :CONV_CH] * jax.nn.sigmoid(z[:, CONV_CH:c0])
        qnt = _rms(z[:, c0:c1], qg_ref[...]).T.astype(BF16)
        kvn = _rms(z[:, c1:c2], kvg_ref[...])
        kvnt = kvn.T.astype(BF16)
        kpe = z[:, c2:c2 + LANES]
        qt = jnp.dot(wuqt_ref[...], qnt, preferred_element_type=F32)
        vt = jnp.dot(wuvt_ref[...], kvnt, preferred_element_type=F32)
        kn = jnp.dot(kvn.astype(BF16), wuk_ref[...], preferred_element_type=F32)

        ang = invft_ref[...] * pos_ref[:, rs].astype(F32)
        cos_t = jnp.cos(ang)
        sin_t = jnp.sin(ang)
        zeros = jnp.zeros((LANES - ROPE, sub), F32)
        cos_k = jnp.concatenate([cos_t, cos_t, zeros], axis=0).T
        sin_k = jnp.concatenate([-sin_t, sin_t, zeros], axis=0).T

        kpe_g = kpe * khg[:, NOPE:]
        kpe_rot = kpe_g * cos_k + jnp.where(lane < half, pltpu.roll(kpe_g, LANES - half, 1),
                                            pltpu.roll(kpe_g, half, 1)) * sin_k
        kpe_ss = jnp.sum(kpe * kpe, axis=-1, keepdims=True)
        for h in range(N_HEADS):
            qh = qt[h * HEAD_PAD:h * HEAD_PAD + QK_DIM, :]
            r = lax.rsqrt(jnp.sum(qh * qh, axis=0, keepdims=True) * inv_d + EPS) * scale
            qh = qh * qhgt[:QK_DIM, :] * r
            x1 = qh[NOPE:NOPE + half, :]
            x2 = qh[NOPE + half:, :]
            qt_ref[h, :NOPE, rs] = qh[:NOPE, :].astype(BF16)
            qt_ref[h, NOPE:NOPE + half, rs] = (x1 * cos_t - x2 * sin_t).astype(BF16)
            qt_ref[h, NOPE + half:QK_DIM, rs] = (x2 * cos_t + x1 * sin_t).astype(BF16)
            qt_ref[h, QK_DIM:, rs] = q_tail
            ka = kn[:, h * NOPE:(h + 1) * NOPE]
            rk = lax.rsqrt((jnp.sum(ka * ka, axis=-1, keepdims=True) + kpe_ss) * inv_d + EPS)
            k_ref[h, rs, :NOPE] = (ka * rk * khg[:, :NOPE]).astype(BF16)
            k_ref[h, rs, NOPE:] = (kpe_rot * rk + one_col).astype(BF16)
            vt_ref[h, :V_DIM, rs] = vt[h * V_DIM:(h + 1) * V_DIM, :].astype(BF16)
            vt_ref[h, V_DIM:, rs] = ones_rows

    for r0 in range(0, x_ref.shape[0], sub):
        part(r0)


def _proj(x1, pos, mixg, win, qg, wuqt, kvg, wuk, wuvt, qhg, khg, inv_freq, *, rows):
    s, d = x1.shape
    sub = min(PROJ_SUB, rows)
    qhgt = jnp.broadcast_to(qhg.reshape(HEAD_PAD, 1), (HEAD_PAD, sub))
    invft = jnp.broadcast_to(inv_freq.reshape(-1, 1), (ROPE // 2, sub))
    row = lambda w: pl.BlockSpec((rows, w), lambda i: (i, 0))
    in_specs = [row(d), pl.BlockSpec((1, rows), lambda i: (0, i)), _resident(mixg.shape),
                _resident(win.shape), _resident(qg.shape), _resident(wuqt.shape),
                _resident(kvg.shape), _resident(wuk.shape), _resident(wuvt.shape),
                _resident(qhgt.shape), _resident(khg.shape), _resident(invft.shape)]
    out_shape = [
        jax.ShapeDtypeStruct((s, CONV_CH), F32),
        jax.ShapeDtypeStruct((N_HEADS, HEAD_PAD, s), BF16),
        jax.ShapeDtypeStruct((N_HEADS, s, HEAD_PAD), BF16),
        jax.ShapeDtypeStruct((N_HEADS, V_PAD, s), BF16),
    ]
    out_specs = [
        row(CONV_CH),
        pl.BlockSpec((N_HEADS, HEAD_PAD, rows), lambda i: (0, 0, i)),
        pl.BlockSpec((N_HEADS, rows, HEAD_PAD), lambda i: (0, i, 0)),
        pl.BlockSpec((N_HEADS, V_PAD, rows), lambda i: (0, 0, i)),
    ]
    return pl.pallas_call(
        functools.partial(_proj_kernel, sub=sub),
        grid=(s // rows,),
        in_specs=in_specs,
        out_specs=out_specs,
        out_shape=out_shape,
        compiler_params=_params(("parallel",)),
        name="proj",
    )(x1, pos, mixg, win, qg, wuqt, kvg, wuk, wuvt, qhgt, khg, invft)


def _attn_kernel(qt_ref, k_ref, vt_ref, o_ref, qa_ref, p0_ref, p1_ref, r_ref, pm_ref, a_ref,
                 bad_ref, acc_ref, *, tk):
    nk = k_ref.shape[0] // tk
    tq = qt_ref.shape[1]
    p_refs = (p0_ref, p1_ref)
    first_row = lax.broadcasted_iota(jnp.int32, (SUBLANES_BF16, tq), 0) == 0

    def set_reference(r):
        qa_ref[QK_DIM:QK_DIM + SUBLANES_BF16, :] = jnp.where(first_row, -r, 0.0).astype(BF16)

    def scores(c):
        off = pl.multiple_of(c * tk, tk)
        return jnp.dot(k_ref[pl.ds(off, tk), :], qa_ref[...], preferred_element_type=F32)

    def values(c, slot, alpha):
        off = pl.multiple_of(c * tk, tk)
        pv = jnp.dot(vt_ref[:, pl.ds(off, tk)], p_refs[slot][...], preferred_element_type=F32)
        acc_ref[...] = alpha * acc_ref[...] + pv

    def emit(p, slot):
        pb = p.astype(BF16)
        p_refs[slot][...] = pb
        return jnp.max(pb, axis=0, keepdims=True).astype(F32)

    def step(c, slot):
        r_prev = r_ref[...]
        pm_prev = pm_ref[...]
        r = jnp.maximum(r_prev, r_prev + jnp.log2(pm_prev)).astype(BF16).astype(F32)
        alpha = jnp.exp2(r_prev - r)
        set_reference(r)
        pm_ref[...] = emit(jnp.exp2(scores(c)), slot)
        values(c - 1, 1 - slot, a_ref[...])
        bad_ref[...] = jnp.maximum(bad_ref[...], jnp.where(pm_prev <= 2.0 ** MAX_JUMP, 0.0, 1.0))
        r_ref[...] = r
        a_ref[...] = alpha

    qa_ref[...] = qt_ref[...]
    pm_ref[...] = emit(jnp.exp2(scores(0)), 0)
    r_ref[...] = jnp.zeros_like(r_ref)
    a_ref[...] = jnp.zeros_like(a_ref)
    bad_ref[...] = jnp.zeros_like(bad_ref)
    acc_ref[...] = jnp.zeros_like(acc_ref)

    def body(t, carry):
        c = 2 * t + 1
        step(c, 1)
        step(c + 1, 0)
        return carry

    lax.fori_loop(0, (nk - 2) // 2, body, 0)
    step(nk - 1, 1)
    values(nk - 1, 1, a_ref[...])
    bad = jnp.maximum(bad_ref[...], jnp.where(pm_ref[...] <= 2.0 ** MAX_JUMP, 0.0, 1.0))
    bad = jnp.maximum(bad, jnp.where(acc_ref[V_DIM:V_DIM + 1, :] >= 2.0 ** -MAX_JUMP, 0.0, 1.0))

    @pl.when(jnp.max(bad) > 0.0)
    def _():
        set_reference(jnp.zeros_like(bad))
        r_ref[...] = jnp.full(r_ref.shape, -jnp.inf, F32)
        acc_ref[...] = jnp.zeros_like(acc_ref)

        def exact(c, carry):
            st = scores(c)
            m_prev = r_ref[...]
            m_new = jnp.maximum(m_prev, jnp.max(st, axis=0, keepdims=True))
            p0_ref[...] = jnp.exp2(st - m_new).astype(BF16)
            values(c, 0, jnp.exp2(m_prev - m_new))
            r_ref[...] = m_new
            return carry

        lax.fori_loop(0, nk, exact, 0)

    o = acc_ref[:V_DIM, :] * (1.0 / acc_ref[V_DIM:V_DIM + 1, :])
    o_ref[...] = o.T.astype(o_ref.dtype)


def _attn(qt, k, vt, *, tq, tk):
    h, s, dp = k.shape
    assert (s // tk) % 2 == 0
    stat = pltpu.VMEM((1, tq), F32)
    return pl.pallas_call(
        functools.partial(_attn_kernel, tk=tk),
        grid=(h, s // tq),
        in_specs=[
            pl.BlockSpec((None, dp, tq), lambda hh, i: (hh, 0, i)),
            pl.BlockSpec((None, s, dp), lambda hh, i: (hh, 0, 0)),
            pl.BlockSpec((None, V_PAD, s), lambda hh, i: (hh, 0, 0)),
        ],
        out_specs=pl.BlockSpec((tq, V_DIM), lambda hh, i: (i, hh)),
        out_shape=jax.ShapeDtypeStruct((s, h * V_DIM), BF16),
        scratch_shapes=[pltpu.VMEM((dp, tq), BF16),
                        pltpu.VMEM((tk, tq), BF16), pltpu.VMEM((tk, tq), BF16),
                        stat, stat, stat, stat, pltpu.VMEM((V_PAD, tq), F32)],
        compiler_params=_params(("parallel", "arbitrary")),
        name="attn",
    )(qt, k, vt)


def _conv_kernel(prev_ref, u_ref, next_ref, w_ref, b_ref, g_ref, o_ref, sh_ref, y_ref, *, sub):
    i = pl.program_id(0)
    rows = u_ref.shape[0]
    sh_ref[0, :HALO, :] = jnp.where(i > 0, prev_ref[...], 0.0)
    sh_ref[0, HALO:HALO + rows, :] = u_ref[...]
    sh_ref[0, HALO + rows:, :] = jnp.where(i < pl.num_programs(0) - 1, next_ref[...], 0.0)
    n = rows + 2 * HALO - SUBLANES
    base = HALO - CONV_PAD

    def lane_block(ci, carry):
        cs = pl.ds(pl.multiple_of(ci * LANES, LANES), LANES)
        for b in range(1, SUBLANES):
            sh_ref[b, :n, cs] = sh_ref[0, b:b + n, cs]
        for r0 in range(0, rows, sub):
            acc = jnp.broadcast_to(b_ref[:, cs], (sub, LANES))
            for t in range(CONV_WIDTH):
                a, b = divmod(base + t, SUBLANES)
                lo = r0 + a * SUBLANES
                acc = acc + sh_ref[b, lo:lo + sub, cs] * w_ref[t:t + 1, cs]
            y_ref[r0:r0 + sub, cs] = acc
        return carry

    lax.fori_loop(0, CONV_CH // LANES, lane_block, 0)
    y = _rms(y_ref[...], g_ref[...])
    o_ref[...] = (y * jax.nn.sigmoid(y)).astype(o_ref.dtype)


def _conv(u, w, b, g, *, rows, sub):
    s, c = u.shape
    hb = rows // HALO
    last = s // HALO - 1
    return pl.pallas_call(
        functools.partial(_conv_kernel, sub=sub),
        grid=(s // rows,),
        in_specs=[
            pl.BlockSpec((HALO, c), lambda i: (jnp.maximum(i * hb - 1, 0), 0)),
            pl.BlockSpec((rows, c), lambda i: (i, 0)),
            pl.BlockSpec((HALO, c), lambda i: (jnp.minimum((i + 1) * hb, last), 0)),
            _resident(w.shape), _resident(b.shape), _resident(g.shape),
        ],
        out_specs=pl.BlockSpec((rows, c), lambda i: (i, 0)),
        out_shape=jax.ShapeDtypeStruct((s, c), BF16),
        scratch_shapes=[pltpu.VMEM((SUBLANES, rows + 2 * HALO, c), F32), pltpu.VMEM((rows, c), F32)],
        compiler_params=_params(("parallel",)),
        name="conv",
    )(u, u, u, w, b, g)


def _outproj_kernel(x_ref, a_ref, b_ref, wa_ref, wb_ref, o_ref):
    o_ref[...] = (x_ref[...]
                  + jnp.dot(a_ref[...], wa_ref[...], preferred_element_type=F32)
                  + jnp.dot(b_ref[...], wb_ref[...], preferred_element_type=F32))


def _outproj(x1, a, b, wa, wb, *, rows):
    s, d = x1.shape
    row = lambda w: pl.BlockSpec((rows, w), lambda i: (i, 0))
    return pl.pallas_call(
        _outproj_kernel,
        grid=(s // rows,),
        in_specs=[row(d), row(a.shape[1]), row(b.shape[1]), _resident(wa.shape), _resident(wb.shape)],
        out_specs=row(d),
        out_shape=jax.ShapeDtypeStruct((s, d), F32),
        compiler_params=_params(("parallel",)),
        name="outproj",
    )(x1, a, b, wa, wb)


def _pad_cols(w, n):
    return jnp.pad(w, ((0, 0), (0, n - w.shape[1])))


def _cast_cols_kernel(w_ref, o_ref):
    n = w_ref.shape[1]
    o_ref[:, :n] = w_ref[...].astype(BF16)
    o_ref[:, n:] = jnp.zeros((o_ref.shape[0], o_ref.shape[1] - n), BF16)


def _cast_pad_cols(w, n_out):
    r, n = w.shape
    rows = min(CAST_ROWS, r)
    return pl.pallas_call(
        _cast_cols_kernel,
        grid=(r // rows,),
        in_specs=[pl.BlockSpec((rows, n), lambda i: (i, 0))],
        out_specs=pl.BlockSpec((rows, n_out), lambda i: (i, 0)),
        out_shape=jax.ShapeDtypeStruct((r, n_out), BF16),
        compiler_params=_params(("parallel",)),
        name="cast_cols",
    )(w)


def _cast_rows_kernel(w_ref, o_ref, *, n_blocks):
    @pl.when(pl.program_id(0) < n_blocks)
    def _():
        o_ref[...] = w_ref[...].astype(BF16)

    @pl.when(pl.program_id(0) >= n_blocks)
    def _():
        o_ref[...] = jnp.zeros_like(o_ref)


def _cast_pad_rows(w, n_out):
    n, c = w.shape
    n_blocks = n // LANES
    return pl.pallas_call(
        functools.partial(_cast_rows_kernel, n_blocks=n_blocks),
        grid=(n_out // LANES,),
        in_specs=[pl.BlockSpec((LANES, c), lambda i: (jnp.minimum(i, n_blocks - 1), 0))],
        out_specs=pl.BlockSpec((LANES, c), lambda i: (i, 0)),
        out_shape=jax.ShapeDtypeStruct((n_out, c), BF16),
        compiler_params=_params(("parallel",)),
        name="cast_rows",
    )(w)


def _ffn_weights(wg, wu, wd):
    f = wg.shape[1]
    fp = -(-f // FF_TILE) * FF_TILE
    assert f % LANES == 0
    return _cast_pad_cols(wg, fp), _cast_pad_cols(wu, fp), _cast_pad_rows(wd, fp)


def _head_gain(g):
    return jnp.pad(g, (0, HEAD_PAD - QK_DIM)).reshape(1, HEAD_PAD)


def _layer(x, positions, tiles, ffn1_norm, ffn1_w_gate, ffn1_w_up, ffn1_w_down, mix_norm, w_in,
           q_norm, w_uq, kv_norm, w_ukv, q_head_norm, k_head_norm, dw_kernel, dw_bias, conv_norm,
           w_out, ffn2_norm, ffn2_w_gate, ffn2_w_up, ffn2_w_down, final_norm):
    s, d = x.shape
    row = lambda g: g.reshape(1, -1)

    x1 = _ffn(x, row(ffn1_norm), *_ffn_weights(ffn1_w_gate, ffn1_w_up, ffn1_w_down), None,
              rows=tiles["ffn"])

    c2 = 2 * CONV_CH + Q_RANK + KV_RANK
    win = _pad_cols(w_in.astype(BF16), c2 + LANES)
    wuq = jnp.pad(w_uq.reshape(Q_RANK, N_HEADS, QK_DIM),
                  ((0, 0), (0, 0), (0, HEAD_PAD - QK_DIM))).reshape(Q_RANK, N_HEADS * HEAD_PAD)
    wkv = w_ukv.reshape(KV_RANK, N_HEADS, NOPE + V_DIM)
    wuk = wkv[:, :, :NOPE].reshape(KV_RANK, N_HEADS * NOPE)
    wuv = wkv[:, :, NOPE:].reshape(KV_RANK, N_HEADS * V_DIM)
    inv_freq = ROPE_THETA ** (-(jnp.arange(ROPE // 2, dtype=F32) * 2.0 / ROPE))
    u, qt, k, vt = _proj(x1, positions.reshape(1, s), row(mix_norm), win, row(q_norm),
                         wuq.T.astype(BF16), row(kv_norm), wuk.astype(BF16), wuv.T.astype(BF16),
                         _head_gain(q_head_norm), _head_gain(k_head_norm), inv_freq,
                         rows=tiles["proj"])

    b_out = _attn(qt, k, vt, tq=tiles["attn_q"], tk=tiles["attn_k"])
    a_out = _conv(u, jnp.pad(dw_kernel, ((0, 1), (0, 0))), row(dw_bias), row(conv_norm),
                  rows=tiles["conv"], sub=tiles["conv_sub"])
    wo = w_out.astype(BF16)
    x2 = _outproj(x1, a_out, b_out, wo[:CONV_CH], wo[CONV_CH:], rows=tiles["out"])

    return _ffn(x2, row(ffn2_norm), *_ffn_weights(ffn2_w_gate, ffn2_w_up, ffn2_w_down),
                row(final_norm), rows=tiles["ffn"])


def _tiles(s):
    return {
        "ffn": min(FFN_ROWS, s), "proj": min(PROJ_ROWS, s), "attn_q": min(ATTN_Q, s),
        "attn_k": min(ATTN_K, s), "conv": min(CONV_ROWS, s), "conv_sub": CONV_SUB,
        "out": min(OUT_ROWS, s),
    }


def kernel(x, positions, ffn1_norm, ffn1_w_gate, ffn1_w_up, ffn1_w_down, mix_norm, w_in, q_norm, w_uq, kv_norm, w_ukv, q_head_norm, k_head_norm, dw_kernel, dw_bias, conv_norm, w_out, ffn2_norm, ffn2_w_gate, ffn2_w_up, ffn2_w_down, final_norm):
    batch, s, d = x.shape
    assert ffn1_norm.shape[0] == 1, "single-layer stack expected"
    outs = []
    for bi in range(batch):
        outs.append(_layer(
            x[bi], positions[bi], _tiles(s), ffn1_norm[0], ffn1_w_gate[0], ffn1_w_up[0],
            ffn1_w_down[0], mix_norm[0], w_in[0], q_norm[0], w_uq[0], kv_norm[0], w_ukv[0],
            q_head_norm[0], k_head_norm[0], dw_kernel[0], dw_bias[0], conv_norm[0], w_out[0],
            ffn2_norm[0], ffn2_w_gate[0], ffn2_w_up[0], ffn2_w_down[0], final_norm[0]))
    return jnp.stack(outs)
```

```python
import functools
import math

import jax
import jax.numpy as jnp
from jax import lax
from jax.experimental import pallas as pl
from jax.experimental.pallas import tpu as pltpu

F32 = jnp.float32
BF16 = jnp.bfloat16

EPS = 1e-6
ROPE_THETA = 10000.0
CONV_CH = 1024
CONV_WIDTH = 31
CONV_PAD = CONV_WIDTH // 2
N_HEADS = 8
NOPE = 128
ROPE = 64
QK_DIM = NOPE + ROPE
V_DIM = 128
Q_RANK = 512
KV_RANK = 256

LANES = 128
SUBLANES = 8
SUBLANES_BF16 = 16
HEAD_PAD = 2 * LANES
V_PAD = V_DIM + SUBLANES_BF16
VMEM_LIMIT = 56 * 1024 * 1024

FF_TILE = 512
FFN_ROWS = 512
PROJ_ROWS = 512
PROJ_SUB = 256
ATTN_Q = 1024
ATTN_K = 1024
CONV_ROWS = 256
CONV_SUB = 256
OUT_ROWS = 512
CAST_ROWS = 256
CAST_COLS = 256
HALO = 16
MAX_JUMP = 64.0


def _rms(x, g):
    ms = jnp.mean(x * x, axis=-1, keepdims=True)
    return x * lax.rsqrt(ms + EPS) * g


def _params(sem):
    return pltpu.CompilerParams(dimension_semantics=sem, vmem_limit_bytes=VMEM_LIMIT)


def _resident(shape):
    nd = len(shape)
    return pl.BlockSpec(shape, lambda *_: (0,) * nd, pipeline_mode=pl.Buffered(1))


def _ffn_kernel(*refs, final_norm):
    if final_norm:
        x_ref, g_ref, wg_ref, wu_ref, wd_ref, fg_ref, o_ref, xn_ref = refs
    else:
        x_ref, g_ref, wg_ref, wu_ref, wd_ref, o_ref, xn_ref = refs
    f = pl.program_id(1)

    @pl.when(f == 0)
    def _():
        xn_ref[...] = _rms(x_ref[...], g_ref[...]).astype(BF16)
        o_ref[...] = jnp.zeros_like(o_ref)

    xn = xn_ref[...]
    gate = jnp.dot(xn, wg_ref[...], preferred_element_type=F32)
    up = jnp.dot(xn, wu_ref[...], preferred_element_type=F32)
    h = (gate * jax.nn.sigmoid(gate) * up).astype(BF16)
    o_ref[...] += jnp.dot(h, wd_ref[...], preferred_element_type=F32)

    @pl.when(f == pl.num_programs(1) - 1)
    def _():
        y = x_ref[...] + 0.5 * o_ref[...]
        if final_norm:
            y = _rms(y, fg_ref[...])
        o_ref[...] = y


def _ffn(x, g, wg, wu, wd, final_g, *, rows):
    s, d = x.shape
    fp = wg.shape[1]
    tf = min(FF_TILE, fp)
    in_specs = [
        pl.BlockSpec((rows, d), lambda i, f: (i, 0)),
        pl.BlockSpec((1, d), lambda i, f: (0, 0)),
        pl.BlockSpec((d, tf), lambda i, f: (0, f)),
        pl.BlockSpec((d, tf), lambda i, f: (0, f)),
        pl.BlockSpec((tf, d), lambda i, f: (f, 0)),
    ]
    args = [x, g, wg, wu, wd]
    if final_g is not None:
        in_specs.append(pl.BlockSpec((1, d), lambda i, f: (0, 0)))
        args.append(final_g)
    return pl.pallas_call(
        functools.partial(_ffn_kernel, final_norm=final_g is not None),
        grid=(s // rows, fp // tf),
        in_specs=in_specs,
        out_specs=pl.BlockSpec((rows, d), lambda i, f: (i, 0)),
        out_shape=jax.ShapeDtypeStruct((s, d), F32),
        scratch_shapes=[pltpu.VMEM((rows, d), BF16)],
        compiler_params=_params(("parallel", "arbitrary")),
        name="ffn_final" if final_g is not None else "ffn",
    )(*args)


def _proj_kernel(x_ref, pos_ref, mixg_ref, win_ref, qg_ref, wuqt_ref, kvg_ref, wuk_ref, wuvt_ref,
                 qhgt_ref, khg_ref, invft_ref, u_ref, qt_ref, k_ref, vt_ref, *, sub):
    c0 = 2 * CONV_CH
    c1 = c0 + Q_RANK
    c2 = c1 + KV_RANK
    half = ROPE // 2
    inv_d = 1.0 / QK_DIM
    scale = math.log2(math.e) / math.sqrt(QK_DIM)
    qhgt = qhgt_ref[...]
    khg = khg_ref[...]
    lane = lax.broadcasted_iota(jnp.int32, (sub, LANES), 1)
    one_col = jnp.where(lane == ROPE, 1.0, 0.0)
    ones_rows = jnp.where(
        lax.broadcasted_iota(jnp.int32, (V_PAD - V_DIM, sub), 0) == 0, 1.0, 0.0).astype(BF16)
    q_tail = jnp.zeros((HEAD_PAD - QK_DIM, sub), BF16)

    def part(r0):
        rs = slice(r0, r0 + sub)
        hn = _rms(x_ref[rs, :], mixg_ref[...]).astype(BF16)
        z = jnp.dot(hn, win_ref[...], preferred_element_type=F32)
        u_ref[rs, :] = z[:, :CONV_CH] * jax.nn.sigmoid(z[:, CONV_CH:c0])
        qnt = _rms(z[:, c0:c1], qg_ref[...]).T.astype(BF16)
        kvn = _rms(z[:, c1:c2], kvg_ref[...])
        kvnt = kvn.T.astype(BF16)
        kpe = z[:, c2:c2 + LANES]
        qt = jnp.dot(wuqt_ref[...], qnt, preferred_element_type=F32)
        vt = jnp.dot(wuvt_ref[...], kvnt, preferred_element_type=F32)
        kn = jnp.dot(kvn.astype(BF16), wuk_ref[...], preferred_element_type=F32)

        ang = invft_ref[...] * pos_ref[:, rs].astype(F32)
        cos_t = jnp.cos(ang)
        sin_t = jnp.sin(ang)
        zeros = jnp.zeros((LANES - ROPE, sub), F32)
        cos_k = jnp.concatenate([cos_t, cos_t, zeros], axis=0).T
        sin_k = jnp.concatenate([-sin_t, sin_t, zeros], axis=0).T

        kpe_g = kpe * khg[:, NOPE:]
        kpe_rot = kpe_g * cos_k + jnp.where(lane < half, pltpu.roll(kpe_g, LANES - half, 1),
                                            pltpu.roll(kpe_g, half, 1)) * sin_k
        kpe_ss = jnp.sum(kpe * kpe, axis=-1, keepdims=True)
        for h in range(N_HEADS):
            qh = qt[h * HEAD_PAD:h * HEAD_PAD + QK_DIM, :]
            r = lax.rsqrt(jnp.sum(qh * qh, axis=0, keepdims=True) * inv_d + EPS) * scale
            qh = qh * qhgt[:QK_DIM, :] * r
            x1 = qh[NOPE:NOPE + half, :]
            x2 = qh[NOPE + half:, :]
            qt_ref[h, :NOPE, rs] = qh[:NOPE, :].astype(BF16)
            qt_ref[h, NOPE:NOPE + half, rs] = (x1 * cos_t - x2 * sin_t).astype(BF16)
            qt_ref[h, NOPE + half:QK_DIM, rs] = (x2 * cos_t + x1 * sin_t).astype(BF16)
            qt_ref[h, QK_DIM:, rs] = q_tail
            ka = kn[:, h * NOPE:(h + 1) * NOPE]
            rk = lax.rsqrt((jnp.sum(ka * ka, axis=-1, keepdims=True) + kpe_ss) * inv_d + EPS)
            k_ref[h, rs, :NOPE] = (ka * rk * khg[:, :NOPE]).astype(BF16)
            k_ref[h, rs, NOPE:] = (kpe_rot * rk + one_col).astype(BF16)
            vt_ref[h, :V_DIM, rs] = vt[h * V_DIM:(h + 1) * V_DIM, :].astype(BF16)
            vt_ref[h, V_DIM:, rs] = ones_rows

    for r0 in range(0, x_ref.shape[0], sub):
        part(r0)


def _proj(x1, pos, mixg, win, qg, wuqt, kvg, wuk, wuvt, qhg, khg, inv_freq, *, rows):
    s, d = x1.shape
    sub = min(PROJ_SUB, rows)
    qhgt = jnp.broadcast_to(qhg.reshape(HEAD_PAD, 1), (HEAD_PAD, sub))
    invft = jnp.broadcast_to(inv_freq.reshape(-1, 1), (ROPE // 2, sub))
    row = lambda w: pl.BlockSpec((rows, w), lambda i: (i, 0))
    in_specs = [row(d), pl.BlockSpec((1, rows), lambda i: (0, i)), _resident(mixg.shape),
                _resident(win.shape), _resident(qg.shape), _resident(wuqt.shape),
                _resident(kvg.shape), _resident(wuk.shape), _resident(wuvt.shape),
                _resident(qhgt.shape), _resident(khg.shape), _resident(invft.shape)]
    out_shape = [
        jax.ShapeDtypeStruct((s, CONV_CH), F32),
        jax.ShapeDtypeStruct((N_HEADS, HEAD_PAD, s), BF16),
        jax.ShapeDtypeStruct((N_HEADS, s, HEAD_PAD), BF16),
        jax.ShapeDtypeStruct((N_HEADS, V_PAD, s), BF16),
    ]
    out_specs = [
        row(CONV_CH),
        pl.BlockSpec((N_HEADS, HEAD_PAD, rows), lambda i: (0, 0, i)),
        pl.BlockSpec((N_HEADS, rows, HEAD_PAD), lambda i: (0, i, 0)),
        pl.BlockSpec((N_HEADS, V_PAD, rows), lambda i: (0, 0, i)),
    ]
    return pl.pallas_call(
        functools.partial(_proj_kernel, sub=sub),
        grid=(s // rows,),
        in_specs=in_specs,
        out_specs=out_specs,
        out_shape=out_shape,
        compiler_params=_params(("parallel",)),
        name="proj",
    )(x1, pos, mixg, win, qg, wuqt, kvg, wuk, wuvt, qhgt, khg, invft)


def _attn_kernel(qt_ref, k_ref, vt_ref, o_ref, qa_ref, p0_ref, p1_ref, r_ref, pm_ref, a_ref,
                 bad_ref, acc_ref, *, tk):
    nk = k_ref.shape[0] // tk
    tq = qt_ref.shape[1]
    p_refs = (p0_ref, p1_ref)
    first_row = lax.broadcasted_iota(jnp.int32, (SUBLANES_BF16, tq), 0) == 0

    def set_reference(r):
        qa_ref[QK_DIM:QK_DIM + SUBLANES_BF16, :] = jnp.where(first_row, -r, 0.0).astype(BF16)

    def scores(c):
        off = pl.multiple_of(c * tk, tk)
        return jnp.dot(k_ref[pl.ds(off, tk), :], qa_ref[...], preferred_element_type=F32)

    def values(c, slot, alpha):
        off = pl.multiple_of(c * tk, tk)
        pv = jnp.dot(vt_ref[:, pl.ds(off, tk)], p_refs[slot][...], preferred_element_type=F32)
        acc_ref[...] = alpha * acc_ref[...] + pv

    def emit(p, slot):
        pb = p.astype(BF16)
        p_refs[slot][...] = pb
        return jnp.max(pb, axis=0, keepdims=True).astype(F32)

    def step(c, slot):
        r_prev = r_ref[...]
        pm_prev = pm_ref[...]
        r = jnp.maximum(r_prev, r_prev + jnp.log2(pm_prev)).astype(BF16).astype(F32)
        alpha = jnp.exp2(r_prev - r)
        set_reference(r)
        pm_ref[...] = emit(jnp.exp2(scores(c)), slot)
        values(c - 1, 1 - slot, a_ref[...])
        bad_ref[...] = jnp.maximum(bad_ref[...], jnp.where(pm_prev <= 2.0 ** MAX_JUMP, 0.0, 1.0))
        r_ref[...] = r
        a_ref[...] = alpha

    qa_ref[...] = qt_ref[...]
    pm_ref[...] = emit(jnp.exp2(scores(0)), 0)
    r_ref[...] = jnp.zeros_like(r_ref)
    a_ref[...] = jnp.zeros_like(a_ref)
    bad_ref[...] = jnp.zeros_like(bad_ref)
    acc_ref[...] = jnp.zeros_like(acc_ref)

    def body(t, carry):
        c = 2 * t + 1
        step(c, 1)
        step(c + 1, 0)
        return carry

    lax.fori_loop(0, (nk - 2) // 2, body, 0)
    step(nk - 1, 1)
    values(nk - 1, 1, a_ref[...])
    bad = jnp.maximum(bad_ref[...], jnp.where(pm_ref[...] <= 2.0 ** MAX_JUMP, 0.0, 1.0))
    bad = jnp.maximum(bad, jnp.where(acc_ref[V_DIM:V_DIM + 1, :] >= 2.0 ** -MAX_JUMP, 0.0, 1.0))

    @pl.when(jnp.max(bad) > 0.0)
    def _():
        set_reference(jnp.zeros_like(bad))
        r_ref[...] = jnp.full(r_ref.shape, -jnp.inf, F32)
        acc_ref[...] = jnp.zeros_like(acc_ref)

        def exact(c, carry):
            st = scores(c)
            m_prev = r_ref[...]
            m_new = jnp.maximum(m_prev, jnp.max(st, axis=0, keepdims=True))
            p0_ref[...] = jnp.exp2(st - m_new).astype(BF16)
            values(c, 0, jnp.exp2(m_prev - m_new))
            r_ref[...] = m_new
            return carry

        lax.fori_loop(0, nk, exact, 0)

    o = acc_ref[:V_DIM, :] * (1.0 / acc_ref[V_DIM:V_DIM + 1, :])
    o_ref[...] = o.T.astype(o_ref.dtype)


def _attn(qt, k, vt, *, tq, tk):
    h, s, dp = k.shape
    assert (s // tk) % 2 == 0
    stat = pltpu.VMEM((1, tq), F32)
    return pl.pallas_call(
        functools.partial(_attn_kernel, tk=tk),
        grid=(h, s // tq),
        in_specs=[
            pl.BlockSpec((None, dp, tq), lambda hh, i: (hh, 0, i)),
            pl.BlockSpec((None, s, dp), lambda hh, i: (hh, 0, 0)),
            pl.BlockSpec((None, V_PAD, s), lambda hh, i: (hh, 0, 0)),
        ],
        out_specs=pl.BlockSpec((tq, V_DIM), lambda hh, i: (i, hh)),
        out_shape=jax.ShapeDtypeStruct((s, h * V_DIM), BF16),
        scratch_shapes=[pltpu.VMEM((dp, tq), BF16),
                        pltpu.VMEM((tk, tq), BF16), pltpu.VMEM((tk, tq), BF16),
                        stat, stat, stat, stat, pltpu.VMEM((V_PAD, tq), F32)],
        compiler_params=_params(("parallel", "arbitrary")),
        name="attn",
    )(qt, k, vt)


def _conv_kernel(prev_ref, u_ref, next_ref, w_ref, b_ref, g_ref, o_ref, sh_ref, y_ref, sq_ref, *,
                 sub):
    i = pl.program_id(0)
    rows = u_ref.shape[0]
    blocks = CONV_CH // LANES
    has_prev = i > 0
    has_next = i < pl.num_programs(0) - 1
    for c in range(blocks):
        cs = slice(c * LANES, (c + 1) * LANES)
        sh_ref[c, 0, :HALO, :] = jnp.where(has_prev, prev_ref[:, cs], 0.0)
        sh_ref[c, 0, HALO:HALO + rows, :] = u_ref[:, cs]
        sh_ref[c, 0, HALO + rows:, :] = jnp.where(has_next, next_ref[:, cs], 0.0)
    n = rows + 2 * HALO - SUBLANES
    base = HALO - CONV_PAD

    sq_ref[...] = jnp.zeros_like(sq_ref)

    def lane_block(ci, carry):
        cs = pl.ds(pl.multiple_of(ci * LANES, LANES), LANES)
        for b in range(1, SUBLANES):
            sh_ref[ci, b, :n, :] = sh_ref[ci, 0, b:b + n, :]
        for r0 in range(0, rows, sub):
            acc = jnp.broadcast_to(b_ref[:, cs], (sub, LANES))
            for t in range(CONV_WIDTH):
                a, b = divmod(base + t, SUBLANES)
                lo = r0 + a * SUBLANES
                acc = acc + sh_ref[ci, b, lo:lo + sub, :] * w_ref[t:t + 1, cs]
            y_ref[ci, r0:r0 + sub, :] = acc
            sq_ref[r0:r0 + sub, :] += acc * acc
        return carry

    lax.fori_loop(0, blocks, lane_block, 0)
    ssq = jnp.sum(sq_ref[...], axis=-1, keepdims=True)
    inv = lax.rsqrt(ssq * (1.0 / CONV_CH) + EPS)
    for c in range(blocks):
        cs = slice(c * LANES, (c + 1) * LANES)
        y = y_ref[c] * inv * g_ref[:, cs]
        o_ref[:, cs] = (y * jax.nn.sigmoid(y)).astype(o_ref.dtype)


def _conv(u, w, b, g, *, rows, sub):
    s, c = u.shape
    hb = rows // HALO
    last = s // HALO - 1
    return pl.pallas_call(
        functools.partial(_conv_kernel, sub=sub),
        grid=(s // rows,),
        in_specs=[
            pl.BlockSpec((HALO, c), lambda i: (jnp.maximum(i * hb - 1, 0), 0)),
            pl.BlockSpec((rows, c), lambda i: (i, 0)),
            pl.BlockSpec((HALO, c), lambda i: (jnp.minimum((i + 1) * hb, last), 0)),
            _resident(w.shape), _resident(b.shape), _resident(g.shape),
        ],
        out_specs=pl.BlockSpec((rows, c), lambda i: (i, 0)),
        out_shape=jax.ShapeDtypeStruct((s, c), BF16),
        scratch_shapes=[pltpu.VMEM((c // LANES, SUBLANES, rows + 2 * HALO, LANES), F32),
                        pltpu.VMEM((c // LANES, rows, LANES), F32), pltpu.VMEM((rows, LANES), F32)],
        compiler_params=_params(("parallel",)),
        name="conv",
    )(u, u, u, w, b, g)


def _outproj_kernel(x_ref, a_ref, b_ref, wa_ref, wb_ref, o_ref):
    o_ref[...] = (x_ref[...]
                  + jnp.dot(a_ref[...], wa_ref[...], preferred_element_type=F32)
                  + jnp.dot(b_ref[...], wb_ref[...], preferred_element_type=F32))


def _outproj(x1, a, b, wa, wb, *, rows):
    s, d = x1.shape
    row = lambda w: pl.BlockSpec((rows, w), lambda i: (i, 0))
    return pl.pallas_call(
        _outproj_kernel,
        grid=(s // rows,),
        in_specs=[row(d), row(a.shape[1]), row(b.shape[1]), _resident(wa.shape), _resident(wb.shape)],
        out_specs=row(d),
        out_shape=jax.ShapeDtypeStruct((s, d), F32),
        compiler_params=_params(("parallel",)),
        name="outproj",
    )(x1, a, b, wa, wb)


def _pad_cols(w, n):
    return jnp.pad(w, ((0, 0), (0, n - w.shape[1])))


def _cast_cols_kernel(w_ref, o_ref):
    n = w_ref.shape[1]
    o_ref[:, :n] = w_ref[...].astype(BF16)
    o_ref[:, n:] = jnp.zeros((o_ref.shape[0], o_ref.shape[1] - n), BF16)


def _cast_pad_cols(w, n_out):
    r, n = w.shape
    rows = min(CAST_ROWS, r)
    return pl.pallas_call(
        _cast_cols_kernel,
        grid=(r // rows,),
        in_specs=[pl.BlockSpec((rows, n), lambda i: (i, 0))],
        out_specs=pl.BlockSpec((rows, n_out), lambda i: (i, 0)),
        out_shape=jax.ShapeDtypeStruct((r, n_out), BF16),
        compiler_params=_params(("parallel",)),
        name="cast_cols",
    )(w)


def _cast_rows_kernel(w_ref, o_ref):
    n = w_ref.shape[0]
    o_ref[:n, :] = w_ref[...].astype(BF16)
    o_ref[n:, :] = jnp.zeros((o_ref.shape[0] - n, o_ref.shape[1]), BF16)


def _cast_pad_rows(w, n_out):
    n, c = w.shape
    cols = min(CAST_COLS, c)
    return pl.pallas_call(
        _cast_rows_kernel,
        grid=(c // cols,),
        in_specs=[pl.BlockSpec((n, cols), lambda i: (0, i))],
        out_specs=pl.BlockSpec((n_out, cols), lambda i: (0, i)),
        out_shape=jax.ShapeDtypeStruct((n_out, c), BF16),
        compiler_params=_params(("parallel",)),
        name="cast_rows",
    )(w)


def _ffn_weights(wg, wu, wd):
    f = wg.shape[1]
    fp = -(-f // FF_TILE) * FF_TILE
    assert f % LANES == 0
    return _cast_pad_cols(wg, fp), _cast_pad_cols(wu, fp), _cast_pad_rows(wd, fp)


def _head_gain(g):
    return jnp.pad(g, (0, HEAD_PAD - QK_DIM)).reshape(1, HEAD_PAD)


def _layer(x, positions, tiles, ffn1_norm, ffn1_w_gate, ffn1_w_up, ffn1_w_down, mix_norm, w_in,
           q_norm, w_uq, kv_norm, w_ukv, q_head_norm, k_head_norm, dw_kernel, dw_bias, conv_norm,
           w_out, ffn2_norm, ffn2_w_gate, ffn2_w_up, ffn2_w_down, final_norm):
    s, d = x.shape
    row = lambda g: g.reshape(1, -1)

    x1 = _ffn(x, row(ffn1_norm), *_ffn_weights(ffn1_w_gate, ffn1_w_up, ffn1_w_down), None,
              rows=tiles["ffn"])

    c2 = 2 * CONV_CH + Q_RANK + KV_RANK
    win = _pad_cols(w_in.astype(BF16), c2 + LANES)
    wuq = jnp.pad(w_uq.reshape(Q_RANK, N_HEADS, QK_DIM),
                  ((0, 0), (0, 0), (0, HEAD_PAD - QK_DIM))).reshape(Q_RANK, N_HEADS * HEAD_PAD)
    wkv = w_ukv.reshape(KV_RANK, N_HEADS, NOPE + V_DIM)
    wuk = wkv[:, :, :NOPE].reshape(KV_RANK, N_HEADS * NOPE)
    wuv = wkv[:, :, NOPE:].reshape(KV_RANK, N_HEADS * V_DIM)
    inv_freq = ROPE_THETA ** (-(jnp.arange(ROPE // 2, dtype=F32) * 2.0 / ROPE))
    u, qt, k, vt = _proj(x1, positions.reshape(1, s), row(mix_norm), win, row(q_norm),
                         wuq.T.astype(BF16), row(kv_norm), wuk.astype(BF16), wuv.T.astype(BF16),
                         _head_gain(q_head_norm), _head_gain(k_head_norm), inv_freq,
                         rows=tiles["proj"])

    b_out = _attn(qt, k, vt, tq=tiles["attn_q"], tk=tiles["attn_k"])
    a_out = _conv(u, jnp.pad(dw_kernel, ((0, 1), (0, 0))), row(dw_bias), row(conv_norm),
                  rows=tiles["conv"], sub=tiles["conv_sub"])
    wo = w_out.astype(BF16)
    x2 = _outproj(x1, a_out, b_out, wo[:CONV_CH], wo[CONV_CH:], rows=tiles["out"])

    return _ffn(x2, row(ffn2_norm), *_ffn_weights(ffn2_w_gate, ffn2_w_up, ffn2_w_down),
                row(final_norm), rows=tiles["ffn"])


def _tiles(s):
    return {
        "ffn": min(FFN_ROWS, s), "proj": min(PROJ_ROWS, s), "attn_q": min(ATTN_Q, s),
        "attn_k": min(ATTN_K, s), "conv": min(CONV_ROWS, s), "conv_sub": CONV_SUB,
        "out": min(OUT_ROWS, s),
    }


def kernel(x, positions, ffn1_norm, ffn1_w_gate, ffn1_w_up, ffn1_w_down, mix_norm, w_in, q_norm, w_uq, kv_norm, w_ukv, q_head_norm, k_head_norm, dw_kernel, dw_bias, conv_norm, w_out, ffn2_norm, ffn2_w_gate, ffn2_w_up, ffn2_w_down, final_norm):
    batch, s, d = x.shape
    assert ffn1_norm.shape[0] == 1, "single-layer stack expected"
    outs = []
    for bi in range(batch):
        outs.append(_layer(
            x[bi], positions[bi], _tiles(s), ffn1_norm[0], ffn1_w_gate[0], ffn1_w_up[0],
            ffn1_w_down[0], mix_norm[0], w_in[0], q_norm[0], w_uq[0], kv_norm[0], w_ukv[0],
            q_head_norm[0], k_head_norm[0], dw_kernel[0], dw_bias[0], conv_norm[0], w_out[0],
            ffn2_norm[0], ffn2_w_gate[0], ffn2_w_up[0], ffn2_w_down[0], final_norm[0]))
    return jnp.stack(outs)
```

```python
import functools
import math

import jax
import jax.numpy as jnp
from jax import lax
from jax.experimental import pallas as pl
from jax.experimental.pallas import tpu as pltpu

F32 = jnp.float32
BF16 = jnp.bfloat16

EPS = 1e-6
ROPE_THETA = 10000.0
CONV_CH = 1024
CONV_WIDTH = 31
CONV_PAD = CONV_WIDTH // 2
N_HEADS = 8
NOPE = 128
ROPE = 64
QK_DIM = NOPE + ROPE
V_DIM = 128
Q_RANK = 512
KV_RANK = 256

LANES = 128
SUBLANES = 8
SUBLANES_BF16 = 16
HEAD_PAD = 2 * LANES
V_PAD = V_DIM + SUBLANES_BF16
VMEM_LIMIT = 56 * 1024 * 1024

FF_TILE = 512
FFN_ROWS = 512
FFN_LANES = 256
PROJ_ROWS = 512
PROJ_SUB = 256
ATTN_Q = 1024
ATTN_K = 1024
ATTN_LANES = 256
CONV_ROWS = 256
CONV_SUB = 256
OUT_ROWS = 512
CAST_ROWS = 256
CAST_COLS = 256
HALO = 16
MAX_JUMP = 64.0


def _rms(x, g):
    ms = jnp.mean(x * x, axis=-1, keepdims=True)
    return x * lax.rsqrt(ms + EPS) * g


def _params(sem):
    return pltpu.CompilerParams(dimension_semantics=sem, vmem_limit_bytes=VMEM_LIMIT)


def _resident(shape):
    nd = len(shape)
    return pl.BlockSpec(shape, lambda *_: (0,) * nd, pipeline_mode=pl.Buffered(1))


def _ffn_kernel(*refs, final_norm):
    if final_norm:
        x_ref, g_ref, wg_ref, wu_ref, wd_ref, fg_ref, o_ref, xn_ref, acc_ref = refs
    else:
        x_ref, g_ref, wg_ref, wu_ref, wd_ref, o_ref, xn_ref, acc_ref = refs
    f = pl.program_id(1)
    nb, _, bw = acc_ref.shape

    @pl.when(f == 0)
    def _():
        xn_ref[...] = _rms(x_ref[...], g_ref[...]).astype(BF16)
        acc_ref[...] = jnp.zeros_like(acc_ref)

    xn = xn_ref[...]
    gate = jnp.dot(xn, wg_ref[...], preferred_element_type=F32)
    up = jnp.dot(xn, wu_ref[...], preferred_element_type=F32)
    h = (gate * jax.nn.sigmoid(gate) * up).astype(BF16)
    for b in range(nb):
        acc_ref[b] += jnp.dot(h, wd_ref[:, b * bw:(b + 1) * bw], preferred_element_type=F32)

    @pl.when(f == pl.num_programs(1) - 1)
    def _():
        for b in range(nb):
            o_ref[:, b * bw:(b + 1) * bw] = x_ref[:, b * bw:(b + 1) * bw] + 0.5 * acc_ref[b]
        if final_norm:
            o_ref[...] = _rms(o_ref[...], fg_ref[...])


def _ffn(x, g, wg, wu, wd, final_g, *, rows):
    s, d = x.shape
    fp = wg.shape[1]
    tf = min(FF_TILE, fp)
    in_specs = [
        pl.BlockSpec((rows, d), lambda i, f: (i, 0)),
        pl.BlockSpec((1, d), lambda i, f: (0, 0)),
        pl.BlockSpec((d, tf), lambda i, f: (0, f)),
        pl.BlockSpec((d, tf), lambda i, f: (0, f)),
        pl.BlockSpec((tf, d), lambda i, f: (f, 0)),
    ]
    args = [x, g, wg, wu, wd]
    if final_g is not None:
        in_specs.append(pl.BlockSpec((1, d), lambda i, f: (0, 0)))
        args.append(final_g)
    return pl.pallas_call(
        functools.partial(_ffn_kernel, final_norm=final_g is not None),
        grid=(s // rows, fp // tf),
        in_specs=in_specs,
        out_specs=pl.BlockSpec((rows, d), lambda i, f: (i, 0)),
        out_shape=jax.ShapeDtypeStruct((s, d), F32),
        scratch_shapes=[pltpu.VMEM((rows, d), BF16),
                        pltpu.VMEM((d // FFN_LANES, rows, FFN_LANES), F32)],
        compiler_params=_params(("parallel", "arbitrary")),
        name="ffn_final" if final_g is not None else "ffn",
    )(*args)


def _proj_kernel(x_ref, pos_ref, mixg_ref, win_ref, qg_ref, wuqt_ref, kvg_ref, wuk_ref, wuvt_ref,
                 qhgt_ref, khg_ref, invft_ref, u_ref, qt_ref, k_ref, vt_ref, *, sub):
    c0 = 2 * CONV_CH
    c1 = c0 + Q_RANK
    c2 = c1 + KV_RANK
    half = ROPE // 2
    inv_d = 1.0 / QK_DIM
    scale = math.log2(math.e) / math.sqrt(QK_DIM)
    qhgt = qhgt_ref[...]
    khg = khg_ref[...]
    lane = lax.broadcasted_iota(jnp.int32, (sub, LANES), 1)
    one_col = jnp.where(lane == ROPE, 1.0, 0.0)
    ones_rows = jnp.where(
        lax.broadcasted_iota(jnp.int32, (V_PAD - V_DIM, sub), 0) == 0, 1.0, 0.0).astype(BF16)
    q_tail = jnp.zeros((HEAD_PAD - QK_DIM, sub), BF16)

    def part(r0):
        rs = slice(r0, r0 + sub)
        hn = _rms(x_ref[rs, :], mixg_ref[...]).astype(BF16)
        z = jnp.dot(hn, win_ref[...], preferred_element_type=F32)
        u_ref[rs, :] = z[:, :CONV_CH] * jax.nn.sigmoid(z[:, CONV_CH:c0])
        qnt = _rms(z[:, c0:c1], qg_ref[...]).T.astype(BF16)
        kvn = _rms(z[:, c1:c2], kvg_ref[...])
        kvnt = kvn.T.astype(BF16)
        kpe = z[:, c2:c2 + LANES]
        qt = jnp.dot(wuqt_ref[...], qnt, preferred_element_type=F32)
        vt = jnp.dot(wuvt_ref[...], kvnt, preferred_element_type=F32)
        kn = jnp.dot(kvn.astype(BF16), wuk_ref[...], preferred_element_type=F32)

        ang = invft_ref[...] * pos_ref[:, rs].astype(F32)
        cos_t = jnp.cos(ang)
        sin_t = jnp.sin(ang)
        zeros = jnp.zeros((LANES - ROPE, sub), F32)
        cos_k = jnp.concatenate([cos_t, cos_t, zeros], axis=0).T
        sin_k = jnp.concatenate([-sin_t, sin_t, zeros], axis=0).T

        kpe_g = kpe * khg[:, NOPE:]
        kpe_rot = kpe_g * cos_k + jnp.where(lane < half, pltpu.roll(kpe_g, LANES - half, 1),
                                            pltpu.roll(kpe_g, half, 1)) * sin_k
        kpe_ss = jnp.sum(kpe * kpe, axis=-1, keepdims=True)
        for h in range(N_HEADS):
            qh = qt[h * HEAD_PAD:h * HEAD_PAD + QK_DIM, :]
            r = lax.rsqrt(jnp.sum(qh * qh, axis=0, keepdims=True) * inv_d + EPS) * scale
            qh = qh * qhgt[:QK_DIM, :] * r
            x1 = qh[NOPE:NOPE + half, :]
            x2 = qh[NOPE + half:, :]
            qt_ref[h, :NOPE, rs] = qh[:NOPE, :].astype(BF16)
            qt_ref[h, NOPE:NOPE + half, rs] = (x1 * cos_t - x2 * sin_t).astype(BF16)
            qt_ref[h, NOPE + half:QK_DIM, rs] = (x2 * cos_t + x1 * sin_t).astype(BF16)
            qt_ref[h, QK_DIM:, rs] = q_tail
            ka = kn[:, h * NOPE:(h + 1) * NOPE]
            rk = lax.rsqrt((jnp.sum(ka * ka, axis=-1, keepdims=True) + kpe_ss) * inv_d + EPS)
            k_ref[h, rs, :NOPE] = (ka * rk * khg[:, :NOPE]).astype(BF16)
            k_ref[h, rs, NOPE:] = (kpe_rot * rk + one_col).astype(BF16)
            vt_ref[h, :V_DIM, rs] = vt[h * V_DIM:(h + 1) * V_DIM, :].astype(BF16)
            vt_ref[h, V_DIM:, rs] = ones_rows

    for r0 in range(0, x_ref.shape[0], sub):
        part(r0)


def _proj(x1, pos, mixg, win, qg, wuqt, kvg, wuk, wuvt, qhg, khg, inv_freq, *, rows):
    s, d = x1.shape
    sub = min(PROJ_SUB, rows)
    qhgt = jnp.broadcast_to(qhg.reshape(HEAD_PAD, 1), (HEAD_PAD, sub))
    invft = jnp.broadcast_to(inv_freq.reshape(-1, 1), (ROPE // 2, sub))
    row = lambda w: pl.BlockSpec((rows, w), lambda i: (i, 0))
    in_specs = [row(d), pl.BlockSpec((1, rows), lambda i: (0, i)), _resident(mixg.shape),
                _resident(win.shape), _resident(qg.shape), _resident(wuqt.shape),
                _resident(kvg.shape), _resident(wuk.shape), _resident(wuvt.shape),
                _resident(qhgt.shape), _resident(khg.shape), _resident(invft.shape)]
    out_shape = [
        jax.ShapeDtypeStruct((s, CONV_CH), F32),
        jax.ShapeDtypeStruct((N_HEADS, HEAD_PAD, s), BF16),
        jax.ShapeDtypeStruct((N_HEADS, s, HEAD_PAD), BF16),
        jax.ShapeDtypeStruct((N_HEADS, V_PAD, s), BF16),
    ]
    out_specs = [
        row(CONV_CH),
        pl.BlockSpec((N_HEADS, HEAD_PAD, rows), lambda i: (0, 0, i)),
        pl.BlockSpec((N_HEADS, rows, HEAD_PAD), lambda i: (0, i, 0)),
        pl.BlockSpec((N_HEADS, V_PAD, rows), lambda i: (0, 0, i)),
    ]
    return pl.pallas_call(
        functools.partial(_proj_kernel, sub=sub),
        grid=(s // rows,),
        in_specs=in_specs,
        out_specs=out_specs,
        out_shape=out_shape,
        compiler_params=_params(("parallel",)),
        name="proj",
    )(x1, pos, mixg, win, qg, wuqt, kvg, wuk, wuvt, qhgt, khg, invft)


def _attn_kernel(qt_ref, k_ref, vt_ref, o_ref, qa_ref, p0_ref, p1_ref, r_ref, pm_ref, a_ref,
                 bad_ref, acc_ref, *, tk):
    nk = k_ref.shape[0] // tk
    nb, _, bw = qa_ref.shape
    p_refs = (p0_ref, p1_ref)
    first_row = lax.broadcasted_iota(jnp.int32, (SUBLANES_BF16, bw), 0) == 0
    blk = lambda b: slice(b * bw, (b + 1) * bw)

    def set_reference(r):
        for b in range(nb):
            qa_ref[b, QK_DIM:QK_DIM + SUBLANES_BF16, :] = jnp.where(
                first_row, -r[:, blk(b)], 0.0).astype(BF16)

    def scores(c, b):
        off = pl.multiple_of(c * tk, tk)
        return jnp.dot(k_ref[pl.ds(off, tk), :], qa_ref[b], preferred_element_type=F32)

    def values(c, slot, alpha):
        off = pl.multiple_of(c * tk, tk)
        for b in range(nb):
            pv = jnp.dot(vt_ref[:, pl.ds(off, tk)], p_refs[slot][b], preferred_element_type=F32)
            acc_ref[b] = alpha[:, blk(b)] * acc_ref[b] + pv

    def emit(c, slot):
        for b in range(nb):
            pb = jnp.exp2(scores(c, b)).astype(BF16)
            p_refs[slot][b] = pb
            pm_ref[:, blk(b)] = jnp.max(pb, axis=0, keepdims=True).astype(F32)

    def step(c, slot):
        r_prev = r_ref[...]
        pm_prev = pm_ref[...]
        r = jnp.maximum(r_prev, r_prev + jnp.log2(pm_prev)).astype(BF16).astype(F32)
        alpha = jnp.exp2(r_prev - r)
        set_reference(r)
        emit(c, slot)
        values(c - 1, 1 - slot, a_ref[...])
        bad_ref[...] = jnp.maximum(bad_ref[...], jnp.where(pm_prev <= 2.0 ** MAX_JUMP, 0.0, 1.0))
        r_ref[...] = r
        a_ref[...] = alpha

    for b in range(nb):
        qa_ref[b] = qt_ref[:, blk(b)]
    emit(0, 0)
    r_ref[...] = jnp.zeros_like(r_ref)
    a_ref[...] = jnp.zeros_like(a_ref)
    bad_ref[...] = jnp.zeros_like(bad_ref)
    acc_ref[...] = jnp.zeros_like(acc_ref)

    def body(t, carry):
        c = 2 * t + 1
        step(c, 1)
        step(c + 1, 0)
        return carry

    lax.fori_loop(0, (nk - 2) // 2, body, 0)
    step(nk - 1, 1)
    values(nk - 1, 1, a_ref[...])
    bad = jnp.maximum(bad_ref[...], jnp.where(pm_ref[...] <= 2.0 ** MAX_JUMP, 0.0, 1.0))
    for b in range(nb):
        low = jnp.where(acc_ref[b, V_DIM:V_DIM + 1, :] >= 2.0 ** -MAX_JUMP, 0.0, 1.0)
        bad_ref[:, blk(b)] = jnp.maximum(bad[:, blk(b)], low)

    @pl.when(jnp.max(bad_ref[...]) > 0.0)
    def _():
        set_reference(jnp.zeros_like(bad))
        r_ref[...] = jnp.full(r_ref.shape, -jnp.inf, F32)
        acc_ref[...] = jnp.zeros_like(acc_ref)

        def exact(c, carry):
            m_prev = r_ref[...]
            for b in range(nb):
                st = scores(c, b)
                m_new = jnp.maximum(m_prev[:, blk(b)], jnp.max(st, axis=0, keepdims=True))
                p0_ref[b] = jnp.exp2(st - m_new).astype(BF16)
                r_ref[:, blk(b)] = m_new
            values(c, 0, jnp.exp2(m_prev - r_ref[...]))
            return carry

        lax.fori_loop(0, nk, exact, 0)

    for b in range(nb):
        o = acc_ref[b, :V_DIM, :] * (1.0 / acc_ref[b, V_DIM:V_DIM + 1, :])
        o_ref[blk(b), :] = o.T.astype(o_ref.dtype)


def _attn(qt, k, vt, *, tq, tk):
    h, s, dp = k.shape
    assert (s // tk) % 2 == 0
    bw = min(ATTN_LANES, tq)
    nb = tq // bw
    stat = pltpu.VMEM((1, tq), F32)
    return pl.pallas_call(
        functools.partial(_attn_kernel, tk=tk),
        grid=(h, s // tq),
        in_specs=[
            pl.BlockSpec((None, dp, tq), lambda hh, i: (hh, 0, i)),
            pl.BlockSpec((None, s, dp), lambda hh, i: (hh, 0, 0)),
            pl.BlockSpec((None, V_PAD, s), lambda hh, i: (hh, 0, 0)),
        ],
        out_specs=pl.BlockSpec((tq, V_DIM), lambda hh, i: (i, hh)),
        out_shape=jax.ShapeDtypeStruct((s, h * V_DIM), BF16),
        scratch_shapes=[pltpu.VMEM((nb, dp, bw), BF16),
                        pltpu.VMEM((nb, tk, bw), BF16), pltpu.VMEM((nb, tk, bw), BF16),
                        stat, stat, stat, stat, pltpu.VMEM((nb, V_PAD, bw), F32)],
        compiler_params=_params(("parallel", "arbitrary")),
        name="attn",
    )(qt, k, vt)


def _conv_kernel(prev_ref, u_ref, next_ref, w_ref, b_ref, g_ref, o_ref, sh_ref, y_ref, sq_ref, *,
                 sub):
    i = pl.program_id(0)
    rows = u_ref.shape[0]
    blocks = CONV_CH // LANES
    has_prev = i > 0
    has_next = i < pl.num_programs(0) - 1
    for c in range(blocks):
        cs = slice(c * LANES, (c + 1) * LANES)
        sh_ref[c, 0, :HALO, :] = jnp.where(has_prev, prev_ref[:, cs], 0.0)
        sh_ref[c, 0, HALO:HALO + rows, :] = u_ref[:, cs]
        sh_ref[c, 0, HALO + rows:, :] = jnp.where(has_next, next_ref[:, cs], 0.0)
    n = rows + 2 * HALO - SUBLANES
    base = HALO - CONV_PAD

    sq_ref[...] = jnp.zeros_like(sq_ref)

    def lane_block(ci, carry):
        cs = pl.ds(pl.multiple_of(ci * LANES, LANES), LANES)
        for b in range(1, SUBLANES):
            sh_ref[ci, b, :n, :] = sh_ref[ci, 0, b:b + n, :]
        for r0 in range(0, rows, sub):
            acc = jnp.broadcast_to(b_ref[:, cs], (sub, LANES))
            for t in range(CONV_WIDTH):
                a, b = divmod(base + t, SUBLANES)
                lo = r0 + a * SUBLANES
                acc = acc + sh_ref[ci, b, lo:lo + sub, :] * w_ref[t:t + 1, cs]
            y_ref[ci, r0:r0 + sub, :] = acc
            sq_ref[r0:r0 + sub, :] += acc * acc
        return carry

    lax.fori_loop(0, blocks, lane_block, 0)
    ssq = jnp.sum(sq_ref[...], axis=-1, keepdims=True)
    inv = lax.rsqrt(ssq * (1.0 / CONV_CH) + EPS)
    for c in range(blocks):
        cs = slice(c * LANES, (c + 1) * LANES)
        y = y_ref[c] * inv * g_ref[:, cs]
        o_ref[:, cs] = (y * jax.nn.sigmoid(y)).astype(o_ref.dtype)


def _conv(u, w, b, g, *, rows, sub):
    s, c = u.shape
    hb = rows // HALO
    last = s // HALO - 1
    return pl.pallas_call(
        functools.partial(_conv_kernel, sub=sub),
        grid=(s // rows,),
        in_specs=[
            pl.BlockSpec((HALO, c), lambda i: (jnp.maximum(i * hb - 1, 0), 0)),
            pl.BlockSpec((rows, c), lambda i: (i, 0)),
            pl.BlockSpec((HALO, c), lambda i: (jnp.minimum((i + 1) * hb, last), 0)),
            _resident(w.shape), _resident(b.shape), _resident(g.shape),
        ],
        out_specs=pl.BlockSpec((rows, c), lambda i: (i, 0)),
        out_shape=jax.ShapeDtypeStruct((s, c), BF16),
        scratch_shapes=[pltpu.VMEM((c // LANES, SUBLANES, rows + 2 * HALO, LANES), F32),
                        pltpu.VMEM((c // LANES, rows, LANES), F32), pltpu.VMEM((rows, LANES), F32)],
        compiler_params=_params(("parallel",)),
        name="conv",
    )(u, u, u, w, b, g)


def _outproj_kernel(x_ref, a_ref, b_ref, wa_ref, wb_ref, o_ref):
    o_ref[...] = (x_ref[...]
                  + jnp.dot(a_ref[...], wa_ref[...], preferred_element_type=F32)
                  + jnp.dot(b_ref[...], wb_ref[...], preferred_element_type=F32))


def _outproj(x1, a, b, wa, wb, *, rows):
    s, d = x1.shape
    row = lambda w: pl.BlockSpec((rows, w), lambda i: (i, 0))
    return pl.pallas_call(
        _outproj_kernel,
        grid=(s // rows,),
        in_specs=[row(d), row(a.shape[1]), row(b.shape[1]), _resident(wa.shape), _resident(wb.shape)],
        out_specs=row(d),
        out_shape=jax.ShapeDtypeStruct((s, d), F32),
        compiler_params=_params(("parallel",)),
        name="outproj",
    )(x1, a, b, wa, wb)


def _pad_cols(w, n):
    return jnp.pad(w, ((0, 0), (0, n - w.shape[1])))


def _cast_cols_kernel(w_ref, o_ref):
    n = w_ref.shape[1]
    o_ref[:, :n] = w_ref[...].astype(BF16)
    o_ref[:, n:] = jnp.zeros((o_ref.shape[0], o_ref.shape[1] - n), BF16)


def _cast_pad_cols(w, n_out):
    r, n = w.shape
    rows = min(CAST_ROWS, r)
    return pl.pallas_call(
        _cast_cols_kernel,
        grid=(r // rows,),
        in_specs=[pl.BlockSpec((rows, n), lambda i: (i, 0))],
        out_specs=pl.BlockSpec((rows, n_out), lambda i: (i, 0)),
        out_shape=jax.ShapeDtypeStruct((r, n_out), BF16),
        compiler_params=_params(("parallel",)),
        name="cast_cols",
    )(w)


def _cast_rows_kernel(w_ref, o_ref):
    n = w_ref.shape[0]
    o_ref[:n, :] = w_ref[...].astype(BF16)
    o_ref[n:, :] = jnp.zeros((o_ref.shape[0] - n, o_ref.shape[1]), BF16)


def _cast_pad_rows(w, n_out):
    n, c = w.shape
    cols = min(CAST_COLS, c)
    return pl.pallas_call(
        _cast_rows_kernel,
        grid=(c // cols,),
        in_specs=[pl.BlockSpec((n, cols), lambda i: (0, i))],
        out_specs=pl.BlockSpec((n_out, cols), lambda i: (0, i)),
        out_shape=jax.ShapeDtypeStruct((n_out, c), BF16),
        compiler_params=_params(("parallel",)),
        name="cast_rows",
    )(w)


def _ffn_weights(wg, wu, wd):
    f = wg.shape[1]
    fp = -(-f // FF_TILE) * FF_TILE
    assert f % LANES == 0
    return _cast_pad_cols(wg, fp), _cast_pad_cols(wu, fp), _cast_pad_rows(wd, fp)


def _head_gain(g):
    return jnp.pad(g, (0, HEAD_PAD - QK_DIM)).reshape(1, HEAD_PAD)


def _layer(x, positions, tiles, ffn1_norm, ffn1_w_gate, ffn1_w_up, ffn1_w_down, mix_norm, w_in,
           q_norm, w_uq, kv_norm, w_ukv, q_head_norm, k_head_norm, dw_kernel, dw_bias, conv_norm,
           w_out, ffn2_norm, ffn2_w_gate, ffn2_w_up, ffn2_w_down, final_norm):
    s, d = x.shape
    row = lambda g: g.reshape(1, -1)

    x1 = _ffn(x, row(ffn1_norm), *_ffn_weights(ffn1_w_gate, ffn1_w_up, ffn1_w_down), None,
              rows=tiles["ffn"])

    c2 = 2 * CONV_CH + Q_RANK + KV_RANK
    win = _pad_cols(w_in.astype(BF16), c2 + LANES)
    wuq = jnp.pad(w_uq.reshape(Q_RANK, N_HEADS, QK_DIM),
                  ((0, 0), (0, 0), (0, HEAD_PAD - QK_DIM))).reshape(Q_RANK, N_HEADS * HEAD_PAD)
    wkv = w_ukv.reshape(KV_RANK, N_HEADS, NOPE + V_DIM)
    wuk = wkv[:, :, :NOPE].reshape(KV_RANK, N_HEADS * NOPE)
    wuv = wkv[:, :, NOPE:].reshape(KV_RANK, N_HEADS * V_DIM)
    inv_freq = ROPE_THETA ** (-(jnp.arange(ROPE // 2, dtype=F32) * 2.0 / ROPE))
    u, qt, k, vt = _proj(x1, positions.reshape(1, s), row(mix_norm), win, row(q_norm),
                         wuq.T.astype(BF16), row(kv_norm), wuk.astype(BF16), wuv.T.astype(BF16),
                         _head_gain(q_head_norm), _head_gain(k_head_norm), inv_freq,
                         rows=tiles["proj"])

    b_out = _attn(qt, k, vt, tq=tiles["attn_q"], tk=tiles["attn_k"])
    a_out = _conv(u, jnp.pad(dw_kernel, ((0, 1), (0, 0))), row(dw_bias), row(conv_norm),
                  rows=tiles["conv"], sub=tiles["conv_sub"])
    wo = w_out.astype(BF16)
    x2 = _outproj(x1, a_out, b_out, wo[:CONV_CH], wo[CONV_CH:], rows=tiles["out"])

    return _ffn(x2, row(ffn2_norm), *_ffn_weights(ffn2_w_gate, ffn2_w_up, ffn2_w_down),
                row(final_norm), rows=tiles["ffn"])


def _tiles(s):
    return {
        "ffn": min(FFN_ROWS, s), "proj": min(PROJ_ROWS, s), "attn_q": min(ATTN_Q, s),
        "attn_k": min(ATTN_K, s), "conv": min(CONV_ROWS, s), "conv_sub": CONV_SUB,
        "out": min(OUT_ROWS, s),
    }


def kernel(x, positions, ffn1_norm, ffn1_w_gate, ffn1_w_up, ffn1_w_down, mix_norm, w_in, q_norm, w_uq, kv_norm, w_ukv, q_head_norm, k_head_norm, dw_kernel, dw_bias, conv_norm, w_out, ffn2_norm, ffn2_w_gate, ffn2_w_up, ffn2_w_down, final_norm):
    batch, s, d = x.shape
    assert ffn1_norm.shape[0] == 1, "single-layer stack expected"
    outs = []
    for bi in range(batch):
        outs.append(_layer(
            x[bi], positions[bi], _tiles(s), ffn1_norm[0], ffn1_w_gate[0], ffn1_w_up[0],
            ffn1_w_down[0], mix_norm[0], w_in[0], q_norm[0], w_uq[0], kv_norm[0], w_ukv[0],
            q_head_norm[0], k_head_norm[0], dw_kernel[0], dw_bias[0], conv_norm[0], w_out[0],
            ffn2_norm[0], ffn2_w_gate[0], ffn2_w_up[0], ffn2_w_down[0], final_norm[0]))
    return jnp.stack(outs)
```

```python
import functools
import math

import jax
import jax.numpy as jnp
from jax import lax
from jax.experimental import pallas as pl
from jax.experimental.pallas import tpu as pltpu

F32 = jnp.float32
BF16 = jnp.bfloat16

EPS = 1e-6
ROPE_THETA = 10000.0
CONV_CH = 1024
CONV_WIDTH = 31
CONV_PAD = CONV_WIDTH // 2
N_HEADS = 8
NOPE = 128
ROPE = 64
QK_DIM = NOPE + ROPE
V_DIM = 128
Q_RANK = 512
KV_RANK = 256

LANES = 128
SUBLANES = 8
SUBLANES_BF16 = 16
HEAD_PAD = 2 * LANES
V_PAD = V_DIM + SUBLANES_BF16
VMEM_LIMIT = 56 * 1024 * 1024

FF_TILE = 1024
FF_UNIT = 256
FFN_ROWS = 512
PROJ_ROWS = 512
PROJ_SUB = 256
ATTN_Q = 1024
ATTN_K = 1024
CONV_ROWS = 256
CONV_SUB = 256
OUT_ROWS = 512
CAST_ROWS = 256
CAST_COLS = 256
HALO = 16
MAX_JUMP = 64.0


def _rms(x, g):
    ms = jnp.mean(x * x, axis=-1, keepdims=True)
    return x * lax.rsqrt(ms + EPS) * g


def _params(sem):
    return pltpu.CompilerParams(dimension_semantics=sem, vmem_limit_bytes=VMEM_LIMIT)


def _resident(shape):
    nd = len(shape)
    return pl.BlockSpec(shape, lambda *_: (0,) * nd, pipeline_mode=pl.Buffered(1))


def _ffn_kernel(*refs, final_norm, tail):
    if final_norm:
        x_ref, g_ref, wg_ref, wu_ref, wd_ref, fg_ref, o_ref, xn_ref = refs
    else:
        x_ref, g_ref, wg_ref, wu_ref, wd_ref, o_ref, xn_ref = refs
    f = pl.program_id(1)
    last = pl.num_programs(1) - 1

    def hidden_tile(width):
        xn = xn_ref[...]
        gate = jnp.dot(xn, wg_ref[:, :width], preferred_element_type=F32)
        up = jnp.dot(xn, wu_ref[:, :width], preferred_element_type=F32)
        h = (gate * jax.nn.sigmoid(gate) * up).astype(BF16)
        o_ref[...] += jnp.dot(h, wd_ref[:width, :], preferred_element_type=F32)

    @pl.when(f == 0)
    def _():
        xn_ref[...] = _rms(x_ref[...], g_ref[...]).astype(BF16)
        o_ref[...] = jnp.zeros_like(o_ref)

    @pl.when(f < last)
    def _():
        hidden_tile(wg_ref.shape[1])

    @pl.when(f == last)
    def _():
        hidden_tile(tail)
        y = x_ref[...] + 0.5 * o_ref[...]
        if final_norm:
            y = _rms(y, fg_ref[...])
        o_ref[...] = y


def _ffn(x, g, wg, wu, wd, final_g, *, rows, hidden):
    s, d = x.shape
    fp = wg.shape[1]
    tf = FF_TILE
    tail = -(-hidden // FF_UNIT) * FF_UNIT - (fp - tf)
    assert fp % tf == 0 and 0 < tail <= tf
    in_specs = [
        pl.BlockSpec((rows, d), lambda i, f: (i, 0)),
        pl.BlockSpec((1, d), lambda i, f: (0, 0)),
        pl.BlockSpec((d, tf), lambda i, f: (0, f)),
        pl.BlockSpec((d, tf), lambda i, f: (0, f)),
        pl.BlockSpec((tf, d), lambda i, f: (f, 0)),
    ]
    args = [x, g, wg, wu, wd]
    if final_g is not None:
        in_specs.append(pl.BlockSpec((1, d), lambda i, f: (0, 0)))
        args.append(final_g)
    return pl.pallas_call(
        functools.partial(_ffn_kernel, final_norm=final_g is not None, tail=tail),
        grid=(s // rows, fp // tf),
        in_specs=in_specs,
        out_specs=pl.BlockSpec((rows, d), lambda i, f: (i, 0)),
        out_shape=jax.ShapeDtypeStruct((s, d), F32),
        scratch_shapes=[pltpu.VMEM((rows, d), BF16)],
        compiler_params=_params(("parallel", "arbitrary")),
        name="ffn_final" if final_g is not None else "ffn",
    )(*args)


def _proj_kernel(x_ref, pos_ref, mixg_ref, win_ref, qg_ref, wuqt_ref, kvg_ref, wuk_ref, wuvt_ref,
                 qhgt_ref, khg_ref, invft_ref, u_ref, qt_ref, k_ref, vt_ref, *, sub):
    c0 = 2 * CONV_CH
    c1 = c0 + Q_RANK
    c2 = c1 + KV_RANK
    half = ROPE // 2
    inv_d = 1.0 / QK_DIM
    scale = math.log2(math.e) / math.sqrt(QK_DIM)
    qhgt = qhgt_ref[...]
    khg = khg_ref[...]
    lane = lax.broadcasted_iota(jnp.int32, (sub, LANES), 1)
    one_col = jnp.where(lane == ROPE, 1.0, 0.0)
    ones_rows = jnp.where(
        lax.broadcasted_iota(jnp.int32, (V_PAD - V_DIM, sub), 0) == 0, 1.0, 0.0).astype(BF16)
    q_tail = jnp.zeros((HEAD_PAD - QK_DIM, sub), BF16)

    def part(r0):
        rs = slice(r0, r0 + sub)
        hn = _rms(x_ref[rs, :], mixg_ref[...]).astype(BF16)
        z = jnp.dot(hn, win_ref[...], preferred_element_type=F32)
        u_ref[rs, :] = z[:, :CONV_CH] * jax.nn.sigmoid(z[:, CONV_CH:c0])
        qnt = _rms(z[:, c0:c1], qg_ref[...]).T.astype(BF16)
        kvn = _rms(z[:, c1:c2], kvg_ref[...])
        kvnt = kvn.T.astype(BF16)
        kpe = z[:, c2:c2 + LANES]
        qt = jnp.dot(wuqt_ref[...], qnt, preferred_element_type=F32)
        vt = jnp.dot(wuvt_ref[...], kvnt, preferred_element_type=F32)
        kn = jnp.dot(kvn.astype(BF16), wuk_ref[...], preferred_element_type=F32)

        ang = invft_ref[...] * pos_ref[:, rs].astype(F32)
        cos_t = jnp.cos(ang)
        sin_t = jnp.sin(ang)
        zeros = jnp.zeros((LANES - ROPE, sub), F32)
        cos_k = jnp.concatenate([cos_t, cos_t, zeros], axis=0).T
        sin_k = jnp.concatenate([-sin_t, sin_t, zeros], axis=0).T

        kpe_g = kpe * khg[:, NOPE:]
        kpe_rot = kpe_g * cos_k + jnp.where(lane < half, pltpu.roll(kpe_g, LANES - half, 1),
                                            pltpu.roll(kpe_g, half, 1)) * sin_k
        kpe_ss = jnp.sum(kpe * kpe, axis=-1, keepdims=True)
        for h in range(N_HEADS):
            qh = qt[h * HEAD_PAD:h * HEAD_PAD + QK_DIM, :]
            r = lax.rsqrt(jnp.sum(qh * qh, axis=0, keepdims=True) * inv_d + EPS) * scale
            qh = qh * qhgt[:QK_DIM, :] * r
            x1 = qh[NOPE:NOPE + half, :]
            x2 = qh[NOPE + half:, :]
            qt_ref[h, :NOPE, rs] = qh[:NOPE, :].astype(BF16)
            qt_ref[h, NOPE:NOPE + half, rs] = (x1 * cos_t - x2 * sin_t).astype(BF16)
            qt_ref[h, NOPE + half:QK_DIM, rs] = (x2 * cos_t + x1 * sin_t).astype(BF16)
            qt_ref[h, QK_DIM:, rs] = q_tail
            ka = kn[:, h * NOPE:(h + 1) * NOPE]
            rk = lax.rsqrt((jnp.sum(ka * ka, axis=-1, keepdims=True) + kpe_ss) * inv_d + EPS)
            k_ref[h, rs, :NOPE] = (ka * rk * khg[:, :NOPE]).astype(BF16)
            k_ref[h, rs, NOPE:] = (kpe_rot * rk + one_col).astype(BF16)
            vt_ref[h, :V_DIM, rs] = vt[h * V_DIM:(h + 1) * V_DIM, :].astype(BF16)
            vt_ref[h, V_DIM:, rs] = ones_rows

    for r0 in range(0, x_ref.shape[0], sub):
        part(r0)


def _proj(x1, pos, mixg, win, qg, wuqt, kvg, wuk, wuvt, qhg, khg, inv_freq, *, rows):
    s, d = x1.shape
    sub = min(PROJ_SUB, rows)
    qhgt = jnp.broadcast_to(qhg.reshape(HEAD_PAD, 1), (HEAD_PAD, sub))
    invft = jnp.broadcast_to(inv_freq.reshape(-1, 1), (ROPE // 2, sub))
    row = lambda w: pl.BlockSpec((rows, w), lambda i: (i, 0))
    in_specs = [row(d), pl.BlockSpec((1, rows), lambda i: (0, i)), _resident(mixg.shape),
                _resident(win.shape), _resident(qg.shape), _resident(wuqt.shape),
                _resident(kvg.shape), _resident(wuk.shape), _resident(wuvt.shape),
                _resident(qhgt.shape), _resident(khg.shape), _resident(invft.shape)]
    out_shape = [
        jax.ShapeDtypeStruct((s, CONV_CH), F32),
        jax.ShapeDtypeStruct((N_HEADS, HEAD_PAD, s), BF16),
        jax.ShapeDtypeStruct((N_HEADS, s, HEAD_PAD), BF16),
        jax.ShapeDtypeStruct((N_HEADS, V_PAD, s), BF16),
    ]
    out_specs = [
        row(CONV_CH),
        pl.BlockSpec((N_HEADS, HEAD_PAD, rows), lambda i: (0, 0, i)),
        pl.BlockSpec((N_HEADS, rows, HEAD_PAD), lambda i: (0, i, 0)),
        pl.BlockSpec((N_HEADS, V_PAD, rows), lambda i: (0, 0, i)),
    ]
    return pl.pallas_call(
        functools.partial(_proj_kernel, sub=sub),
        grid=(s // rows,),
        in_specs=in_specs,
        out_specs=out_specs,
        out_shape=out_shape,
        compiler_params=_params(("parallel",)),
        name="proj",
    )(x1, pos, mixg, win, qg, wuqt, kvg, wuk, wuvt, qhgt, khg, invft)


def _attn_kernel(qt_ref, k_ref, vt_ref, o_ref, qa_ref, p0_ref, p1_ref, r_ref, pm_ref, a_ref,
                 bad_ref, acc_ref, *, tk):
    nk = k_ref.shape[0] // tk
    tq = qt_ref.shape[1]
    p_refs = (p0_ref, p1_ref)
    first_row = lax.broadcasted_iota(jnp.int32, (SUBLANES_BF16, tq), 0) == 0

    def set_reference(r):
        qa_ref[QK_DIM:QK_DIM + SUBLANES_BF16, :] = jnp.where(first_row, -r, 0.0).astype(BF16)

    def scores(c):
        off = pl.multiple_of(c * tk, tk)
        return jnp.dot(k_ref[pl.ds(off, tk), :], qa_ref[...], preferred_element_type=F32)

    def values(c, slot, alpha):
        off = pl.multiple_of(c * tk, tk)
        pv = jnp.dot(vt_ref[:, pl.ds(off, tk)], p_refs[slot][...], preferred_element_type=F32)
        acc_ref[...] = alpha * acc_ref[...] + pv

    def emit(p, slot):
        pb = p.astype(BF16)
        p_refs[slot][...] = pb
        return jnp.max(pb, axis=0, keepdims=True).astype(F32)

    def step(c, slot):
        r_prev = r_ref[...]
        pm_prev = pm_ref[...]
        r = jnp.maximum(r_prev, r_prev + jnp.log2(pm_prev)).astype(BF16).astype(F32)
        alpha = jnp.exp2(r_prev - r)
        set_reference(r)
        pm_ref[...] = emit(jnp.exp2(scores(c)), slot)
        values(c - 1, 1 - slot, a_ref[...])
        bad_ref[...] = jnp.maximum(bad_ref[...], jnp.where(pm_prev <= 2.0 ** MAX_JUMP, 0.0, 1.0))
        r_ref[...] = r
        a_ref[...] = alpha

    qa_ref[...] = qt_ref[...]
    pm_ref[...] = emit(jnp.exp2(scores(0)), 0)
    r_ref[...] = jnp.zeros_like(r_ref)
    a_ref[...] = jnp.zeros_like(a_ref)
    bad_ref[...] = jnp.zeros_like(bad_ref)
    acc_ref[...] = jnp.zeros_like(acc_ref)

    def body(t, carry):
        c = 2 * t + 1
        step(c, 1)
        step(c + 1, 0)
        return carry

    lax.fori_loop(0, (nk - 2) // 2, body, 0)
    step(nk - 1, 1)
    values(nk - 1, 1, a_ref[...])
    bad = jnp.maximum(bad_ref[...], jnp.where(pm_ref[...] <= 2.0 ** MAX_JUMP, 0.0, 1.0))
    bad = jnp.maximum(bad, jnp.where(acc_ref[V_DIM:V_DIM + 1, :] >= 2.0 ** -MAX_JUMP, 0.0, 1.0))

    @pl.when(jnp.max(bad) > 0.0)
    def _():
        set_reference(jnp.zeros_like(bad))
        r_ref[...] = jnp.full(r_ref.shape, -jnp.inf, F32)
        acc_ref[...] = jnp.zeros_like(acc_ref)

        def exact(c, carry):
            st = scores(c)
            m_prev = r_ref[...]
            m_new = jnp.maximum(m_prev, jnp.max(st, axis=0, keepdims=True))
            p0_ref[...] = jnp.exp2(st - m_new).astype(BF16)
            values(c, 0, jnp.exp2(m_prev - m_new))
            r_ref[...] = m_new
            return carry

        lax.fori_loop(0, nk, exact, 0)

    o = acc_ref[:V_DIM, :] * (1.0 / acc_ref[V_DIM:V_DIM + 1, :])
    o_ref[...] = o.T.astype(o_ref.dtype)


def _attn(qt, k, vt, *, tq, tk):
    h, s, dp = k.shape
    assert (s // tk) % 2 == 0
    stat = pltpu.VMEM((1, tq), F32)
    return pl.pallas_call(
        functools.partial(_attn_kernel, tk=tk),
        grid=(h, s // tq),
        in_specs=[
            pl.BlockSpec((None, dp, tq), lambda hh, i: (hh, 0, i)),
            pl.BlockSpec((None, s, dp), lambda hh, i: (hh, 0, 0)),
            pl.BlockSpec((None, V_PAD, s), lambda hh, i: (hh, 0, 0)),
        ],
        out_specs=pl.BlockSpec((tq, V_DIM), lambda hh, i: (i, hh)),
        out_shape=jax.ShapeDtypeStruct((s, h * V_DIM), BF16),
        scratch_shapes=[pltpu.VMEM((dp, tq), BF16),
                        pltpu.VMEM((tk, tq), BF16), pltpu.VMEM((tk, tq), BF16),
                        stat, stat, stat, stat, pltpu.VMEM((V_PAD, tq), F32)],
        compiler_params=_params(("parallel", "arbitrary")),
        name="attn",
    )(qt, k, vt)


def _conv_kernel(prev_ref, u_ref, next_ref, w_ref, b_ref, g_ref, o_ref, sh_ref, y_ref, sq_ref, *,
                 sub):
    i = pl.program_id(0)
    rows = u_ref.shape[0]
    blocks = CONV_CH // LANES
    has_prev = i > 0
    has_next = i < pl.num_programs(0) - 1
    for c in range(blocks):
        cs = slice(c * LANES, (c + 1) * LANES)
        sh_ref[c, 0, :HALO, :] = jnp.where(has_prev, prev_ref[:, cs], 0.0)
        sh_ref[c, 0, HALO:HALO + rows, :] = u_ref[:, cs]
        sh_ref[c, 0, HALO + rows:, :] = jnp.where(has_next, next_ref[:, cs], 0.0)
    n = rows + 2 * HALO - SUBLANES
    base = HALO - CONV_PAD

    sq_ref[...] = jnp.zeros_like(sq_ref)

    def lane_block(ci, carry):
        cs = pl.ds(pl.multiple_of(ci * LANES, LANES), LANES)
        for b in range(1, SUBLANES):
            sh_ref[ci, b, :n, :] = sh_ref[ci, 0, b:b + n, :]
        for r0 in range(0, rows, sub):
            acc = jnp.broadcast_to(b_ref[:, cs], (sub, LANES))
            for t in range(CONV_WIDTH):
                a, b = divmod(base + t, SUBLANES)
                lo = r0 + a * SUBLANES
                acc = acc + sh_ref[ci, b, lo:lo + sub, :] * w_ref[t:t + 1, cs]
            y_ref[ci, r0:r0 + sub, :] = acc
            sq_ref[r0:r0 + sub, :] += acc * acc
        return carry

    lax.fori_loop(0, blocks, lane_block, 0)
    ssq = jnp.sum(sq_ref[...], axis=-1, keepdims=True)
    inv = lax.rsqrt(ssq * (1.0 / CONV_CH) + EPS)
    for c in range(blocks):
        cs = slice(c * LANES, (c + 1) * LANES)
        y = y_ref[c] * inv * g_ref[:, cs]
        o_ref[:, cs] = (y * jax.nn.sigmoid(y)).astype(o_ref.dtype)


def _conv(u, w, b, g, *, rows, sub):
    s, c = u.shape
    hb = rows // HALO
    last = s // HALO - 1
    return pl.pallas_call(
        functools.partial(_conv_kernel, sub=sub),
        grid=(s // rows,),
        in_specs=[
            pl.BlockSpec((HALO, c), lambda i: (jnp.maximum(i * hb - 1, 0), 0)),
            pl.BlockSpec((rows, c), lambda i: (i, 0)),
            pl.BlockSpec((HALO, c), lambda i: (jnp.minimum((i + 1) * hb, last), 0)),
            _resident(w.shape), _resident(b.shape), _resident(g.shape),
        ],
        out_specs=pl.BlockSpec((rows, c), lambda i: (i, 0)),
        out_shape=jax.ShapeDtypeStruct((s, c), BF16),
        scratch_shapes=[pltpu.VMEM((c // LANES, SUBLANES, rows + 2 * HALO, LANES), F32),
                        pltpu.VMEM((c // LANES, rows, LANES), F32), pltpu.VMEM((rows, LANES), F32)],
        compiler_params=_params(("parallel",)),
        name="conv",
    )(u, u, u, w, b, g)


def _outproj_kernel(x_ref, a_ref, b_ref, wa_ref, wb_ref, o_ref):
    o_ref[...] = (x_ref[...]
                  + jnp.dot(a_ref[...], wa_ref[...], preferred_element_type=F32)
                  + jnp.dot(b_ref[...], wb_ref[...], preferred_element_type=F32))


def _outproj(x1, a, b, wa, wb, *, rows):
    s, d = x1.shape
    row = lambda w: pl.BlockSpec((rows, w), lambda i: (i, 0))
    return pl.pallas_call(
        _outproj_kernel,
        grid=(s // rows,),
        in_specs=[row(d), row(a.shape[1]), row(b.shape[1]), _resident(wa.shape), _resident(wb.shape)],
        out_specs=row(d),
        out_shape=jax.ShapeDtypeStruct((s, d), F32),
        compiler_params=_params(("parallel",)),
        name="outproj",
    )(x1, a, b, wa, wb)


def _pad_cols(w, n):
    return jnp.pad(w, ((0, 0), (0, n - w.shape[1])))


def _cast_cols_kernel(w_ref, o_ref):
    n = w_ref.shape[1]
    o_ref[:, :n] = w_ref[...].astype(BF16)
    o_ref[:, n:] = jnp.zeros((o_ref.shape[0], o_ref.shape[1] - n), BF16)


def _cast_pad_cols(w, n_out):
    r, n = w.shape
    rows = min(CAST_ROWS, r)
    return pl.pallas_call(
        _cast_cols_kernel,
        grid=(r // rows,),
        in_specs=[pl.BlockSpec((rows, n), lambda i: (i, 0))],
        out_specs=pl.BlockSpec((rows, n_out), lambda i: (i, 0)),
        out_shape=jax.ShapeDtypeStruct((r, n_out), BF16),
        compiler_params=_params(("parallel",)),
        name="cast_cols",
    )(w)


def _cast_rows_kernel(w_ref, o_ref):
    n = w_ref.shape[0]
    o_ref[:n, :] = w_ref[...].astype(BF16)
    o_ref[n:, :] = jnp.zeros((o_ref.shape[0] - n, o_ref.shape[1]), BF16)


def _cast_pad_rows(w, n_out):
    n, c = w.shape
    cols = min(CAST_COLS, c)
    return pl.pallas_call(
        _cast_rows_kernel,
        grid=(c // cols,),
        in_specs=[pl.BlockSpec((n, cols), lambda i: (0, i))],
        out_specs=pl.BlockSpec((n_out, cols), lambda i: (0, i)),
        out_shape=jax.ShapeDtypeStruct((n_out, c), BF16),
        compiler_params=_params(("parallel",)),
        name="cast_rows",
    )(w)


def _ffn_weights(wg, wu, wd):
    f = wg.shape[1]
    fp = -(-f // FF_TILE) * FF_TILE
    assert f % LANES == 0
    return _cast_pad_cols(wg, fp), _cast_pad_cols(wu, fp), _cast_pad_rows(wd, fp)


def _head_gain(g):
    return jnp.pad(g, (0, HEAD_PAD - QK_DIM)).reshape(1, HEAD_PAD)


def _layer(x, positions, tiles, ffn1_norm, ffn1_w_gate, ffn1_w_up, ffn1_w_down, mix_norm, w_in,
           q_norm, w_uq, kv_norm, w_ukv, q_head_norm, k_head_norm, dw_kernel, dw_bias, conv_norm,
           w_out, ffn2_norm, ffn2_w_gate, ffn2_w_up, ffn2_w_down, final_norm):
    s, d = x.shape
    row = lambda g: g.reshape(1, -1)

    hidden = ffn1_w_gate.shape[1]
    x1 = _ffn(x, row(ffn1_norm), *_ffn_weights(ffn1_w_gate, ffn1_w_up, ffn1_w_down), None,
              rows=tiles["ffn"], hidden=hidden)

    c2 = 2 * CONV_CH + Q_RANK + KV_RANK
    win = _pad_cols(w_in.astype(BF16), c2 + LANES)
    wuq = jnp.pad(w_uq.reshape(Q_RANK, N_HEADS, QK_DIM),
                  ((0, 0), (0, 0), (0, HEAD_PAD - QK_DIM))).reshape(Q_RANK, N_HEADS * HEAD_PAD)
    wkv = w_ukv.reshape(KV_RANK, N_HEADS, NOPE + V_DIM)
    wuk = wkv[:, :, :NOPE].reshape(KV_RANK, N_HEADS * NOPE)
    wuv = wkv[:, :, NOPE:].reshape(KV_RANK, N_HEADS * V_DIM)
    inv_freq = ROPE_THETA ** (-(jnp.arange(ROPE // 2, dtype=F32) * 2.0 / ROPE))
    u, qt, k, vt = _proj(x1, positions.reshape(1, s), row(mix_norm), win, row(q_norm),
                         wuq.T.astype(BF16), row(kv_norm), wuk.astype(BF16), wuv.T.astype(BF16),
                         _head_gain(q_head_norm), _head_gain(k_head_norm), inv_freq,
                         rows=tiles["proj"])

    b_out = _attn(qt, k, vt, tq=tiles["attn_q"], tk=tiles["attn_k"])
    a_out = _conv(u, jnp.pad(dw_kernel, ((0, 1), (0, 0))), row(dw_bias), row(conv_norm),
                  rows=tiles["conv"], sub=tiles["conv_sub"])
    wo = w_out.astype(BF16)
    x2 = _outproj(x1, a_out, b_out, wo[:CONV_CH], wo[CONV_CH:], rows=tiles["out"])

    return _ffn(x2, row(ffn2_norm), *_ffn_weights(ffn2_w_gate, ffn2_w_up, ffn2_w_down),
                row(final_norm), rows=tiles["ffn"], hidden=hidden)


def _tiles(s):
    return {
        "ffn": min(FFN_ROWS, s), "proj": min(PROJ_ROWS, s), "attn_q": min(ATTN_Q, s),
        "attn_k": min(ATTN_K, s), "conv": min(CONV_ROWS, s), "conv_sub": CONV_SUB,
        "out": min(OUT_ROWS, s),
    }


def kernel(x, positions, ffn1_norm, ffn1_w_gate, ffn1_w_up, ffn1_w_down, mix_norm, w_in, q_norm, w_uq, kv_norm, w_ukv, q_head_norm, k_head_norm, dw_kernel, dw_bias, conv_norm, w_out, ffn2_norm, ffn2_w_gate, ffn2_w_up, ffn2_w_down, final_norm):
    batch, s, d = x.shape
    assert ffn1_norm.shape[0] == 1, "single-layer stack expected"
    outs = []
    for bi in range(batch):
        outs.append(_layer(
            x[bi], positions[bi], _tiles(s), ffn1_norm[0], ffn1_w_gate[0], ffn1_w_up[0],
            ffn1_w_down[0], mix_norm[0], w_in[0], q_norm[0], w_uq[0], kv_norm[0], w_ukv[0],
            q_head_norm[0], k_head_norm[0], dw_kernel[0], dw_bias[0], conv_norm[0], w_out[0],
            ffn2_norm[0], ffn2_w_gate[0], ffn2_w_up[0], ffn2_w_down[0], final_norm[0]))
    return jnp.stack(outs)
```

```python
import functools
import math

import jax
import jax.numpy as jnp
from jax import lax
from jax.experimental import pallas as pl
from jax.experimental.pallas import tpu as pltpu

F32 = jnp.float32
BF16 = jnp.bfloat16

EPS = 1e-6
ROPE_THETA = 10000.0
CONV_CH = 1024
CONV_WIDTH = 31
CONV_PAD = CONV_WIDTH // 2
N_HEADS = 8
NOPE = 128
ROPE = 64
QK_DIM = NOPE + ROPE
V_DIM = 128
Q_RANK = 512
KV_RANK = 256

LANES = 128
SUBLANES = 8
SUBLANES_BF16 = 16
HEAD_PAD = 2 * LANES
V_PAD = V_DIM + SUBLANES_BF16
VMEM_LIMIT = 56 * 1024 * 1024

FF_TILE = 512
FFN_ROWS = 512
PROJ_ROWS = 512
PROJ_SUB = 256
ATTN_Q = 1024
ATTN_K = 1024
CONV_ROWS = 256
CONV_SUB = 256
OUT_ROWS = 512
CAST_ROWS = 256
CAST_COLS = 256
HALO = 16
MAX_JUMP = 64.0


def _rms(x, g):
    ms = jnp.mean(x * x, axis=-1, keepdims=True)
    return x * lax.rsqrt(ms + EPS) * g


def _params(sem):
    return pltpu.CompilerParams(dimension_semantics=sem, vmem_limit_bytes=VMEM_LIMIT)


def _resident(shape):
    nd = len(shape)
    return pl.BlockSpec(shape, lambda *_: (0,) * nd, pipeline_mode=pl.Buffered(1))


def _ffn_kernel(*refs, final_norm):
    if final_norm:
        x_ref, g_ref, wg_ref, wu_ref, wd_ref, fg_ref, o_ref, xn_ref = refs
    else:
        x_ref, g_ref, wg_ref, wu_ref, wd_ref, o_ref, xn_ref = refs
    f = pl.program_id(1)

    @pl.when(f == 0)
    def _():
        xn_ref[...] = _rms(x_ref[...], g_ref[...]).astype(BF16)
        o_ref[...] = jnp.zeros_like(o_ref)

    xn = xn_ref[...]
    gate = jnp.dot(xn, wg_ref[...], preferred_element_type=F32)
    up = jnp.dot(xn, wu_ref[...], preferred_element_type=F32)
    h = (gate * jax.nn.sigmoid(gate) * up).astype(BF16)
    o_ref[...] += jnp.dot(h, wd_ref[...], preferred_element_type=F32)

    @pl.when(f == pl.num_programs(1) - 1)
    def _():
        y = x_ref[...] + 0.5 * o_ref[...]
        if final_norm:
            y = _rms(y, fg_ref[...])
        o_ref[...] = y


def _ffn(x, g, wg, wu, wd, final_g, *, rows):
    s, d = x.shape
    fp = wg.shape[1]
    tf = min(FF_TILE, fp)
    in_specs = [
        pl.BlockSpec((rows, d), lambda i, f: (i, 0)),
        pl.BlockSpec((1, d), lambda i, f: (0, 0)),
        pl.BlockSpec((d, tf), lambda i, f: (0, f)),
        pl.BlockSpec((d, tf), lambda i, f: (0, f)),
        pl.BlockSpec((tf, d), lambda i, f: (f, 0)),
    ]
    args = [x, g, wg, wu, wd]
    if final_g is not None:
        in_specs.append(pl.BlockSpec((1, d), lambda i, f: (0, 0)))
        args.append(final_g)
    return pl.pallas_call(
        functools.partial(_ffn_kernel, final_norm=final_g is not None),
        grid=(s // rows, fp // tf),
        in_specs=in_specs,
        out_specs=pl.BlockSpec((rows, d), lambda i, f: (i, 0)),
        out_shape=jax.ShapeDtypeStruct((s, d), F32),
        scratch_shapes=[pltpu.VMEM((rows, d), BF16)],
        compiler_params=_params(("parallel", "arbitrary")),
        name="ffn_final" if final_g is not None else "ffn",
    )(*args)


def _proj_kernel(x_ref, pos_ref, mixg_ref, win_ref, qg_ref, wuqt_ref, kvg_ref, wuk_ref, wuvt_ref,
                 qhgt_ref, khg_ref, invft_ref, wg_ref, wu_ref,
                 u_ref, qt_ref, k_ref, vt_ref, wgo_ref, wuo_ref, *, sub):
    _cast_cols_kernel(wg_ref, wgo_ref)
    _cast_cols_kernel(wu_ref, wuo_ref)

    c0 = 2 * CONV_CH
    c1 = c0 + Q_RANK
    c2 = c1 + KV_RANK
    half = ROPE // 2
    inv_d = 1.0 / QK_DIM
    scale = math.log2(math.e) / math.sqrt(QK_DIM)
    qhgt = qhgt_ref[...]
    khg = khg_ref[...]
    lane = lax.broadcasted_iota(jnp.int32, (sub, LANES), 1)
    one_col = jnp.where(lane == ROPE, 1.0, 0.0)
    ones_rows = jnp.where(
        lax.broadcasted_iota(jnp.int32, (V_PAD - V_DIM, sub), 0) == 0, 1.0, 0.0).astype(BF16)
    q_tail = jnp.zeros((HEAD_PAD - QK_DIM, sub), BF16)

    def part(r0):
        rs = slice(r0, r0 + sub)
        hn = _rms(x_ref[rs, :], mixg_ref[...]).astype(BF16)
        z = jnp.dot(hn, win_ref[...], preferred_element_type=F32)
        u_ref[rs, :] = z[:, :CONV_CH] * jax.nn.sigmoid(z[:, CONV_CH:c0])
        qnt = _rms(z[:, c0:c1], qg_ref[...]).T.astype(BF16)
        kvn = _rms(z[:, c1:c2], kvg_ref[...])
        kvnt = kvn.T.astype(BF16)
        kpe = z[:, c2:c2 + LANES]
        qt = jnp.dot(wuqt_ref[...], qnt, preferred_element_type=F32)
        vt = jnp.dot(wuvt_ref[...], kvnt, preferred_element_type=F32)
        kn = jnp.dot(kvn.astype(BF16), wuk_ref[...], preferred_element_type=F32)

        ang = invft_ref[...] * pos_ref[:, rs].astype(F32)
        cos_t = jnp.cos(ang)
        sin_t = jnp.sin(ang)
        zeros = jnp.zeros((LANES - ROPE, sub), F32)
        cos_k = jnp.concatenate([cos_t, cos_t, zeros], axis=0).T
        sin_k = jnp.concatenate([-sin_t, sin_t, zeros], axis=0).T

        kpe_g = kpe * khg[:, NOPE:]
        kpe_rot = kpe_g * cos_k + jnp.where(lane < half, pltpu.roll(kpe_g, LANES - half, 1),
                                            pltpu.roll(kpe_g, half, 1)) * sin_k
        kpe_ss = jnp.sum(kpe * kpe, axis=-1, keepdims=True)
        for h in range(N_HEADS):
            qh = qt[h * HEAD_PAD:h * HEAD_PAD + QK_DIM, :]
            r = lax.rsqrt(jnp.sum(qh * qh, axis=0, keepdims=True) * inv_d + EPS) * scale
            qh = qh * qhgt[:QK_DIM, :] * r
            x1 = qh[NOPE:NOPE + half, :]
            x2 = qh[NOPE + half:, :]
            qt_ref[h, :NOPE, rs] = qh[:NOPE, :].astype(BF16)
            qt_ref[h, NOPE:NOPE + half, rs] = (x1 * cos_t - x2 * sin_t).astype(BF16)
            qt_ref[h, NOPE + half:QK_DIM, rs] = (x2 * cos_t + x1 * sin_t).astype(BF16)
            qt_ref[h, QK_DIM:, rs] = q_tail
            ka = kn[:, h * NOPE:(h + 1) * NOPE]
            rk = lax.rsqrt((jnp.sum(ka * ka, axis=-1, keepdims=True) + kpe_ss) * inv_d + EPS)
            k_ref[h, rs, :NOPE] = (ka * rk * khg[:, :NOPE]).astype(BF16)
            k_ref[h, rs, NOPE:] = (kpe_rot * rk + one_col).astype(BF16)
            vt_ref[h, :V_DIM, rs] = vt[h * V_DIM:(h + 1) * V_DIM, :].astype(BF16)
            vt_ref[h, V_DIM:, rs] = ones_rows

    for r0 in range(0, x_ref.shape[0], sub):
        part(r0)


def _proj(x1, pos, mixg, win, qg, wuqt, kvg, wuk, wuvt, qhg, khg, inv_freq, wg, wu, *, rows):
    s, d = x1.shape
    f = wg.shape[1]
    fp = _padded_hidden(f)
    w_rows = d // (s // rows)
    assert w_rows % SUBLANES_BF16 == 0
    sub = min(PROJ_SUB, rows)
    qhgt = jnp.broadcast_to(qhg.reshape(HEAD_PAD, 1), (HEAD_PAD, sub))
    invft = jnp.broadcast_to(inv_freq.reshape(-1, 1), (ROPE // 2, sub))
    row = lambda w: pl.BlockSpec((rows, w), lambda i: (i, 0))
    in_specs = [row(d), pl.BlockSpec((1, rows), lambda i: (0, i)), _resident(mixg.shape),
                _resident(win.shape), _resident(qg.shape), _resident(wuqt.shape),
                _resident(kvg.shape), _resident(wuk.shape), _resident(wuvt.shape),
                _resident(qhgt.shape), _resident(khg.shape), _resident(invft.shape),
                pl.BlockSpec((w_rows, f), lambda i: (i, 0)),
                pl.BlockSpec((w_rows, f), lambda i: (i, 0))]
    out_shape = [
        jax.ShapeDtypeStruct((s, CONV_CH), F32),
        jax.ShapeDtypeStruct((N_HEADS, HEAD_PAD, s), BF16),
        jax.ShapeDtypeStruct((N_HEADS, s, HEAD_PAD), BF16),
        jax.ShapeDtypeStruct((N_HEADS, V_PAD, s), BF16),
        jax.ShapeDtypeStruct((d, fp), BF16),
        jax.ShapeDtypeStruct((d, fp), BF16),
    ]
    out_specs = [
        row(CONV_CH),
        pl.BlockSpec((N_HEADS, HEAD_PAD, rows), lambda i: (0, 0, i)),
        pl.BlockSpec((N_HEADS, rows, HEAD_PAD), lambda i: (0, i, 0)),
        pl.BlockSpec((N_HEADS, V_PAD, rows), lambda i: (0, 0, i)),
        pl.BlockSpec((w_rows, fp), lambda i: (i, 0)),
        pl.BlockSpec((w_rows, fp), lambda i: (i, 0)),
    ]
    return pl.pallas_call(
        functools.partial(_proj_kernel, sub=sub),
        grid=(s // rows,),
        in_specs=in_specs,
        out_specs=out_specs,
        out_shape=out_shape,
        compiler_params=_params(("parallel",)),
        name="proj",
    )(x1, pos, mixg, win, qg, wuqt, kvg, wuk, wuvt, qhgt, khg, invft, wg, wu)


def _attn_kernel(qt_ref, k_ref, vt_ref, o_ref, qa_ref, p0_ref, p1_ref, r_ref, pm_ref, a_ref,
                 bad_ref, acc_ref, *, tk):
    nk = k_ref.shape[0] // tk
    tq = qt_ref.shape[1]
    p_refs = (p0_ref, p1_ref)
    first_row = lax.broadcasted_iota(jnp.int32, (SUBLANES_BF16, tq), 0) == 0

    def set_reference(r):
        qa_ref[QK_DIM:QK_DIM + SUBLANES_BF16, :] = jnp.where(first_row, -r, 0.0).astype(BF16)

    def scores(c):
        off = pl.multiple_of(c * tk, tk)
        return jnp.dot(k_ref[pl.ds(off, tk), :], qa_ref[...], preferred_element_type=F32)

    def values(c, slot, alpha):
        off = pl.multiple_of(c * tk, tk)
        pv = jnp.dot(vt_ref[:, pl.ds(off, tk)], p_refs[slot][...], preferred_element_type=F32)
        acc_ref[...] = alpha * acc_ref[...] + pv

    def emit(p, slot):
        pb = p.astype(BF16)
        p_refs[slot][...] = pb
        return jnp.max(pb, axis=0, keepdims=True).astype(F32)

    def step(c, slot):
        r_prev = r_ref[...]
        pm_prev = pm_ref[...]
        r = jnp.maximum(r_prev, r_prev + jnp.log2(pm_prev)).astype(BF16).astype(F32)
        alpha = jnp.exp2(r_prev - r)
        set_reference(r)
        pm_ref[...] = emit(jnp.exp2(scores(c)), slot)
        values(c - 1, 1 - slot, a_ref[...])
        bad_ref[...] = jnp.maximum(bad_ref[...], jnp.where(pm_prev <= 2.0 ** MAX_JUMP, 0.0, 1.0))
        r_ref[...] = r
        a_ref[...] = alpha

    qa_ref[...] = qt_ref[...]
    pm_ref[...] = emit(jnp.exp2(scores(0)), 0)
    r_ref[...] = jnp.zeros_like(r_ref)
    a_ref[...] = jnp.zeros_like(a_ref)
    bad_ref[...] = jnp.zeros_like(bad_ref)
    acc_ref[...] = jnp.zeros_like(acc_ref)

    def body(t, carry):
        c = 2 * t + 1
        step(c, 1)
        step(c + 1, 0)
        return carry

    lax.fori_loop(0, (nk - 2) // 2, body, 0)
    step(nk - 1, 1)
    values(nk - 1, 1, a_ref[...])
    bad = jnp.maximum(bad_ref[...], jnp.where(pm_ref[...] <= 2.0 ** MAX_JUMP, 0.0, 1.0))
    bad = jnp.maximum(bad, jnp.where(acc_ref[V_DIM:V_DIM + 1, :] >= 2.0 ** -MAX_JUMP, 0.0, 1.0))

    @pl.when(jnp.max(bad) > 0.0)
    def _():
        set_reference(jnp.zeros_like(bad))
        r_ref[...] = jnp.full(r_ref.shape, -jnp.inf, F32)
        acc_ref[...] = jnp.zeros_like(acc_ref)

        def exact(c, carry):
            st = scores(c)
            m_prev = r_ref[...]
            m_new = jnp.maximum(m_prev, jnp.max(st, axis=0, keepdims=True))
            p0_ref[...] = jnp.exp2(st - m_new).astype(BF16)
            values(c, 0, jnp.exp2(m_prev - m_new))
            r_ref[...] = m_new
            return carry

        lax.fori_loop(0, nk, exact, 0)

    o = acc_ref[:V_DIM, :] * (1.0 / acc_ref[V_DIM:V_DIM + 1, :])
    o_ref[...] = o.T.astype(o_ref.dtype)


def _attn(qt, k, vt, *, tq, tk):
    h, s, dp = k.shape
    assert (s // tk) % 2 == 0
    stat = pltpu.VMEM((1, tq), F32)
    return pl.pallas_call(
        functools.partial(_attn_kernel, tk=tk),
        grid=(h, s // tq),
        in_specs=[
            pl.BlockSpec((None, dp, tq), lambda hh, i: (hh, 0, i)),
            pl.BlockSpec((None, s, dp), lambda hh, i: (hh, 0, 0)),
            pl.BlockSpec((None, V_PAD, s), lambda hh, i: (hh, 0, 0)),
        ],
        out_specs=pl.BlockSpec((tq, V_DIM), lambda hh, i: (i, hh)),
        out_shape=jax.ShapeDtypeStruct((s, h * V_DIM), BF16),
        scratch_shapes=[pltpu.VMEM((dp, tq), BF16),
                        pltpu.VMEM((tk, tq), BF16), pltpu.VMEM((tk, tq), BF16),
                        stat, stat, stat, stat, pltpu.VMEM((V_PAD, tq), F32)],
        compiler_params=_params(("parallel", "arbitrary")),
        name="attn",
    )(qt, k, vt)


def _conv_kernel(prev_ref, u_ref, next_ref, w_ref, b_ref, g_ref, o_ref, sh_ref, y_ref, sq_ref, *,
                 sub):
    i = pl.program_id(0)
    rows = u_ref.shape[0]
    blocks = CONV_CH // LANES
    has_prev = i > 0
    has_next = i < pl.num_programs(0) - 1
    for c in range(blocks):
        cs = slice(c * LANES, (c + 1) * LANES)
        sh_ref[c, 0, :HALO, :] = jnp.where(has_prev, prev_ref[:, cs], 0.0)
        sh_ref[c, 0, HALO:HALO + rows, :] = u_ref[:, cs]
        sh_ref[c, 0, HALO + rows:, :] = jnp.where(has_next, next_ref[:, cs], 0.0)
    n = rows + 2 * HALO - SUBLANES
    base = HALO - CONV_PAD

    sq_ref[...] = jnp.zeros_like(sq_ref)

    def lane_block(ci, carry):
        cs = pl.ds(pl.multiple_of(ci * LANES, LANES), LANES)
        for b in range(1, SUBLANES):
            sh_ref[ci, b, :n, :] = sh_ref[ci, 0, b:b + n, :]
        for r0 in range(0, rows, sub):
            acc = jnp.broadcast_to(b_ref[:, cs], (sub, LANES))
            for t in range(CONV_WIDTH):
                a, b = divmod(base + t, SUBLANES)
                lo = r0 + a * SUBLANES
                acc = acc + sh_ref[ci, b, lo:lo + sub, :] * w_ref[t:t + 1, cs]
            y_ref[ci, r0:r0 + sub, :] = acc
            sq_ref[r0:r0 + sub, :] += acc * acc
        return carry

    lax.fori_loop(0, blocks, lane_block, 0)
    ssq = jnp.sum(sq_ref[...], axis=-1, keepdims=True)
    inv = lax.rsqrt(ssq * (1.0 / CONV_CH) + EPS)
    for c in range(blocks):
        cs = slice(c * LANES, (c + 1) * LANES)
        y = y_ref[c] * inv * g_ref[:, cs]
        o_ref[:, cs] = (y * jax.nn.sigmoid(y)).astype(o_ref.dtype)


def _conv(u, w, b, g, *, rows, sub):
    s, c = u.shape
    hb = rows // HALO
    last = s // HALO - 1
    return pl.pallas_call(
        functools.partial(_conv_kernel, sub=sub),
        grid=(s // rows,),
        in_specs=[
            pl.BlockSpec((HALO, c), lambda i: (jnp.maximum(i * hb - 1, 0), 0)),
            pl.BlockSpec((rows, c), lambda i: (i, 0)),
            pl.BlockSpec((HALO, c), lambda i: (jnp.minimum((i + 1) * hb, last), 0)),
            _resident(w.shape), _resident(b.shape), _resident(g.shape),
        ],
        out_specs=pl.BlockSpec((rows, c), lambda i: (i, 0)),
        out_shape=jax.ShapeDtypeStruct((s, c), BF16),
        scratch_shapes=[pltpu.VMEM((c // LANES, SUBLANES, rows + 2 * HALO, LANES), F32),
                        pltpu.VMEM((c // LANES, rows, LANES), F32), pltpu.VMEM((rows, LANES), F32)],
        compiler_params=_params(("parallel",)),
        name="conv",
    )(u, u, u, w, b, g)


def _outproj_kernel(x_ref, a_ref, b_ref, wa_ref, wb_ref, wd_ref, o_ref, wdo_ref):
    _cast_rows_kernel(wd_ref, wdo_ref)
    o_ref[...] = (x_ref[...]
                  + jnp.dot(a_ref[...], wa_ref[...], preferred_element_type=F32)
                  + jnp.dot(b_ref[...], wb_ref[...], preferred_element_type=F32))


def _outproj(x1, a, b, wa, wb, wd, *, rows):
    s, d = x1.shape
    steps = s // rows
    f, dw = wd.shape
    fp = _padded_hidden(f)
    strip = max(LANES, dw // steps)
    per_strip = steps // (dw // strip)
    assert steps == per_strip * (dw // strip)
    row = lambda w: pl.BlockSpec((rows, w), lambda i: (i, 0))
    return pl.pallas_call(
        _outproj_kernel,
        grid=(steps,),
        in_specs=[row(d), row(a.shape[1]), row(b.shape[1]), _resident(wa.shape), _resident(wb.shape),
                  pl.BlockSpec((f, strip), lambda i: (0, i // per_strip))],
        out_specs=[row(d), pl.BlockSpec((fp, strip), lambda i: (0, i // per_strip))],
        out_shape=[jax.ShapeDtypeStruct((s, d), F32), jax.ShapeDtypeStruct((fp, dw), BF16)],
        compiler_params=_params(("arbitrary",)),
        name="outproj",
    )(x1, a, b, wa, wb, wd)


def _pad_cols(w, n):
    return jnp.pad(w, ((0, 0), (0, n - w.shape[1])))


def _cast_cols_kernel(w_ref, o_ref):
    n = w_ref.shape[1]
    o_ref[:, :n] = w_ref[...].astype(BF16)
    o_ref[:, n:] = jnp.zeros((o_ref.shape[0], o_ref.shape[1] - n), BF16)


def _cast_pad_cols(w, n_out):
    r, n = w.shape
    rows = min(CAST_ROWS, r)
    return pl.pallas_call(
        _cast_cols_kernel,
        grid=(r // rows,),
        in_specs=[pl.BlockSpec((rows, n), lambda i: (i, 0))],
        out_specs=pl.BlockSpec((rows, n_out), lambda i: (i, 0)),
        out_shape=jax.ShapeDtypeStruct((r, n_out), BF16),
        compiler_params=_params(("parallel",)),
        name="cast_cols",
    )(w)


def _cast_rows_kernel(w_ref, o_ref):
    n = w_ref.shape[0]
    o_ref[:n, :] = w_ref[...].astype(BF16)
    o_ref[n:, :] = jnp.zeros((o_ref.shape[0] - n, o_ref.shape[1]), BF16)


def _cast_pad_rows(w, n_out):
    n, c = w.shape
    cols = min(CAST_COLS, c)
    return pl.pallas_call(
        _cast_rows_kernel,
        grid=(c // cols,),
        in_specs=[pl.BlockSpec((n, cols), lambda i: (0, i))],
        out_specs=pl.BlockSpec((n_out, cols), lambda i: (0, i)),
        out_shape=jax.ShapeDtypeStruct((n_out, c), BF16),
        compiler_params=_params(("parallel",)),
        name="cast_rows",
    )(w)


def _padded_hidden(f):
    assert f % LANES == 0
    return -(-f // FF_TILE) * FF_TILE


def _ffn_weights(wg, wu, wd):
    fp = _padded_hidden(wg.shape[1])
    return _cast_pad_cols(wg, fp), _cast_pad_cols(wu, fp), _cast_pad_rows(wd, fp)


def _head_gain(g):
    return jnp.pad(g, (0, HEAD_PAD - QK_DIM)).reshape(1, HEAD_PAD)


def _layer(x, positions, tiles, ffn1_norm, ffn1_w_gate, ffn1_w_up, ffn1_w_down, mix_norm, w_in,
           q_norm, w_uq, kv_norm, w_ukv, q_head_norm, k_head_norm, dw_kernel, dw_bias, conv_norm,
           w_out, ffn2_norm, ffn2_w_gate, ffn2_w_up, ffn2_w_down, final_norm):
    s, d = x.shape
    row = lambda g: g.reshape(1, -1)

    x1 = _ffn(x, row(ffn1_norm), *_ffn_weights(ffn1_w_gate, ffn1_w_up, ffn1_w_down), None,
              rows=tiles["ffn"])

    c2 = 2 * CONV_CH + Q_RANK + KV_RANK
    win = _pad_cols(w_in.astype(BF16), c2 + LANES)
    wuq = jnp.pad(w_uq.reshape(Q_RANK, N_HEADS, QK_DIM),
                  ((0, 0), (0, 0), (0, HEAD_PAD - QK_DIM))).reshape(Q_RANK, N_HEADS * HEAD_PAD)
    wkv = w_ukv.reshape(KV_RANK, N_HEADS, NOPE + V_DIM)
    wuk = wkv[:, :, :NOPE].reshape(KV_RANK, N_HEADS * NOPE)
    wuv = wkv[:, :, NOPE:].reshape(KV_RANK, N_HEADS * V_DIM)
    inv_freq = ROPE_THETA ** (-(jnp.arange(ROPE // 2, dtype=F32) * 2.0 / ROPE))
    u, qt, k, vt, wg2, wu2 = _proj(
        x1, positions.reshape(1, s), row(mix_norm), win, row(q_norm), wuq.T.astype(BF16),
        row(kv_norm), wuk.astype(BF16), wuv.T.astype(BF16), _head_gain(q_head_norm),
        _head_gain(k_head_norm), inv_freq, ffn2_w_gate, ffn2_w_up, rows=tiles["proj"])

    b_out = _attn(qt, k, vt, tq=tiles["attn_q"], tk=tiles["attn_k"])
    a_out = _conv(u, jnp.pad(dw_kernel, ((0, 1), (0, 0))), row(dw_bias), row(conv_norm),
                  rows=tiles["conv"], sub=tiles["conv_sub"])
    wo = w_out.astype(BF16)
    x2, wd2 = _outproj(x1, a_out, b_out, wo[:CONV_CH], wo[CONV_CH:], ffn2_w_down, rows=tiles["out"])

    return _ffn(x2, row(ffn2_norm), wg2, wu2, wd2, row(final_norm), rows=tiles["ffn"])


def _tiles(s):
    return {
        "ffn": min(FFN_ROWS, s), "proj": min(PROJ_ROWS, s), "attn_q": min(ATTN_Q, s),
        "attn_k": min(ATTN_K, s), "conv": min(CONV_ROWS, s), "conv_sub": CONV_SUB,
        "out": min(OUT_ROWS, s),
    }


def kernel(x, positions, ffn1_norm, ffn1_w_gate, ffn1_w_up, ffn1_w_down, mix_norm, w_in, q_norm, w_uq, kv_norm, w_ukv, q_head_norm, k_head_norm, dw_kernel, dw_bias, conv_norm, w_out, ffn2_norm, ffn2_w_gate, ffn2_w_up, ffn2_w_down, final_norm):
    batch, s, d = x.shape
    assert ffn1_norm.shape[0] == 1, "single-layer stack expected"
    outs = []
    for bi in range(batch):
        outs.append(_layer(
            x[bi], positions[bi], _tiles(s), ffn1_norm[0], ffn1_w_gate[0], ffn1_w_up[0],
            ffn1_w_down[0], mix_norm[0], w_in[0], q_norm[0], w_uq[0], kv_norm[0], w_ukv[0],
            q_head_norm[0], k_head_norm[0], dw_kernel[0], dw_bias[0], conv_norm[0], w_out[0],
            ffn2_norm[0], ffn2_w_gate[0], ffn2_w_up[0], ffn2_w_down[0], final_norm[0]))
    return jnp.stack(outs)
```

```python
import functools
import math

import jax
import jax.numpy as jnp
from jax import lax
from jax.experimental import pallas as pl
from jax.experimental.pallas import tpu as pltpu

F32 = jnp.float32
BF16 = jnp.bfloat16

EPS = 1e-6
ROPE_THETA = 10000.0
CONV_CH = 1024
CONV_WIDTH = 31
CONV_PAD = CONV_WIDTH // 2
N_HEADS = 8
NOPE = 128
ROPE = 64
QK_DIM = NOPE + ROPE
V_DIM = 128
Q_RANK = 512
KV_RANK = 256

LANES = 128
SUBLANES = 8
SUBLANES_BF16 = 16
HEAD_PAD = 2 * LANES
V_PAD = V_DIM + SUBLANES_BF16
VMEM_LIMIT = 56 * 1024 * 1024

FF_TILE = 512
FFN_ROWS = 512
PROJ_ROWS = 512
PROJ_SUB = 256
ATTN_Q = 2048
ATTN_K = 1024
CONV_ROWS = 256
CONV_SUB = 256
OUT_ROWS = 512
CAST_ROWS = 256
CAST_COLS = 256
HALO = 16
MAX_JUMP = 64.0


def _rms(x, g):
    ms = jnp.mean(x * x, axis=-1, keepdims=True)
    return x * lax.rsqrt(ms + EPS) * g


def _params(sem):
    return pltpu.CompilerParams(dimension_semantics=sem, vmem_limit_bytes=VMEM_LIMIT)


def _resident(shape):
    nd = len(shape)
    return pl.BlockSpec(shape, lambda *_: (0,) * nd, pipeline_mode=pl.Buffered(1))


def _ffn_kernel(*refs, final_norm):
    if final_norm:
        x_ref, g_ref, wg_ref, wu_ref, wd_ref, fg_ref, o_ref, xn_ref = refs
    else:
        x_ref, g_ref, wg_ref, wu_ref, wd_ref, o_ref, xn_ref = refs
    f = pl.program_id(1)

    @pl.when(f == 0)
    def _():
        xn_ref[...] = _rms(x_ref[...], g_ref[...]).astype(BF16)
        o_ref[...] = jnp.zeros_like(o_ref)

    xn = xn_ref[...]
    gate = jnp.dot(xn, wg_ref[...], preferred_element_type=F32)
    up = jnp.dot(xn, wu_ref[...], preferred_element_type=F32)
    h = (gate * jax.nn.sigmoid(gate) * up).astype(BF16)
    o_ref[...] += jnp.dot(h, wd_ref[...], preferred_element_type=F32)

    @pl.when(f == pl.num_programs(1) - 1)
    def _():
        y = x_ref[...] + 0.5 * o_ref[...]
        if final_norm:
            y = _rms(y, fg_ref[...])
        o_ref[...] = y


def _ffn(x, g, wg, wu, wd, final_g, *, rows):
    s, d = x.shape
    fp = wg.shape[1]
    tf = min(FF_TILE, fp)
    in_specs = [
        pl.BlockSpec((rows, d), lambda i, f: (i, 0)),
        pl.BlockSpec((1, d), lambda i, f: (0, 0)),
        pl.BlockSpec((d, tf), lambda i, f: (0, f)),
        pl.BlockSpec((d, tf), lambda i, f: (0, f)),
        pl.BlockSpec((tf, d), lambda i, f: (f, 0)),
    ]
    args = [x, g, wg, wu, wd]
    if final_g is not None:
        in_specs.append(pl.BlockSpec((1, d), lambda i, f: (0, 0)))
        args.append(final_g)
    return pl.pallas_call(
        functools.partial(_ffn_kernel, final_norm=final_g is not None),
        grid=(s // rows, fp // tf),
        in_specs=in_specs,
        out_specs=pl.BlockSpec((rows, d), lambda i, f: (i, 0)),
        out_shape=jax.ShapeDtypeStruct((s, d), F32),
        scratch_shapes=[pltpu.VMEM((rows, d), BF16)],
        compiler_params=_params(("parallel", "arbitrary")),
        name="ffn_final" if final_g is not None else "ffn",
    )(*args)


def _proj_kernel(x_ref, pos_ref, mixg_ref, win_ref, qg_ref, wuqt_ref, kvg_ref, wuk_ref, wuvt_ref,
                 qhgt_ref, khg_ref, invft_ref, wg_ref, wu_ref,
                 u_ref, qt_ref, k_ref, vt_ref, wgo_ref, wuo_ref, *, sub):
    _cast_cols_kernel(wg_ref, wgo_ref)
    _cast_cols_kernel(wu_ref, wuo_ref)

    c0 = 2 * CONV_CH
    c1 = c0 + Q_RANK
    c2 = c1 + KV_RANK
    half = ROPE // 2
    inv_d = 1.0 / QK_DIM
    scale = math.log2(math.e) / math.sqrt(QK_DIM)
    qhgt = qhgt_ref[...]
    khg = khg_ref[...]
    lane = lax.broadcasted_iota(jnp.int32, (sub, LANES), 1)
    one_col = jnp.where(lane == ROPE, 1.0, 0.0)
    ones_rows = jnp.where(
        lax.broadcasted_iota(jnp.int32, (V_PAD - V_DIM, sub), 0) == 0, 1.0, 0.0).astype(BF16)
    q_tail = jnp.zeros((HEAD_PAD - QK_DIM, sub), BF16)

    def part(r0):
        rs = slice(r0, r0 + sub)
        hn = _rms(x_ref[rs, :], mixg_ref[...]).astype(BF16)
        z = jnp.dot(hn, win_ref[...], preferred_element_type=F32)
        u_ref[rs, :] = z[:, :CONV_CH] * jax.nn.sigmoid(z[:, CONV_CH:c0])
        qnt = _rms(z[:, c0:c1], qg_ref[...]).T.astype(BF16)
        kvn = _rms(z[:, c1:c2], kvg_ref[...])
        kvnt = kvn.T.astype(BF16)
        kpe = z[:, c2:c2 + LANES]
        qt = jnp.dot(wuqt_ref[...], qnt, preferred_element_type=F32)
        vt = jnp.dot(wuvt_ref[...], kvnt, preferred_element_type=F32)
        kn = jnp.dot(kvn.astype(BF16), wuk_ref[...], preferred_element_type=F32)

        ang = invft_ref[...] * pos_ref[:, rs].astype(F32)
        cos_t = jnp.cos(ang)
        sin_t = jnp.sin(ang)
        zeros = jnp.zeros((LANES - ROPE, sub), F32)
        cos_k = jnp.concatenate([cos_t, cos_t, zeros], axis=0).T
        sin_k = jnp.concatenate([-sin_t, sin_t, zeros], axis=0).T

        kpe_g = kpe * khg[:, NOPE:]
        kpe_rot = kpe_g * cos_k + jnp.where(lane < half, pltpu.roll(kpe_g, LANES - half, 1),
                                            pltpu.roll(kpe_g, half, 1)) * sin_k
        kpe_ss = jnp.sum(kpe * kpe, axis=-1, keepdims=True)
        for h in range(N_HEADS):
            qh = qt[h * HEAD_PAD:h * HEAD_PAD + QK_DIM, :]
            r = lax.rsqrt(jnp.sum(qh * qh, axis=0, keepdims=True) * inv_d + EPS) * scale
            qh = qh * qhgt[:QK_DIM, :] * r
            x1 = qh[NOPE:NOPE + half, :]
            x2 = qh[NOPE + half:, :]
            qt_ref[h, :NOPE, rs] = qh[:NOPE, :].astype(BF16)
            qt_ref[h, NOPE:NOPE + half, rs] = (x1 * cos_t - x2 * sin_t).astype(BF16)
            qt_ref[h, NOPE + half:QK_DIM, rs] = (x2 * cos_t + x1 * sin_t).astype(BF16)
            qt_ref[h, QK_DIM:, rs] = q_tail
            ka = kn[:, h * NOPE:(h + 1) * NOPE]
            rk = lax.rsqrt((jnp.sum(ka * ka, axis=-1, keepdims=True) + kpe_ss) * inv_d + EPS)
            k_ref[h, rs, :NOPE] = (ka * rk * khg[:, :NOPE]).astype(BF16)
            k_ref[h, rs, NOPE:] = (kpe_rot * rk + one_col).astype(BF16)
            vt_ref[h, :V_DIM, rs] = vt[h * V_DIM:(h + 1) * V_DIM, :].astype(BF16)
            vt_ref[h, V_DIM:, rs] = ones_rows

    for r0 in range(0, x_ref.shape[0], sub):
        part(r0)


def _proj(x1, pos, mixg, win, qg, wuqt, kvg, wuk, wuvt, qhg, khg, inv_freq, wg, wu, *, rows):
    s, d = x1.shape
    f = wg.shape[1]
    fp = _padded_hidden(f)
    w_rows = d // (s // rows)
    assert w_rows % SUBLANES_BF16 == 0
    sub = min(PROJ_SUB, rows)
    qhgt = jnp.broadcast_to(qhg.reshape(HEAD_PAD, 1), (HEAD_PAD, sub))
    invft = jnp.broadcast_to(inv_freq.reshape(-1, 1), (ROPE // 2, sub))
    row = lambda w: pl.BlockSpec((rows, w), lambda i: (i, 0))
    in_specs = [row(d), pl.BlockSpec((1, rows), lambda i: (0, i)), _resident(mixg.shape),
                _resident(win.shape), _resident(qg.shape), _resident(wuqt.shape),
                _resident(kvg.shape), _resident(wuk.shape), _resident(wuvt.shape),
                _resident(qhgt.shape), _resident(khg.shape), _resident(invft.shape),
                pl.BlockSpec((w_rows, f), lambda i: (i, 0)),
                pl.BlockSpec((w_rows, f), lambda i: (i, 0))]
    out_shape = [
        jax.ShapeDtypeStruct((s, CONV_CH), F32),
        jax.ShapeDtypeStruct((N_HEADS, HEAD_PAD, s), BF16),
        jax.ShapeDtypeStruct((N_HEADS, s, HEAD_PAD), BF16),
        jax.ShapeDtypeStruct((N_HEADS, V_PAD, s), BF16),
        jax.ShapeDtypeStruct((d, fp), BF16),
        jax.ShapeDtypeStruct((d, fp), BF16),
    ]
    out_specs = [
        row(CONV_CH),
        pl.BlockSpec((N_HEADS, HEAD_PAD, rows), lambda i: (0, 0, i)),
        pl.BlockSpec((N_HEADS, rows, HEAD_PAD), lambda i: (0, i, 0)),
        pl.BlockSpec((N_HEADS, V_PAD, rows), lambda i: (0, 0, i)),
        pl.BlockSpec((w_rows, fp), lambda i: (i, 0)),
        pl.BlockSpec((w_rows, fp), lambda i: (i, 0)),
    ]
    return pl.pallas_call(
        functools.partial(_proj_kernel, sub=sub),
        grid=(s // rows,),
        in_specs=in_specs,
        out_specs=out_specs,
        out_shape=out_shape,
        compiler_params=_params(("parallel",)),
        name="proj",
    )(x1, pos, mixg, win, qg, wuqt, kvg, wuk, wuvt, qhgt, khg, invft, wg, wu)


def _attn_kernel(qt_ref, k_ref, vt_ref, o_ref, qa_ref, p0_ref, p1_ref, r_ref, pm_ref, a_ref,
                 bad_ref, acc_ref, *, tk):
    nk = k_ref.shape[0] // tk
    tq = qt_ref.shape[1]
    p_refs = (p0_ref, p1_ref)
    first_row = lax.broadcasted_iota(jnp.int32, (SUBLANES_BF16, tq), 0) == 0

    def set_reference(r):
        qa_ref[QK_DIM:QK_DIM + SUBLANES_BF16, :] = jnp.where(first_row, -r, 0.0).astype(BF16)

    def scores(c):
        off = pl.multiple_of(c * tk, tk)
        return jnp.dot(k_ref[pl.ds(off, tk), :], qa_ref[...], preferred_element_type=F32)

    def values(c, slot, alpha):
        off = pl.multiple_of(c * tk, tk)
        pv = jnp.dot(vt_ref[:, pl.ds(off, tk)], p_refs[slot][...], preferred_element_type=F32)
        acc_ref[...] = alpha * acc_ref[...] + pv

    def emit(p, slot):
        pb = p.astype(BF16)
        p_refs[slot][...] = pb
        return jnp.max(pb, axis=0, keepdims=True).astype(F32)

    def step(c, slot):
        r_prev = r_ref[...]
        pm_prev = pm_ref[...]
        r = jnp.maximum(r_prev, r_prev + jnp.log2(pm_prev)).astype(BF16).astype(F32)
        alpha = jnp.exp2(r_prev - r)
        set_reference(r)
        pm_ref[...] = emit(jnp.exp2(scores(c)), slot)
        values(c - 1, 1 - slot, a_ref[...])
        bad_ref[...] = jnp.maximum(bad_ref[...], jnp.where(pm_prev <= 2.0 ** MAX_JUMP, 0.0, 1.0))
        r_ref[...] = r
        a_ref[...] = alpha

    qa_ref[...] = qt_ref[...]
    pm_ref[...] = emit(jnp.exp2(scores(0)), 0)
    r_ref[...] = jnp.zeros_like(r_ref)
    a_ref[...] = jnp.zeros_like(a_ref)
    bad_ref[...] = jnp.zeros_like(bad_ref)
    acc_ref[...] = jnp.zeros_like(acc_ref)

    def body(t, carry):
        c = 2 * t + 1
        step(c, 1)
        step(c + 1, 0)
        return carry

    lax.fori_loop(0, (nk - 2) // 2, body, 0)
    step(nk - 1, 1)
    values(nk - 1, 1, a_ref[...])
    bad = jnp.maximum(bad_ref[...], jnp.where(pm_ref[...] <= 2.0 ** MAX_JUMP, 0.0, 1.0))
    bad = jnp.maximum(bad, jnp.where(acc_ref[V_DIM:V_DIM + 1, :] >= 2.0 ** -MAX_JUMP, 0.0, 1.0))

    @pl.when(jnp.max(bad) > 0.0)
    def _():
        set_reference(jnp.zeros_like(bad))
        r_ref[...] = jnp.full(r_ref.shape, -jnp.inf, F32)
        acc_ref[...] = jnp.zeros_like(acc_ref)

        def exact(c, carry):
            st = scores(c)
            m_prev = r_ref[...]
            m_new = jnp.maximum(m_prev, jnp.max(st, axis=0, keepdims=True))
            p0_ref[...] = jnp.exp2(st - m_new).astype(BF16)
            values(c, 0, jnp.exp2(m_prev - m_new))
            r_ref[...] = m_new
            return carry

        lax.fori_loop(0, nk, exact, 0)

    o = acc_ref[:V_DIM, :] * (1.0 / acc_ref[V_DIM:V_DIM + 1, :])
    o_ref[...] = o.T.astype(o_ref.dtype)


def _attn(qt, k, vt, *, tq, tk):
    h, s, dp = k.shape
    assert (s // tk) % 2 == 0
    stat = pltpu.VMEM((1, tq), F32)
    return pl.pallas_call(
        functools.partial(_attn_kernel, tk=tk),
        grid=(h, s // tq),
        in_specs=[
            pl.BlockSpec((None, dp, tq), lambda hh, i: (hh, 0, i)),
            pl.BlockSpec((None, s, dp), lambda hh, i: (hh, 0, 0)),
            pl.BlockSpec((None, V_PAD, s), lambda hh, i: (hh, 0, 0)),
        ],
        out_specs=pl.BlockSpec((tq, V_DIM), lambda hh, i: (i, hh)),
        out_shape=jax.ShapeDtypeStruct((s, h * V_DIM), BF16),
        scratch_shapes=[pltpu.VMEM((dp, tq), BF16),
                        pltpu.VMEM((tk, tq), BF16), pltpu.VMEM((tk, tq), BF16),
                        stat, stat, stat, stat, pltpu.VMEM((V_PAD, tq), F32)],
        compiler_params=_params(("parallel", "arbitrary")),
        name="attn",
    )(qt, k, vt)


def _conv_kernel(prev_ref, u_ref, next_ref, w_ref, b_ref, g_ref, o_ref, sh_ref, y_ref, sq_ref, *,
                 sub):
    i = pl.program_id(0)
    rows = u_ref.shape[0]
    blocks = CONV_CH // LANES
    has_prev = i > 0
    has_next = i < pl.num_programs(0) - 1
    for c in range(blocks):
        cs = slice(c * LANES, (c + 1) * LANES)
        sh_ref[c, 0, :HALO, :] = jnp.where(has_prev, prev_ref[:, cs], 0.0)
        sh_ref[c, 0, HALO:HALO + rows, :] = u_ref[:, cs]
        sh_ref[c, 0, HALO + rows:, :] = jnp.where(has_next, next_ref[:, cs], 0.0)
    n = rows + 2 * HALO - SUBLANES
    base = HALO - CONV_PAD

    sq_ref[...] = jnp.zeros_like(sq_ref)

    def lane_block(ci, carry):
        cs = pl.ds(pl.multiple_of(ci * LANES, LANES), LANES)
        for b in range(1, SUBLANES):
            sh_ref[ci, b, :n, :] = sh_ref[ci, 0, b:b + n, :]
        for r0 in range(0, rows, sub):
            acc = jnp.broadcast_to(b_ref[:, cs], (sub, LANES))
            for t in range(CONV_WIDTH):
                a, b = divmod(base + t, SUBLANES)
                lo = r0 + a * SUBLANES
                acc = acc + sh_ref[ci, b, lo:lo + sub, :] * w_ref[t:t + 1, cs]
            y_ref[ci, r0:r0 + sub, :] = acc
            sq_ref[r0:r0 + sub, :] += acc * acc
        return carry

    lax.fori_loop(0, blocks, lane_block, 0)
    ssq = jnp.sum(sq_ref[...], axis=-1, keepdims=True)
    inv = lax.rsqrt(ssq * (1.0 / CONV_CH) + EPS)
    for c in range(blocks):
        cs = slice(c * LANES, (c + 1) * LANES)
        y = y_ref[c] * inv * g_ref[:, cs]
        o_ref[:, cs] = (y * jax.nn.sigmoid(y)).astype(o_ref.dtype)


def _conv(u, w, b, g, *, rows, sub):
    s, c = u.shape
    hb = rows // HALO
    last = s // HALO - 1
    return pl.pallas_call(
        functools.partial(_conv_kernel, sub=sub),
        grid=(s // rows,),
        in_specs=[
            pl.BlockSpec((HALO, c), lambda i: (jnp.maximum(i * hb - 1, 0), 0)),
            pl.BlockSpec((rows, c), lambda i: (i, 0)),
            pl.BlockSpec((HALO, c), lambda i: (jnp.minimum((i + 1) * hb, last), 0)),
            _resident(w.shape), _resident(b.shape), _resident(g.shape),
        ],
        out_specs=pl.BlockSpec((rows, c), lambda i: (i, 0)),
        out_shape=jax.ShapeDtypeStruct((s, c), BF16),
        scratch_shapes=[pltpu.VMEM((c // LANES, SUBLANES, rows + 2 * HALO, LANES), F32),
                        pltpu.VMEM((c // LANES, rows, LANES), F32), pltpu.VMEM((rows, LANES), F32)],
        compiler_params=_params(("parallel",)),
        name="conv",
    )(u, u, u, w, b, g)


def _outproj_kernel(x_ref, a_ref, b_ref, wa_ref, wb_ref, wd_ref, o_ref, wdo_ref):
    _cast_rows_kernel(wd_ref, wdo_ref)
    o_ref[...] = (x_ref[...]
                  + jnp.dot(a_ref[...], wa_ref[...], preferred_element_type=F32)
                  + jnp.dot(b_ref[...], wb_ref[...], preferred_element_type=F32))


def _outproj(x1, a, b, wa, wb, wd, *, rows):
    s, d = x1.shape
    steps = s // rows
    f, dw = wd.shape
    fp = _padded_hidden(f)
    strip = max(LANES, dw // steps)
    per_strip = steps // (dw // strip)
    assert steps == per_strip * (dw // strip)
    row = lambda w: pl.BlockSpec((rows, w), lambda i: (i, 0))
    return pl.pallas_call(
        _outproj_kernel,
        grid=(steps,),
        in_specs=[row(d), row(a.shape[1]), row(b.shape[1]), _resident(wa.shape), _resident(wb.shape),
                  pl.BlockSpec((f, strip), lambda i: (0, i // per_strip))],
        out_specs=[row(d), pl.BlockSpec((fp, strip), lambda i: (0, i // per_strip))],
        out_shape=[jax.ShapeDtypeStruct((s, d), F32), jax.ShapeDtypeStruct((fp, dw), BF16)],
        compiler_params=_params(("arbitrary",)),
        name="outproj",
    )(x1, a, b, wa, wb, wd)


def _pad_cols(w, n):
    return jnp.pad(w, ((0, 0), (0, n - w.shape[1])))


def _cast_cols_kernel(w_ref, o_ref):
    n = w_ref.shape[1]
    o_ref[:, :n] = w_ref[...].astype(BF16)
    o_ref[:, n:] = jnp.zeros((o_ref.shape[0], o_ref.shape[1] - n), BF16)


def _cast_pad_cols(w, n_out):
    r, n = w.shape
    rows = min(CAST_ROWS, r)
    return pl.pallas_call(
        _cast_cols_kernel,
        grid=(r // rows,),
        in_specs=[pl.BlockSpec((rows, n), lambda i: (i, 0))],
        out_specs=pl.BlockSpec((rows, n_out), lambda i: (i, 0)),
        out_shape=jax.ShapeDtypeStruct((r, n_out), BF16),
        compiler_params=_params(("parallel",)),
        name="cast_cols",
    )(w)


def _cast_rows_kernel(w_ref, o_ref):
    n = w_ref.shape[0]
    o_ref[:n, :] = w_ref[...].astype(BF16)
    o_ref[n:, :] = jnp.zeros((o_ref.shape[0] - n, o_ref.shape[1]), BF16)


def _cast_pad_rows(w, n_out):
    n, c = w.shape
    cols = min(CAST_COLS, c)
    return pl.pallas_call(
        _cast_rows_kernel,
        grid=(c // cols,),
        in_specs=[pl.BlockSpec((n, cols), lambda i: (0, i))],
        out_specs=pl.BlockSpec((n_out, cols), lambda i: (0, i)),
        out_shape=jax.ShapeDtypeStruct((n_out, c), BF16),
        compiler_params=_params(("parallel",)),
        name="cast_rows",
    )(w)


def _padded_hidden(f):
    assert f % LANES == 0
    return -(-f // FF_TILE) * FF_TILE


def _ffn_weights(wg, wu, wd):
    fp = _padded_hidden(wg.shape[1])
    return _cast_pad_cols(wg, fp), _cast_pad_cols(wu, fp), _cast_pad_rows(wd, fp)


def _head_gain(g):
    return jnp.pad(g, (0, HEAD_PAD - QK_DIM)).reshape(1, HEAD_PAD)


def _layer(x, positions, tiles, ffn1_norm, ffn1_w_gate, ffn1_w_up, ffn1_w_down, mix_norm, w_in,
           q_norm, w_uq, kv_norm, w_ukv, q_head_norm, k_head_norm, dw_kernel, dw_bias, conv_norm,
           w_out, ffn2_norm, ffn2_w_gate, ffn2_w_up, ffn2_w_down, final_norm):
    s, d = x.shape
    row = lambda g: g.reshape(1, -1)

    x1 = _ffn(x, row(ffn1_norm), *_ffn_weights(ffn1_w_gate, ffn1_w_up, ffn1_w_down), None,
              rows=tiles["ffn"])

    c2 = 2 * CONV_CH + Q_RANK + KV_RANK
    win = _pad_cols(w_in.astype(BF16), c2 + LANES)
    wuq = jnp.pad(w_uq.astype(BF16).reshape(Q_RANK, N_HEADS, QK_DIM),
                  ((0, 0), (0, 0), (0, HEAD_PAD - QK_DIM))).reshape(Q_RANK, N_HEADS * HEAD_PAD)
    wkv = w_ukv.astype(BF16).reshape(KV_RANK, N_HEADS, NOPE + V_DIM)
    wuk = wkv[:, :, :NOPE].reshape(KV_RANK, N_HEADS * NOPE)
    wuv = wkv[:, :, NOPE:].reshape(KV_RANK, N_HEADS * V_DIM)
    inv_freq = ROPE_THETA ** (-(jnp.arange(ROPE // 2, dtype=F32) * 2.0 / ROPE))
    u, qt, k, vt, wg2, wu2 = _proj(
        x1, positions.reshape(1, s), row(mix_norm), win, row(q_norm), wuq.T, row(kv_norm), wuk,
        wuv.T, _head_gain(q_head_norm), _head_gain(k_head_norm), inv_freq, ffn2_w_gate, ffn2_w_up,
        rows=tiles["proj"])

    b_out = _attn(qt, k, vt, tq=tiles["attn_q"], tk=tiles["attn_k"])
    a_out = _conv(u, jnp.pad(dw_kernel, ((0, 1), (0, 0))), row(dw_bias), row(conv_norm),
                  rows=tiles["conv"], sub=tiles["conv_sub"])
    wo = w_out.astype(BF16)
    x2, wd2 = _outproj(x1, a_out, b_out, wo[:CONV_CH], wo[CONV_CH:], ffn2_w_down, rows=tiles["out"])

    return _ffn(x2, row(ffn2_norm), wg2, wu2, wd2, row(final_norm), rows=tiles["ffn"])


def _tiles(s):
    return {
        "ffn": min(FFN_ROWS, s), "proj": min(PROJ_ROWS, s), "attn_q": min(ATTN_Q, s),
        "attn_k": min(ATTN_K, s), "conv": min(CONV_ROWS, s), "conv_sub": CONV_SUB,
        "out": min(OUT_ROWS, s),
    }


def kernel(x, positions, ffn1_norm, ffn1_w_gate, ffn1_w_up, ffn1_w_down, mix_norm, w_in, q_norm, w_uq, kv_norm, w_ukv, q_head_norm, k_head_norm, dw_kernel, dw_bias, conv_norm, w_out, ffn2_norm, ffn2_w_gate, ffn2_w_up, ffn2_w_down, final_norm):
    batch, s, d = x.shape
    assert ffn1_norm.shape[0] == 1, "single-layer stack expected"
    outs = []
    for bi in range(batch):
        outs.append(_layer(
            x[bi], positions[bi], _tiles(s), ffn1_norm[0], ffn1_w_gate[0], ffn1_w_up[0],
            ffn1_w_down[0], mix_norm[0], w_in[0], q_norm[0], w_uq[0], kv_norm[0], w_ukv[0],
            q_head_norm[0], k_head_norm[0], dw_kernel[0], dw_bias[0], conv_norm[0], w_out[0],
            ffn2_norm[0], ffn2_w_gate[0], ffn2_w_up[0], ffn2_w_down[0], final_norm[0]))
    return jnp.stack(outs)
```

```python
import functools
import math

import jax
import jax.numpy as jnp
from jax import lax
from jax.experimental import pallas as pl
from jax.experimental.pallas import tpu as pltpu

F32 = jnp.float32
BF16 = jnp.bfloat16

EPS = 1e-6
ROPE_THETA = 10000.0
CONV_CH = 1024
CONV_WIDTH = 31
CONV_PAD = CONV_WIDTH // 2
N_HEADS = 8
NOPE = 128
ROPE = 64
QK_DIM = NOPE + ROPE
V_DIM = 128
Q_RANK = 512
KV_RANK = 256

LANES = 128
SUBLANES = 8
SUBLANES_BF16 = 16
HEAD_PAD = 2 * LANES
V_PAD = V_DIM + SUBLANES_BF16
VMEM_LIMIT = 56 * 1024 * 1024

FF_TILE = 1024
FF_UNIT = 512
FFN_ROWS = 512
PROJ_ROWS = 512
PROJ_SUB = 256
ATTN_Q = 2048
ATTN_K = 1024
CONV_ROWS = 256
CONV_SUB = 256
OUT_ROWS = 512
CAST_ROWS = 256
CAST_COLS = 256
HALO = 16
MAX_JUMP = 64.0


def _rms(x, g):
    ms = jnp.mean(x * x, axis=-1, keepdims=True)
    return x * lax.rsqrt(ms + EPS) * g


def _params(sem):
    return pltpu.CompilerParams(dimension_semantics=sem, vmem_limit_bytes=VMEM_LIMIT)


def _resident(shape):
    nd = len(shape)
    return pl.BlockSpec(shape, lambda *_: (0,) * nd, pipeline_mode=pl.Buffered(1))


def _ffn_kernel(*refs, final_norm, skip):
    if final_norm:
        x_ref, g_ref, wg_ref, wu_ref, wd_ref, fg_ref, o_ref, xn_ref = refs
    else:
        x_ref, g_ref, wg_ref, wu_ref, wd_ref, o_ref, xn_ref = refs
    f = pl.program_id(1)
    tf = wg_ref.shape[1]

    def hidden_tiles(start):
        xn = xn_ref[...]
        for c in range(start, tf, FF_UNIT):
            gate = jnp.dot(xn, wg_ref[:, c:c + FF_UNIT], preferred_element_type=F32)
            up = jnp.dot(xn, wu_ref[:, c:c + FF_UNIT], preferred_element_type=F32)
            h = (gate * jax.nn.sigmoid(gate) * up).astype(BF16)
            o_ref[...] += jnp.dot(h, wd_ref[c:c + FF_UNIT, :], preferred_element_type=F32)

    @pl.when(f == 0)
    def _():
        xn_ref[...] = _rms(x_ref[...], g_ref[...]).astype(BF16)
        o_ref[...] = jnp.zeros_like(o_ref)
        hidden_tiles(skip)

    @pl.when(f > 0)
    def _():
        hidden_tiles(0)

    @pl.when(f == pl.num_programs(1) - 1)
    def _():
        y = x_ref[...] + 0.5 * o_ref[...]
        if final_norm:
            y = _rms(y, fg_ref[...])
        o_ref[...] = y


def _ffn(x, g, wg, wu, wd, final_g, *, rows, hidden):
    s, d = x.shape
    fp = wg.shape[1]
    tf = FF_TILE
    skip = (fp - hidden) // FF_UNIT * FF_UNIT
    assert fp % tf == 0 and 0 <= skip < tf
    in_specs = [
        pl.BlockSpec((rows, d), lambda i, f: (i, 0)),
        pl.BlockSpec((1, d), lambda i, f: (0, 0)),
        pl.BlockSpec((d, tf), lambda i, f: (0, f)),
        pl.BlockSpec((d, tf), lambda i, f: (0, f)),
        pl.BlockSpec((tf, d), lambda i, f: (f, 0)),
    ]
    args = [x, g, wg, wu, wd]
    if final_g is not None:
        in_specs.append(pl.BlockSpec((1, d), lambda i, f: (0, 0)))
        args.append(final_g)
    return pl.pallas_call(
        functools.partial(_ffn_kernel, final_norm=final_g is not None, skip=skip),
        grid=(s // rows, fp // tf),
        in_specs=in_specs,
        out_specs=pl.BlockSpec((rows, d), lambda i, f: (i, 0)),
        out_shape=jax.ShapeDtypeStruct((s, d), F32),
        scratch_shapes=[pltpu.VMEM((rows, d), BF16)],
        compiler_params=_params(("parallel", "arbitrary")),
        name="ffn_final" if final_g is not None else "ffn",
    )(*args)


def _proj_kernel(x_ref, pos_ref, mixg_ref, win_ref, qg_ref, wuqt_ref, kvg_ref, wuk_ref, wuvt_ref,
                 qhgt_ref, khg_ref, invft_ref, wg_ref, wu_ref,
                 u_ref, qt_ref, k_ref, vt_ref, wgo_ref, wuo_ref, *, sub):
    _cast_cols_kernel(wg_ref, wgo_ref)
    _cast_cols_kernel(wu_ref, wuo_ref)

    c0 = 2 * CONV_CH
    c1 = c0 + Q_RANK
    c2 = c1 + KV_RANK
    half = ROPE // 2
    inv_d = 1.0 / QK_DIM
    scale = math.log2(math.e) / math.sqrt(QK_DIM)
    qhgt = qhgt_ref[...]
    khg = khg_ref[...]
    lane = lax.broadcasted_iota(jnp.int32, (sub, LANES), 1)
    one_col = jnp.where(lane == ROPE, 1.0, 0.0)
    ones_rows = jnp.where(
        lax.broadcasted_iota(jnp.int32, (V_PAD - V_DIM, sub), 0) == 0, 1.0, 0.0).astype(BF16)
    q_tail = jnp.zeros((HEAD_PAD - QK_DIM, sub), BF16)

    def part(r0):
        rs = slice(r0, r0 + sub)
        hn = _rms(x_ref[rs, :], mixg_ref[...]).astype(BF16)
        z = jnp.dot(hn, win_ref[...], preferred_element_type=F32)
        u_ref[rs, :] = z[:, :CONV_CH] * jax.nn.sigmoid(z[:, CONV_CH:c0])
        qnt = _rms(z[:, c0:c1], qg_ref[...]).T.astype(BF16)
        kvn = _rms(z[:, c1:c2], kvg_ref[...])
        kvnt = kvn.T.astype(BF16)
        kpe = z[:, c2:c2 + LANES]
        qt = jnp.dot(wuqt_ref[...], qnt, preferred_element_type=F32)
        vt = jnp.dot(wuvt_ref[...], kvnt, preferred_element_type=F32)
        kn = jnp.dot(kvn.astype(BF16), wuk_ref[...], preferred_element_type=F32)

        ang = invft_ref[...] * pos_ref[:, rs].astype(F32)
        cos_t = jnp.cos(ang)
        sin_t = jnp.sin(ang)
        zeros = jnp.zeros((LANES - ROPE, sub), F32)
        cos_k = jnp.concatenate([cos_t, cos_t, zeros], axis=0).T
        sin_k = jnp.concatenate([-sin_t, sin_t, zeros], axis=0).T

        kpe_g = kpe * khg[:, NOPE:]
        kpe_rot = kpe_g * cos_k + jnp.where(lane < half, pltpu.roll(kpe_g, LANES - half, 1),
                                            pltpu.roll(kpe_g, half, 1)) * sin_k
        kpe_ss = jnp.sum(kpe * kpe, axis=-1, keepdims=True)
        for h in range(N_HEADS):
            qh = qt[h * HEAD_PAD:h * HEAD_PAD + QK_DIM, :]
            r = lax.rsqrt(jnp.sum(qh * qh, axis=0, keepdims=True) * inv_d + EPS) * scale
            qh = qh * qhgt[:QK_DIM, :] * r
            x1 = qh[NOPE:NOPE + half, :]
            x2 = qh[NOPE + half:, :]
            qt_ref[h, :NOPE, rs] = qh[:NOPE, :].astype(BF16)
            qt_ref[h, NOPE:NOPE + half, rs] = (x1 * cos_t - x2 * sin_t).astype(BF16)
            qt_ref[h, NOPE + half:QK_DIM, rs] = (x2 * cos_t + x1 * sin_t).astype(BF16)
            qt_ref[h, QK_DIM:, rs] = q_tail
            ka = kn[:, h * NOPE:(h + 1) * NOPE]
            rk = lax.rsqrt((jnp.sum(ka * ka, axis=-1, keepdims=True) + kpe_ss) * inv_d + EPS)
            k_ref[h, rs, :NOPE] = (ka * rk * khg[:, :NOPE]).astype(BF16)
            k_ref[h, rs, NOPE:] = (kpe_rot * rk + one_col).astype(BF16)
            vt_ref[h, :V_DIM, rs] = vt[h * V_DIM:(h + 1) * V_DIM, :].astype(BF16)
            vt_ref[h, V_DIM:, rs] = ones_rows

    for r0 in range(0, x_ref.shape[0], sub):
        part(r0)


def _proj(x1, pos, mixg, win, qg, wuqt, kvg, wuk, wuvt, qhg, khg, inv_freq, wg, wu, *, rows):
    s, d = x1.shape
    f = wg.shape[1]
    fp = _padded_hidden(f)
    w_rows = d // (s // rows)
    assert w_rows % SUBLANES_BF16 == 0
    sub = min(PROJ_SUB, rows)
    qhgt = jnp.broadcast_to(qhg.reshape(HEAD_PAD, 1), (HEAD_PAD, sub))
    invft = jnp.broadcast_to(inv_freq.reshape(-1, 1), (ROPE // 2, sub))
    row = lambda w: pl.BlockSpec((rows, w), lambda i: (i, 0))
    in_specs = [row(d), pl.BlockSpec((1, rows), lambda i: (0, i)), _resident(mixg.shape),
                _resident(win.shape), _resident(qg.shape), _resident(wuqt.shape),
                _resident(kvg.shape), _resident(wuk.shape), _resident(wuvt.shape),
                _resident(qhgt.shape), _resident(khg.shape), _resident(invft.shape),
                pl.BlockSpec((w_rows, f), lambda i: (i, 0)),
                pl.BlockSpec((w_rows, f), lambda i: (i, 0))]
    out_shape = [
        jax.ShapeDtypeStruct((s, CONV_CH), F32),
        jax.ShapeDtypeStruct((N_HEADS, HEAD_PAD, s), BF16),
        jax.ShapeDtypeStruct((N_HEADS, s, HEAD_PAD), BF16),
        jax.ShapeDtypeStruct((N_HEADS, V_PAD, s), BF16),
        jax.ShapeDtypeStruct((d, fp), BF16),
        jax.ShapeDtypeStruct((d, fp), BF16),
    ]
    out_specs = [
        row(CONV_CH),
        pl.BlockSpec((N_HEADS, HEAD_PAD, rows), lambda i: (0, 0, i)),
        pl.BlockSpec((N_HEADS, rows, HEAD_PAD), lambda i: (0, i, 0)),
        pl.BlockSpec((N_HEADS, V_PAD, rows), lambda i: (0, 0, i)),
        pl.BlockSpec((w_rows, fp), lambda i: (i, 0)),
        pl.BlockSpec((w_rows, fp), lambda i: (i, 0)),
    ]
    return pl.pallas_call(
        functools.partial(_proj_kernel, sub=sub),
        grid=(s // rows,),
        in_specs=in_specs,
        out_specs=out_specs,
        out_shape=out_shape,
        compiler_params=_params(("parallel",)),
        name="proj",
    )(x1, pos, mixg, win, qg, wuqt, kvg, wuk, wuvt, qhgt, khg, invft, wg, wu)


def _attn_kernel(qt_ref, k_ref, vt_ref, o_ref, qa_ref, p0_ref, p1_ref, r_ref, pm_ref, a_ref,
                 bad_ref, acc_ref, *, tk):
    nk = k_ref.shape[0] // tk
    tq = qt_ref.shape[1]
    p_refs = (p0_ref, p1_ref)
    first_row = lax.broadcasted_iota(jnp.int32, (SUBLANES_BF16, tq), 0) == 0

    def set_reference(r):
        qa_ref[QK_DIM:QK_DIM + SUBLANES_BF16, :] = jnp.where(first_row, -r, 0.0).astype(BF16)

    def scores(c):
        off = pl.multiple_of(c * tk, tk)
        return jnp.dot(k_ref[pl.ds(off, tk), :], qa_ref[...], preferred_element_type=F32)

    def values(c, slot, alpha):
        off = pl.multiple_of(c * tk, tk)
        pv = jnp.dot(vt_ref[:, pl.ds(off, tk)], p_refs[slot][...], preferred_element_type=F32)
        acc_ref[...] = alpha * acc_ref[...] + pv

    def emit(p, slot):
        pb = p.astype(BF16)
        p_refs[slot][...] = pb
        return jnp.max(pb, axis=0, keepdims=True).astype(F32)

    def step(c, slot):
        r_prev = r_ref[...]
        pm_prev = pm_ref[...]
        r = jnp.maximum(r_prev, r_prev + jnp.log2(pm_prev)).astype(BF16).astype(F32)
        alpha = jnp.exp2(r_prev - r)
        set_reference(r)
        pm_ref[...] = emit(jnp.exp2(scores(c)), slot)
        values(c - 1, 1 - slot, a_ref[...])
        bad_ref[...] = jnp.maximum(bad_ref[...], jnp.where(pm_prev <= 2.0 ** MAX_JUMP, 0.0, 1.0))
        r_ref[...] = r
        a_ref[...] = alpha

    qa_ref[...] = qt_ref[...]
    pm_ref[...] = emit(jnp.exp2(scores(0)), 0)
    r_ref[...] = jnp.zeros_like(r_ref)
    a_ref[...] = jnp.zeros_like(a_ref)
    bad_ref[...] = jnp.zeros_like(bad_ref)
    acc_ref[...] = jnp.zeros_like(acc_ref)

    def body(t, carry):
        c = 2 * t + 1
        step(c, 1)
        step(c + 1, 0)
        return carry

    lax.fori_loop(0, (nk - 2) // 2, body, 0)
    step(nk - 1, 1)
    values(nk - 1, 1, a_ref[...])
    bad = jnp.maximum(bad_ref[...], jnp.where(pm_ref[...] <= 2.0 ** MAX_JUMP, 0.0, 1.0))
    bad = jnp.maximum(bad, jnp.where(acc_ref[V_DIM:V_DIM + 1, :] >= 2.0 ** -MAX_JUMP, 0.0, 1.0))

    @pl.when(jnp.max(bad) > 0.0)
    def _():
        set_reference(jnp.zeros_like(bad))
        r_ref[...] = jnp.full(r_ref.shape, -jnp.inf, F32)
        acc_ref[...] = jnp.zeros_like(acc_ref)

        def exact(c, carry):
            st = scores(c)
            m_prev = r_ref[...]
            m_new = jnp.maximum(m_prev, jnp.max(st, axis=0, keepdims=True))
            p0_ref[...] = jnp.exp2(st - m_new).astype(BF16)
            values(c, 0, jnp.exp2(m_prev - m_new))
            r_ref[...] = m_new
            return carry

        lax.fori_loop(0, nk, exact, 0)

    o = acc_ref[:V_DIM, :] * (1.0 / acc_ref[V_DIM:V_DIM + 1, :])
    o_ref[...] = o.T.astype(o_ref.dtype)


def _attn(qt, k, vt, *, tq, tk):
    h, s, dp = k.shape
    assert (s // tk) % 2 == 0
    stat = pltpu.VMEM((1, tq), F32)
    return pl.pallas_call(
        functools.partial(_attn_kernel, tk=tk),
        grid=(h, s // tq),
        in_specs=[
            pl.BlockSpec((None, dp, tq), lambda hh, i: (hh, 0, i)),
            pl.BlockSpec((None, s, dp), lambda hh, i: (hh, 0, 0)),
            pl.BlockSpec((None, V_PAD, s), lambda hh, i: (hh, 0, 0)),
        ],
        out_specs=pl.BlockSpec((tq, V_DIM), lambda hh, i: (i, hh)),
        out_shape=jax.ShapeDtypeStruct((s, h * V_DIM), BF16),
        scratch_shapes=[pltpu.VMEM((dp, tq), BF16),
                        pltpu.VMEM((tk, tq), BF16), pltpu.VMEM((tk, tq), BF16),
                        stat, stat, stat, stat, pltpu.VMEM((V_PAD, tq), F32)],
        compiler_params=_params(("parallel", "arbitrary")),
        name="attn",
    )(qt, k, vt)


def _conv_kernel(prev_ref, u_ref, next_ref, w_ref, b_ref, g_ref, o_ref, sh_ref, y_ref, sq_ref, *,
                 sub):
    i = pl.program_id(0)
    rows = u_ref.shape[0]
    blocks = CONV_CH // LANES
    has_prev = i > 0
    has_next = i < pl.num_programs(0) - 1
    for c in range(blocks):
        cs = slice(c * LANES, (c + 1) * LANES)
        sh_ref[c, 0, :HALO, :] = jnp.where(has_prev, prev_ref[:, cs], 0.0)
        sh_ref[c, 0, HALO:HALO + rows, :] = u_ref[:, cs]
        sh_ref[c, 0, HALO + rows:, :] = jnp.where(has_next, next_ref[:, cs], 0.0)
    n = rows + 2 * HALO - SUBLANES
    base = HALO - CONV_PAD

    sq_ref[...] = jnp.zeros_like(sq_ref)

    def lane_block(ci, carry):
        cs = pl.ds(pl.multiple_of(ci * LANES, LANES), LANES)
        for b in range(1, SUBLANES):
            sh_ref[ci, b, :n, :] = sh_ref[ci, 0, b:b + n, :]
        for r0 in range(0, rows, sub):
            acc = jnp.broadcast_to(b_ref[:, cs], (sub, LANES))
            for t in range(CONV_WIDTH):
                a, b = divmod(base + t, SUBLANES)
                lo = r0 + a * SUBLANES
                acc = acc + sh_ref[ci, b, lo:lo + sub, :] * w_ref[t:t + 1, cs]
            y_ref[ci, r0:r0 + sub, :] = acc
            sq_ref[r0:r0 + sub, :] += acc * acc
        return carry

    lax.fori_loop(0, blocks, lane_block, 0)
    ssq = jnp.sum(sq_ref[...], axis=-1, keepdims=True)
    inv = lax.rsqrt(ssq * (1.0 / CONV_CH) + EPS)
    for c in range(blocks):
        cs = slice(c * LANES, (c + 1) * LANES)
        y = y_ref[c] * inv * g_ref[:, cs]
        o_ref[:, cs] = (y * jax.nn.sigmoid(y)).astype(o_ref.dtype)


def _conv(u, w, b, g, *, rows, sub):
    s, c = u.shape
    hb = rows // HALO
    last = s // HALO - 1
    return pl.pallas_call(
        functools.partial(_conv_kernel, sub=sub),
        grid=(s // rows,),
        in_specs=[
            pl.BlockSpec((HALO, c), lambda i: (jnp.maximum(i * hb - 1, 0), 0)),
            pl.BlockSpec((rows, c), lambda i: (i, 0)),
            pl.BlockSpec((HALO, c), lambda i: (jnp.minimum((i + 1) * hb, last), 0)),
            _resident(w.shape), _resident(b.shape), _resident(g.shape),
        ],
        out_specs=pl.BlockSpec((rows, c), lambda i: (i, 0)),
        out_shape=jax.ShapeDtypeStruct((s, c), BF16),
        scratch_shapes=[pltpu.VMEM((c // LANES, SUBLANES, rows + 2 * HALO, LANES), F32),
                        pltpu.VMEM((c // LANES, rows, LANES), F32), pltpu.VMEM((rows, LANES), F32)],
        compiler_params=_params(("parallel",)),
        name="conv",
    )(u, u, u, w, b, g)


def _outproj_kernel(x_ref, a_ref, b_ref, wa_ref, wb_ref, wd_ref, o_ref, wdo_ref):
    _cast_rows_kernel(wd_ref, wdo_ref)
    o_ref[...] = (x_ref[...]
                  + jnp.dot(a_ref[...], wa_ref[...], preferred_element_type=F32)
                  + jnp.dot(b_ref[...], wb_ref[...], preferred_element_type=F32))


def _outproj(x1, a, b, wa, wb, wd, *, rows):
    s, d = x1.shape
    steps = s // rows
    f, dw = wd.shape
    fp = _padded_hidden(f)
    strip = max(LANES, dw // steps)
    per_strip = steps // (dw // strip)
    assert steps == per_strip * (dw // strip)
    row = lambda w: pl.BlockSpec((rows, w), lambda i: (i, 0))
    return pl.pallas_call(
        _outproj_kernel,
        grid=(steps,),
        in_specs=[row(d), row(a.shape[1]), row(b.shape[1]), _resident(wa.shape), _resident(wb.shape),
                  pl.BlockSpec((f, strip), lambda i: (0, i // per_strip))],
        out_specs=[row(d), pl.BlockSpec((fp, strip), lambda i: (0, i // per_strip))],
        out_shape=[jax.ShapeDtypeStruct((s, d), F32), jax.ShapeDtypeStruct((fp, dw), BF16)],
        compiler_params=_params(("arbitrary",)),
        name="outproj",
    )(x1, a, b, wa, wb, wd)


def _pad_cols(w, n):
    return jnp.pad(w, ((0, 0), (0, n - w.shape[1])))


def _cast_cols_kernel(w_ref, o_ref):
    pad = o_ref.shape[1] - w_ref.shape[1]
    o_ref[:, :pad] = jnp.zeros((o_ref.shape[0], pad), BF16)
    o_ref[:, pad:] = w_ref[...].astype(BF16)


def _cast_pad_cols(w, n_out):
    r, n = w.shape
    rows = min(CAST_ROWS, r)
    return pl.pallas_call(
        _cast_cols_kernel,
        grid=(r // rows,),
        in_specs=[pl.BlockSpec((rows, n), lambda i: (i, 0))],
        out_specs=pl.BlockSpec((rows, n_out), lambda i: (i, 0)),
        out_shape=jax.ShapeDtypeStruct((r, n_out), BF16),
        compiler_params=_params(("parallel",)),
        name="cast_cols",
    )(w)


def _cast_rows_kernel(w_ref, o_ref):
    pad = o_ref.shape[0] - w_ref.shape[0]
    o_ref[:pad, :] = jnp.zeros((pad, o_ref.shape[1]), BF16)
    o_ref[pad:, :] = w_ref[...].astype(BF16)


def _cast_pad_rows(w, n_out):
    n, c = w.shape
    cols = min(CAST_COLS, c)
    return pl.pallas_call(
        _cast_rows_kernel,
        grid=(c // cols,),
        in_specs=[pl.BlockSpec((n, cols), lambda i: (0, i))],
        out_specs=pl.BlockSpec((n_out, cols), lambda i: (0, i)),
        out_shape=jax.ShapeDtypeStruct((n_out, c), BF16),
        compiler_params=_params(("parallel",)),
        name="cast_rows",
    )(w)


def _padded_hidden(f):
    assert f % LANES == 0
    return -(-f // FF_TILE) * FF_TILE


def _ffn_weights(wg, wu, wd):
    fp = _padded_hidden(wg.shape[1])
    return _cast_pad_cols(wg, fp), _cast_pad_cols(wu, fp), _cast_pad_rows(wd, fp)


def _head_gain(g):
    return jnp.pad(g, (0, HEAD_PAD - QK_DIM)).reshape(1, HEAD_PAD)


def _layer(x, positions, tiles, ffn1_norm, ffn1_w_gate, ffn1_w_up, ffn1_w_down, mix_norm, w_in,
           q_norm, w_uq, kv_norm, w_ukv, q_head_norm, k_head_norm, dw_kernel, dw_bias, conv_norm,
           w_out, ffn2_norm, ffn2_w_gate, ffn2_w_up, ffn2_w_down, final_norm):
    s, d = x.shape
    row = lambda g: g.reshape(1, -1)

    hidden = ffn1_w_gate.shape[1]
    x1 = _ffn(x, row(ffn1_norm), *_ffn_weights(ffn1_w_gate, ffn1_w_up, ffn1_w_down), None,
              rows=tiles["ffn"], hidden=hidden)

    c2 = 2 * CONV_CH + Q_RANK + KV_RANK
    win = _pad_cols(w_in.astype(BF16), c2 + LANES)
    wuq = jnp.pad(w_uq.astype(BF16).reshape(Q_RANK, N_HEADS, QK_DIM),
                  ((0, 0), (0, 0), (0, HEAD_PAD - QK_DIM))).reshape(Q_RANK, N_HEADS * HEAD_PAD)
    wkv = w_ukv.astype(BF16).reshape(KV_RANK, N_HEADS, NOPE + V_DIM)
    wuk = wkv[:, :, :NOPE].reshape(KV_RANK, N_HEADS * NOPE)
    wuv = wkv[:, :, NOPE:].reshape(KV_RANK, N_HEADS * V_DIM)
    inv_freq = ROPE_THETA ** (-(jnp.arange(ROPE // 2, dtype=F32) * 2.0 / ROPE))
    u, qt, k, vt, wg2, wu2 = _proj(
        x1, positions.reshape(1, s), row(mix_norm), win, row(q_norm), wuq.T, row(kv_norm), wuk,
        wuv.T, _head_gain(q_head_norm), _head_gain(k_head_norm), inv_freq, ffn2_w_gate, ffn2_w_up,
        rows=tiles["proj"])

    b_out = _attn(qt, k, vt, tq=tiles["attn_q"], tk=tiles["attn_k"])
    a_out = _conv(u, jnp.pad(dw_kernel, ((0, 1), (0, 0))), row(dw_bias), row(conv_norm),
                  rows=tiles["conv"], sub=tiles["conv_sub"])
    wo = w_out.astype(BF16)
    x2, wd2 = _outproj(x1, a_out, b_out, wo[:CONV_CH], wo[CONV_CH:], ffn2_w_down, rows=tiles["out"])

    return _ffn(x2, row(ffn2_norm), wg2, wu2, wd2, row(final_norm), rows=tiles["ffn"],
                hidden=hidden)


def _tiles(s):
    return {
        "ffn": min(FFN_ROWS, s), "proj": min(PROJ_ROWS, s), "attn_q": min(ATTN_Q, s),
        "attn_k": min(ATTN_K, s), "conv": min(CONV_ROWS, s), "conv_sub": CONV_SUB,
        "out": min(OUT_ROWS, s),
    }


def kernel(x, positions, ffn1_norm, ffn1_w_gate, ffn1_w_up, ffn1_w_down, mix_norm, w_in, q_norm, w_uq, kv_norm, w_ukv, q_head_norm, k_head_norm, dw_kernel, dw_bias, conv_norm, w_out, ffn2_norm, ffn2_w_gate, ffn2_w_up, ffn2_w_down, final_norm):
    batch, s, d = x.shape
    assert ffn1_norm.shape[0] == 1, "single-layer stack expected"
    outs = []
    for bi in range(batch):
        outs.append(_layer(
            x[bi], positions[bi], _tiles(s), ffn1_norm[0], ffn1_w_gate[0], ffn1_w_up[0],
            ffn1_w_down[0], mix_norm[0], w_in[0], q_norm[0], w_uq[0], kv_norm[0], w_ukv[0],
            q_head_norm[0], k_head_norm[0], dw_kernel[0], dw_bias[0], conv_norm[0], w_out[0],
            ffn2_norm[0], ffn2_w_gate[0], ffn2_w_up[0], ffn2_w_down[0], final_norm[0]))
    return jnp.stack(outs)
```

```python
import functools
import math

import jax
import jax.numpy as jnp
from jax import lax
from jax.experimental import pallas as pl
from jax.experimental.pallas import tpu as pltpu

F32 = jnp.float32
BF16 = jnp.bfloat16

EPS = 1e-6
ROPE_THETA = 10000.0
CONV_CH = 1024
CONV_WIDTH = 31
CONV_PAD = CONV_WIDTH // 2
N_HEADS = 8
NOPE = 128
ROPE = 64
QK_DIM = NOPE + ROPE
V_DIM = 128
Q_RANK = 512
KV_RANK = 256

LANES = 128
SUBLANES = 8
SUBLANES_BF16 = 16
HEAD_PAD = 2 * LANES
V_PAD = V_DIM + SUBLANES_BF16
VMEM_LIMIT = 56 * 1024 * 1024

FF_TILE = 1024
FF_UNIT = 512
FFN_ROWS = 512
PROJ_ROWS = 512
PROJ_SUB = 256
ATTN_Q = 2048
ATTN_K = 1024
CONV_ROWS = 256
CONV_SUB = 256
OUT_ROWS = 512
CAST_ROWS = 256
CAST_COLS = 256
HALO = 16
MAX_JUMP = 64.0


def _rms(x, g):
    ms = jnp.mean(x * x, axis=-1, keepdims=True)
    return x * lax.rsqrt(ms + EPS) * g


def _params(sem):
    return pltpu.CompilerParams(dimension_semantics=sem, vmem_limit_bytes=VMEM_LIMIT)


def _resident(shape):
    nd = len(shape)
    return pl.BlockSpec(shape, lambda *_: (0,) * nd, pipeline_mode=pl.Buffered(1))


def _ffn_kernel(*refs, final_norm, skip):
    if final_norm:
        x_ref, g_ref, wg_ref, wu_ref, wd_ref, fg_ref, o_ref, xn_ref = refs
    else:
        x_ref, g_ref, wg_ref, wu_ref, wd_ref, o_ref, xn_ref = refs
    f = pl.program_id(1)
    tf = wg_ref.shape[1]

    def hidden_tiles(start):
        xn = xn_ref[...]
        for c in range(start, tf, FF_UNIT):
            gate = jnp.dot(xn, wg_ref[:, c:c + FF_UNIT], preferred_element_type=F32)
            up = jnp.dot(xn, wu_ref[:, c:c + FF_UNIT], preferred_element_type=F32)
            h = (gate * jax.nn.sigmoid(gate) * up).astype(BF16)
            o_ref[...] += jnp.dot(h, wd_ref[c:c + FF_UNIT, :], preferred_element_type=F32)

    @pl.when(f == 0)
    def _():
        xn_ref[...] = _rms(x_ref[...], g_ref[...]).astype(BF16)
        o_ref[...] = jnp.zeros_like(o_ref)
        hidden_tiles(skip)

    @pl.when(f > 0)
    def _():
        hidden_tiles(0)

    @pl.when(f == pl.num_programs(1) - 1)
    def _():
        y = x_ref[...] + 0.5 * o_ref[...]
        if final_norm:
            y = _rms(y, fg_ref[...])
        o_ref[...] = y


def _ffn(x, g, wg, wu, wd, final_g, *, rows, hidden):
    s, d = x.shape
    fp = wd.shape[0]
    tf = FF_TILE
    skip = (fp - hidden) // FF_UNIT * FF_UNIT
    assert fp % tf == 0 and 0 <= skip < tf
    in_specs = [
        pl.BlockSpec((rows, d), lambda i, f: (i, 0)),
        pl.BlockSpec((1, d), lambda i, f: (0, 0)),
        pl.BlockSpec((None, d, tf), lambda i, f: (f, 0, 0)),
        pl.BlockSpec((None, d, tf), lambda i, f: (f, 0, 0)),
        pl.BlockSpec((tf, d), lambda i, f: (f, 0)),
    ]
    args = [x, g, wg, wu, wd]
    if final_g is not None:
        in_specs.append(pl.BlockSpec((1, d), lambda i, f: (0, 0)))
        args.append(final_g)
    return pl.pallas_call(
        functools.partial(_ffn_kernel, final_norm=final_g is not None, skip=skip),
        grid=(s // rows, fp // tf),
        in_specs=in_specs,
        out_specs=pl.BlockSpec((rows, d), lambda i, f: (i, 0)),
        out_shape=jax.ShapeDtypeStruct((s, d), F32),
        scratch_shapes=[pltpu.VMEM((rows, d), BF16)],
        compiler_params=_params(("parallel", "arbitrary")),
        name="ffn_final" if final_g is not None else "ffn",
    )(*args)


def _proj_kernel(x_ref, pos_ref, mixg_ref, win_ref, qg_ref, wuqt_ref, kvg_ref, wuk_ref, wuvt_ref,
                 qhgt_ref, khg_ref, invft_ref, wg_ref, wu_ref,
                 u_ref, qt_ref, k_ref, vt_ref, wgo_ref, wuo_ref, *, sub):
    _cast_cols_kernel(wg_ref, wgo_ref)
    _cast_cols_kernel(wu_ref, wuo_ref)

    c0 = 2 * CONV_CH
    c1 = c0 + Q_RANK
    c2 = c1 + KV_RANK
    half = ROPE // 2
    inv_d = 1.0 / QK_DIM
    scale = math.log2(math.e) / math.sqrt(QK_DIM)
    qhgt = qhgt_ref[...]
    khg = khg_ref[...]
    lane = lax.broadcasted_iota(jnp.int32, (sub, LANES), 1)
    one_col = jnp.where(lane == ROPE, 1.0, 0.0)
    ones_rows = jnp.where(
        lax.broadcasted_iota(jnp.int32, (V_PAD - V_DIM, sub), 0) == 0, 1.0, 0.0).astype(BF16)
    q_tail = jnp.zeros((HEAD_PAD - QK_DIM, sub), BF16)

    def part(r0):
        rs = slice(r0, r0 + sub)
        hn = _rms(x_ref[rs, :], mixg_ref[...]).astype(BF16)
        z = jnp.dot(hn, win_ref[...], preferred_element_type=F32)
        u_ref[rs, :] = z[:, :CONV_CH] * jax.nn.sigmoid(z[:, CONV_CH:c0])
        qnt = _rms(z[:, c0:c1], qg_ref[...]).T.astype(BF16)
        kvn = _rms(z[:, c1:c2], kvg_ref[...])
        kvnt = kvn.T.astype(BF16)
        kpe = z[:, c2:c2 + LANES]
        qt = jnp.dot(wuqt_ref[...], qnt, preferred_element_type=F32)
        vt = jnp.dot(wuvt_ref[...], kvnt, preferred_element_type=F32)
        kn = jnp.dot(kvn.astype(BF16), wuk_ref[...], preferred_element_type=F32)

        ang = invft_ref[...] * pos_ref[:, rs].astype(F32)
        cos_t = jnp.cos(ang)
        sin_t = jnp.sin(ang)
        zeros = jnp.zeros((LANES - ROPE, sub), F32)
        cos_k = jnp.concatenate([cos_t, cos_t, zeros], axis=0).T
        sin_k = jnp.concatenate([-sin_t, sin_t, zeros], axis=0).T

        kpe_g = kpe * khg[:, NOPE:]
        kpe_rot = kpe_g * cos_k + jnp.where(lane < half, pltpu.roll(kpe_g, LANES - half, 1),
                                            pltpu.roll(kpe_g, half, 1)) * sin_k
        kpe_ss = jnp.sum(kpe * kpe, axis=-1, keepdims=True)
        for h in range(N_HEADS):
            qh = qt[h * HEAD_PAD:h * HEAD_PAD + QK_DIM, :]
            r = lax.rsqrt(jnp.sum(qh * qh, axis=0, keepdims=True) * inv_d + EPS) * scale
            qh = qh * qhgt[:QK_DIM, :] * r
            x1 = qh[NOPE:NOPE + half, :]
            x2 = qh[NOPE + half:, :]
            qt_ref[h, :NOPE, rs] = qh[:NOPE, :].astype(BF16)
            qt_ref[h, NOPE:NOPE + half, rs] = (x1 * cos_t - x2 * sin_t).astype(BF16)
            qt_ref[h, NOPE + half:QK_DIM, rs] = (x2 * cos_t + x1 * sin_t).astype(BF16)
            qt_ref[h, QK_DIM:, rs] = q_tail
            ka = kn[:, h * NOPE:(h + 1) * NOPE]
            rk = lax.rsqrt((jnp.sum(ka * ka, axis=-1, keepdims=True) + kpe_ss) * inv_d + EPS)
            k_ref[h, rs, :NOPE] = (ka * rk * khg[:, :NOPE]).astype(BF16)
            k_ref[h, rs, NOPE:] = (kpe_rot * rk + one_col).astype(BF16)
            vt_ref[h, :V_DIM, rs] = vt[h * V_DIM:(h + 1) * V_DIM, :].astype(BF16)
            vt_ref[h, V_DIM:, rs] = ones_rows

    for r0 in range(0, x_ref.shape[0], sub):
        part(r0)


def _proj(x1, pos, mixg, win, qg, wuqt, kvg, wuk, wuvt, qhg, khg, inv_freq, wg, wu, *, rows):
    s, d = x1.shape
    f = wg.shape[1]
    fp = _padded_hidden(f)
    w_rows = d // (s // rows)
    assert w_rows % SUBLANES_BF16 == 0
    sub = min(PROJ_SUB, rows)
    qhgt = jnp.broadcast_to(qhg.reshape(HEAD_PAD, 1), (HEAD_PAD, sub))
    invft = jnp.broadcast_to(inv_freq.reshape(-1, 1), (ROPE // 2, sub))
    row = lambda w: pl.BlockSpec((rows, w), lambda i: (i, 0))
    in_specs = [row(d), pl.BlockSpec((1, rows), lambda i: (0, i)), _resident(mixg.shape),
                _resident(win.shape), _resident(qg.shape), _resident(wuqt.shape),
                _resident(kvg.shape), _resident(wuk.shape), _resident(wuvt.shape),
                _resident(qhgt.shape), _resident(khg.shape), _resident(invft.shape),
                pl.BlockSpec((w_rows, f), lambda i: (i, 0)),
                pl.BlockSpec((w_rows, f), lambda i: (i, 0))]
    out_shape = [
        jax.ShapeDtypeStruct((s, CONV_CH), F32),
        jax.ShapeDtypeStruct((N_HEADS, HEAD_PAD, s), BF16),
        jax.ShapeDtypeStruct((N_HEADS, s, HEAD_PAD), BF16),
        jax.ShapeDtypeStruct((N_HEADS, V_PAD, s), BF16),
        jax.ShapeDtypeStruct((fp // FF_TILE, d, FF_TILE), BF16),
        jax.ShapeDtypeStruct((fp // FF_TILE, d, FF_TILE), BF16),
    ]
    out_specs = [
        row(CONV_CH),
        pl.BlockSpec((N_HEADS, HEAD_PAD, rows), lambda i: (0, 0, i)),
        pl.BlockSpec((N_HEADS, rows, HEAD_PAD), lambda i: (0, i, 0)),
        pl.BlockSpec((N_HEADS, V_PAD, rows), lambda i: (0, 0, i)),
        pl.BlockSpec((fp // FF_TILE, w_rows, FF_TILE), lambda i: (0, i, 0)),
        pl.BlockSpec((fp // FF_TILE, w_rows, FF_TILE), lambda i: (0, i, 0)),
    ]
    return pl.pallas_call(
        functools.partial(_proj_kernel, sub=sub),
        grid=(s // rows,),
        in_specs=in_specs,
        out_specs=out_specs,
        out_shape=out_shape,
        compiler_params=_params(("parallel",)),
        name="proj",
    )(x1, pos, mixg, win, qg, wuqt, kvg, wuk, wuvt, qhgt, khg, invft, wg, wu)


def _attn_kernel(qt_ref, k_ref, vt_ref, o_ref, qa_ref, p0_ref, p1_ref, r_ref, pm_ref, a_ref,
                 bad_ref, acc_ref, *, tk):
    nk = k_ref.shape[0] // tk
    tq = qt_ref.shape[1]
    p_refs = (p0_ref, p1_ref)
    first_row = lax.broadcasted_iota(jnp.int32, (SUBLANES_BF16, tq), 0) == 0

    def set_reference(r):
        qa_ref[QK_DIM:QK_DIM + SUBLANES_BF16, :] = jnp.where(first_row, -r, 0.0).astype(BF16)

    def scores(c):
        off = pl.multiple_of(c * tk, tk)
        return jnp.dot(k_ref[pl.ds(off, tk), :], qa_ref[...], preferred_element_type=F32)

    def values(c, slot, alpha):
        off = pl.multiple_of(c * tk, tk)
        pv = jnp.dot(vt_ref[:, pl.ds(off, tk)], p_refs[slot][...], preferred_element_type=F32)
        acc_ref[...] = alpha * acc_ref[...] + pv

    def emit(p, slot):
        pb = p.astype(BF16)
        p_refs[slot][...] = pb
        return jnp.max(pb, axis=0, keepdims=True).astype(F32)

    def step(c, slot):
        r_prev = r_ref[...]
        pm_prev = pm_ref[...]
        r = jnp.maximum(r_prev, r_prev + jnp.log2(pm_prev)).astype(BF16).astype(F32)
        alpha = jnp.exp2(r_prev - r)
        set_reference(r)
        pm_ref[...] = emit(jnp.exp2(scores(c)), slot)
        values(c - 1, 1 - slot, a_ref[...])
        bad_ref[...] = jnp.maximum(bad_ref[...], jnp.where(pm_prev <= 2.0 ** MAX_JUMP, 0.0, 1.0))
        r_ref[...] = r
        a_ref[...] = alpha

    qa_ref[...] = qt_ref[...]
    pm_ref[...] = emit(jnp.exp2(scores(0)), 0)
    r_ref[...] = jnp.zeros_like(r_ref)
    a_ref[...] = jnp.zeros_like(a_ref)
    bad_ref[...] = jnp.zeros_like(bad_ref)
    acc_ref[...] = jnp.zeros_like(acc_ref)

    def body(t, carry):
        c = 2 * t + 1
        step(c, 1)
        step(c + 1, 0)
        return carry

    lax.fori_loop(0, (nk - 2) // 2, body, 0)
    step(nk - 1, 1)
    values(nk - 1, 1, a_ref[...])
    bad = jnp.maximum(bad_ref[...], jnp.where(pm_ref[...] <= 2.0 ** MAX_JUMP, 0.0, 1.0))
    bad = jnp.maximum(bad, jnp.where(acc_ref[V_DIM:V_DIM + 1, :] >= 2.0 ** -MAX_JUMP, 0.0, 1.0))

    @pl.when(jnp.max(bad) > 0.0)
    def _():
        set_reference(jnp.zeros_like(bad))
        r_ref[...] = jnp.full(r_ref.shape, -jnp.inf, F32)
        acc_ref[...] = jnp.zeros_like(acc_ref)

        def exact(c, carry):
            st = scores(c)
            m_prev = r_ref[...]
            m_new = jnp.maximum(m_prev, jnp.max(st, axis=0, keepdims=True))
            p0_ref[...] = jnp.exp2(st - m_new).astype(BF16)
            values(c, 0, jnp.exp2(m_prev - m_new))
            r_ref[...] = m_new
            return carry

        lax.fori_loop(0, nk, exact, 0)

    o = acc_ref[:V_DIM, :] * (1.0 / acc_ref[V_DIM:V_DIM + 1, :])
    o_ref[...] = o.T.astype(o_ref.dtype)


def _attn(qt, k, vt, *, tq, tk):
    h, s, dp = k.shape
    assert (s // tk) % 2 == 0
    stat = pltpu.VMEM((1, tq), F32)
    return pl.pallas_call(
        functools.partial(_attn_kernel, tk=tk),
        grid=(h, s // tq),
        in_specs=[
            pl.BlockSpec((None, dp, tq), lambda hh, i: (hh, 0, i)),
            pl.BlockSpec((None, s, dp), lambda hh, i: (hh, 0, 0)),
            pl.BlockSpec((None, V_PAD, s), lambda hh, i: (hh, 0, 0)),
        ],
        out_specs=pl.BlockSpec((tq, V_DIM), lambda hh, i: (i, hh)),
        out_shape=jax.ShapeDtypeStruct((s, h * V_DIM), BF16),
        scratch_shapes=[pltpu.VMEM((dp, tq), BF16),
                        pltpu.VMEM((tk, tq), BF16), pltpu.VMEM((tk, tq), BF16),
                        stat, stat, stat, stat, pltpu.VMEM((V_PAD, tq), F32)],
        compiler_params=_params(("parallel", "arbitrary")),
        name="attn",
    )(qt, k, vt)


def _conv_kernel(prev_ref, u_ref, next_ref, w_ref, b_ref, g_ref, o_ref, sh_ref, y_ref, sq_ref, *,
                 sub):
    i = pl.program_id(0)
    rows = u_ref.shape[0]
    blocks = CONV_CH // LANES
    has_prev = i > 0
    has_next = i < pl.num_programs(0) - 1
    for c in range(blocks):
        cs = slice(c * LANES, (c + 1) * LANES)
        sh_ref[c, 0, :HALO, :] = jnp.where(has_prev, prev_ref[:, cs], 0.0)
        sh_ref[c, 0, HALO:HALO + rows, :] = u_ref[:, cs]
        sh_ref[c, 0, HALO + rows:, :] = jnp.where(has_next, next_ref[:, cs], 0.0)
    n = rows + 2 * HALO - SUBLANES
    base = HALO - CONV_PAD

    sq_ref[...] = jnp.zeros_like(sq_ref)

    def lane_block(ci, carry):
        cs = pl.ds(pl.multiple_of(ci * LANES, LANES), LANES)
        for b in range(1, SUBLANES):
            sh_ref[ci, b, :n, :] = sh_ref[ci, 0, b:b + n, :]
        for r0 in range(0, rows, sub):
            acc = jnp.broadcast_to(b_ref[:, cs], (sub, LANES))
            for t in range(CONV_WIDTH):
                a, b = divmod(base + t, SUBLANES)
                lo = r0 + a * SUBLANES
                acc = acc + sh_ref[ci, b, lo:lo + sub, :] * w_ref[t:t + 1, cs]
            y_ref[ci, r0:r0 + sub, :] = acc
            sq_ref[r0:r0 + sub, :] += acc * acc
        return carry

    lax.fori_loop(0, blocks, lane_block, 0)
    ssq = jnp.sum(sq_ref[...], axis=-1, keepdims=True)
    inv = lax.rsqrt(ssq * (1.0 / CONV_CH) + EPS)
    for c in range(blocks):
        cs = slice(c * LANES, (c + 1) * LANES)
        y = y_ref[c] * inv * g_ref[:, cs]
        o_ref[:, cs] = (y * jax.nn.sigmoid(y)).astype(o_ref.dtype)


def _conv(u, w, b, g, *, rows, sub):
    s, c = u.shape
    hb = rows // HALO
    last = s // HALO - 1
    return pl.pallas_call(
        functools.partial(_conv_kernel, sub=sub),
        grid=(s // rows,),
        in_specs=[
            pl.BlockSpec((HALO, c), lambda i: (jnp.maximum(i * hb - 1, 0), 0)),
            pl.BlockSpec((rows, c), lambda i: (i, 0)),
            pl.BlockSpec((HALO, c), lambda i: (jnp.minimum((i + 1) * hb, last), 0)),
            _resident(w.shape), _resident(b.shape), _resident(g.shape),
        ],
        out_specs=pl.BlockSpec((rows, c), lambda i: (i, 0)),
        out_shape=jax.ShapeDtypeStruct((s, c), BF16),
        scratch_shapes=[pltpu.VMEM((c // LANES, SUBLANES, rows + 2 * HALO, LANES), F32),
                        pltpu.VMEM((c // LANES, rows, LANES), F32), pltpu.VMEM((rows, LANES), F32)],
        compiler_params=_params(("parallel",)),
        name="conv",
    )(u, u, u, w, b, g)


def _outproj_kernel(x_ref, a_ref, b_ref, wa_ref, wb_ref, wd_ref, o_ref, wdo_ref):
    _cast_rows_kernel(wd_ref, wdo_ref)
    o_ref[...] = (x_ref[...]
                  + jnp.dot(a_ref[...], wa_ref[...], preferred_element_type=F32)
                  + jnp.dot(b_ref[...], wb_ref[...], preferred_element_type=F32))


def _outproj(x1, a, b, wa, wb, wd, *, rows):
    s, d = x1.shape
    steps = s // rows
    f, dw = wd.shape
    fp = _padded_hidden(f)
    strip = max(LANES, dw // steps)
    per_strip = steps // (dw // strip)
    assert steps == per_strip * (dw // strip)
    row = lambda w: pl.BlockSpec((rows, w), lambda i: (i, 0))
    return pl.pallas_call(
        _outproj_kernel,
        grid=(steps,),
        in_specs=[row(d), row(a.shape[1]), row(b.shape[1]), _resident(wa.shape), _resident(wb.shape),
                  pl.BlockSpec((f, strip), lambda i: (0, i // per_strip))],
        out_specs=[row(d), pl.BlockSpec((fp, strip), lambda i: (0, i // per_strip))],
        out_shape=[jax.ShapeDtypeStruct((s, d), F32), jax.ShapeDtypeStruct((fp, dw), BF16)],
        compiler_params=_params(("arbitrary",)),
        name="outproj",
    )(x1, a, b, wa, wb, wd)


def _pad_cols(w, n):
    return jnp.pad(w, ((0, 0), (0, n - w.shape[1])))


def _cast_cols_kernel(w_ref, o_ref):
    nb, rows, width = o_ref.shape
    pad = nb * width - w_ref.shape[1]
    for j in range(nb):
        lo = j * width - pad
        if lo + width <= 0:
            o_ref[j] = jnp.zeros((rows, width), BF16)
        elif lo < 0:
            o_ref[j, :, :-lo] = jnp.zeros((rows, -lo), BF16)
            o_ref[j, :, -lo:] = w_ref[:, :lo + width].astype(BF16)
        else:
            o_ref[j] = w_ref[:, lo:lo + width].astype(BF16)


def _cast_pad_cols(w, n_out):
    r, n = w.shape
    rows = min(CAST_ROWS, r)
    nb = n_out // FF_TILE
    return pl.pallas_call(
        _cast_cols_kernel,
        grid=(r // rows,),
        in_specs=[pl.BlockSpec((rows, n), lambda i: (i, 0))],
        out_specs=pl.BlockSpec((nb, rows, FF_TILE), lambda i: (0, i, 0)),
        out_shape=jax.ShapeDtypeStruct((nb, r, FF_TILE), BF16),
        compiler_params=_params(("parallel",)),
        name="cast_cols",
    )(w)


def _cast_rows_kernel(w_ref, o_ref):
    pad = o_ref.shape[0] - w_ref.shape[0]
    o_ref[:pad, :] = jnp.zeros((pad, o_ref.shape[1]), BF16)
    o_ref[pad:, :] = w_ref[...].astype(BF16)


def _cast_pad_rows(w, n_out):
    n, c = w.shape
    cols = min(CAST_COLS, c)
    return pl.pallas_call(
        _cast_rows_kernel,
        grid=(c // cols,),
        in_specs=[pl.BlockSpec((n, cols), lambda i: (0, i))],
        out_specs=pl.BlockSpec((n_out, cols), lambda i: (0, i)),
        out_shape=jax.ShapeDtypeStruct((n_out, c), BF16),
        compiler_params=_params(("parallel",)),
        name="cast_rows",
    )(w)


def _padded_hidden(f):
    assert f % LANES == 0
    return -(-f // FF_TILE) * FF_TILE


def _ffn_weights(wg, wu, wd):
    fp = _padded_hidden(wg.shape[1])
    return _cast_pad_cols(wg, fp), _cast_pad_cols(wu, fp), _cast_pad_rows(wd, fp)


def _head_gain(g):
    return jnp.pad(g, (0, HEAD_PAD - QK_DIM)).reshape(1, HEAD_PAD)


def _layer(x, positions, tiles, ffn1_norm, ffn1_w_gate, ffn1_w_up, ffn1_w_down, mix_norm, w_in,
           q_norm, w_uq, kv_norm, w_ukv, q_head_norm, k_head_norm, dw_kernel, dw_bias, conv_norm,
           w_out, ffn2_norm, ffn2_w_gate, ffn2_w_up, ffn2_w_down, final_norm):
    s, d = x.shape
    row = lambda g: g.reshape(1, -1)

    hidden = ffn1_w_gate.shape[1]
    x1 = _ffn(x, row(ffn1_norm), *_ffn_weights(ffn1_w_gate, ffn1_w_up, ffn1_w_down), None,
              rows=tiles["ffn"], hidden=hidden)

    c2 = 2 * CONV_CH + Q_RANK + KV_RANK
    win = _pad_cols(w_in.astype(BF16), c2 + LANES)
    wuq = jnp.pad(w_uq.astype(BF16).reshape(Q_RANK, N_HEADS, QK_DIM),
                  ((0, 0), (0, 0), (0, HEAD_PAD - QK_DIM))).reshape(Q_RANK, N_HEADS * HEAD_PAD)
    wkv = w_ukv.astype(BF16).reshape(KV_RANK, N_HEADS, NOPE + V_DIM)
    wuk = wkv[:, :, :NOPE].reshape(KV_RANK, N_HEADS * NOPE)
    wuv = wkv[:, :, NOPE:].reshape(KV_RANK, N_HEADS * V_DIM)
    inv_freq = ROPE_THETA ** (-(jnp.arange(ROPE // 2, dtype=F32) * 2.0 / ROPE))
    u, qt, k, vt, wg2, wu2 = _proj(
        x1, positions.reshape(1, s), row(mix_norm), win, row(q_norm), wuq.T, row(kv_norm), wuk,
        wuv.T, _head_gain(q_head_norm), _head_gain(k_head_norm), inv_freq, ffn2_w_gate, ffn2_w_up,
        rows=tiles["proj"])

    b_out = _attn(qt, k, vt, tq=tiles["attn_q"], tk=tiles["attn_k"])
    a_out = _conv(u, jnp.pad(dw_kernel, ((0, 1), (0, 0))), row(dw_bias), row(conv_norm),
                  rows=tiles["conv"], sub=tiles["conv_sub"])
    wo = w_out.astype(BF16)
    x2, wd2 = _outproj(x1, a_out, b_out, wo[:CONV_CH], wo[CONV_CH:], ffn2_w_down, rows=tiles["out"])

    return _ffn(x2, row(ffn2_norm), wg2, wu2, wd2, row(final_norm), rows=tiles["ffn"],
                hidden=hidden)


def _tiles(s):
    return {
        "ffn": min(FFN_ROWS, s), "proj": min(PROJ_ROWS, s), "attn_q": min(ATTN_Q, s),
        "attn_k": min(ATTN_K, s), "conv": min(CONV_ROWS, s), "conv_sub": CONV_SUB,
        "out": min(OUT_ROWS, s),
    }


def kernel(x, positions, ffn1_norm, ffn1_w_gate, ffn1_w_up, ffn1_w_down, mix_norm, w_in, q_norm, w_uq, kv_norm, w_ukv, q_head_norm, k_head_norm, dw_kernel, dw_bias, conv_norm, w_out, ffn2_norm, ffn2_w_gate, ffn2_w_up, ffn2_w_down, final_norm):
    batch, s, d = x.shape
    assert ffn1_norm.shape[0] == 1, "single-layer stack expected"
    outs = []
    for bi in range(batch):
        outs.append(_layer(
            x[bi], positions[bi], _tiles(s), ffn1_norm[0], ffn1_w_gate[0], ffn1_w_up[0],
            ffn1_w_down[0], mix_norm[0], w_in[0], q_norm[0], w_uq[0], kv_norm[0], w_ukv[0],
            q_head_norm[0], k_head_norm[0], dw_kernel[0], dw_bias[0], conv_norm[0], w_out[0],
            ffn2_norm[0], ffn2_w_gate[0], ffn2_w_up[0], ffn2_w_down[0], final_norm[0]))
    return jnp.stack(outs)
```

```python
import functools
import math

import jax
import jax.numpy as jnp
from jax import lax
from jax.experimental import pallas as pl
from jax.experimental.pallas import tpu as pltpu

F32 = jnp.float32
BF16 = jnp.bfloat16

EPS = 1e-6
ROPE_THETA = 10000.0
CONV_CH = 1024
CONV_WIDTH = 31
CONV_PAD = CONV_WIDTH // 2
N_HEADS = 8
NOPE = 128
ROPE = 64
QK_DIM = NOPE + ROPE
V_DIM = 128
Q_RANK = 512
KV_RANK = 256

LANES = 128
SUBLANES = 8
SUBLANES_BF16 = 16
HEAD_PAD = 2 * LANES
V_PAD = V_DIM + SUBLANES_BF16
VMEM_LIMIT = 56 * 1024 * 1024

FF_TILE = 1024
FF_UNIT = 512
FFN_ROWS = 512
PROJ_ROWS = 512
PROJ_SUB = 256
ATTN_Q = 2048
ATTN_K = 1024
CONV_ROWS = 256
CONV_SUB = 256
OUT_ROWS = 512
CAST_ROWS = 256
CAST_COLS = 256
HALO = 16
MAX_JUMP = 64.0


def _rms(x, g):
    ms = jnp.mean(x * x, axis=-1, keepdims=True)
    return x * lax.rsqrt(ms + EPS) * g


def _params(sem):
    return pltpu.CompilerParams(dimension_semantics=sem, vmem_limit_bytes=VMEM_LIMIT)


def _resident(shape):
    nd = len(shape)
    return pl.BlockSpec(shape, lambda *_: (0,) * nd, pipeline_mode=pl.Buffered(1))


def _ffn_kernel(*refs, final_norm, skip):
    if final_norm:
        x_ref, g_ref, wg_ref, wu_ref, wd_ref, fg_ref, o_ref, xn_ref = refs
    else:
        x_ref, g_ref, wg_ref, wu_ref, wd_ref, o_ref, xn_ref = refs
    f = pl.program_id(1)
    tf = wg_ref.shape[1]

    def hidden_tiles(start):
        xn = xn_ref[...]
        for c in range(start, tf, FF_UNIT):
            gate = jnp.dot(xn, wg_ref[:, c:c + FF_UNIT], preferred_element_type=F32)
            up = jnp.dot(xn, wu_ref[:, c:c + FF_UNIT], preferred_element_type=F32)
            h = (gate * jax.nn.sigmoid(gate) * up).astype(BF16)
            o_ref[...] += jnp.dot(h, wd_ref[c:c + FF_UNIT, :], preferred_element_type=F32)

    @pl.when(f == 0)
    def _():
        xn_ref[...] = _rms(x_ref[...], g_ref[...]).astype(BF16)
        o_ref[...] = jnp.zeros_like(o_ref)
        hidden_tiles(skip)

    @pl.when(f > 0)
    def _():
        hidden_tiles(0)

    @pl.when(f == pl.num_programs(1) - 1)
    def _():
        y = x_ref[...] + 0.5 * o_ref[...]
        if final_norm:
            y = _rms(y, fg_ref[...])
        o_ref[...] = y


def _ffn(x, g, wg, wu, wd, final_g, *, rows, hidden):
    s, d = x.shape
    fp = wd.shape[0]
    tf = FF_TILE
    skip = (fp - hidden) // FF_UNIT * FF_UNIT
    assert fp % tf == 0 and 0 <= skip < tf
    in_specs = [
        pl.BlockSpec((rows, d), lambda i, f: (i, 0)),
        pl.BlockSpec((1, d), lambda i, f: (0, 0)),
        pl.BlockSpec((None, d, tf), lambda i, f: (f, 0, 0)),
        pl.BlockSpec((None, d, tf), lambda i, f: (f, 0, 0)),
        pl.BlockSpec((tf, d), lambda i, f: (f, 0)),
    ]
    args = [x, g, wg, wu, wd]
    if final_g is not None:
        in_specs.append(pl.BlockSpec((1, d), lambda i, f: (0, 0)))
        args.append(final_g)
    return pl.pallas_call(
        functools.partial(_ffn_kernel, final_norm=final_g is not None, skip=skip),
        grid=(s // rows, fp // tf),
        in_specs=in_specs,
        out_specs=pl.BlockSpec((rows, d), lambda i, f: (i, 0)),
        out_shape=jax.ShapeDtypeStruct((s, d), F32),
        scratch_shapes=[pltpu.VMEM((rows, d), BF16)],
        compiler_params=_params(("parallel", "arbitrary")),
        name="ffn_final" if final_g is not None else "ffn",
    )(*args)


def _proj_kernel(x_ref, pos_ref, mixg_ref, win_ref, wpe_ref, qg_ref, wuqt_ref, kvg_ref, wuk_ref,
                 wuvt_ref, qhgt_ref, khg_ref, invft_ref, wg_ref, wu_ref,
                 u_ref, qt_ref, k_ref, vt_ref, wgo_ref, wuo_ref, *, sub):
    _cast_cols_kernel(wg_ref, wgo_ref)
    _cast_cols_kernel(wu_ref, wuo_ref)

    c0 = 2 * CONV_CH
    c1 = c0 + Q_RANK
    c2 = c1 + KV_RANK
    half = ROPE // 2
    inv_d = 1.0 / QK_DIM
    scale = math.log2(math.e) / math.sqrt(QK_DIM)
    qhgt = qhgt_ref[...]
    khg = khg_ref[...]
    lane = lax.broadcasted_iota(jnp.int32, (sub, LANES), 1)
    one_col = jnp.where(lane == ROPE, 1.0, 0.0)
    ones_rows = jnp.where(
        lax.broadcasted_iota(jnp.int32, (V_PAD - V_DIM, sub), 0) == 0, 1.0, 0.0).astype(BF16)
    q_tail = jnp.zeros((HEAD_PAD - QK_DIM, sub), BF16)

    def part(r0):
        rs = slice(r0, r0 + sub)
        hn = _rms(x_ref[rs, :], mixg_ref[...]).astype(BF16)
        z = jnp.dot(hn, win_ref[...], preferred_element_type=F32)
        kpe = jnp.dot(hn, wpe_ref[...], preferred_element_type=F32)
        u_ref[rs, :] = z[:, :CONV_CH] * jax.nn.sigmoid(z[:, CONV_CH:c0])
        qnt = _rms(z[:, c0:c1], qg_ref[...]).T.astype(BF16)
        kvn = _rms(z[:, c1:c2], kvg_ref[...])
        kvnt = kvn.T.astype(BF16)
        qt = jnp.dot(wuqt_ref[...], qnt, preferred_element_type=F32)
        vt = jnp.dot(wuvt_ref[...], kvnt, preferred_element_type=F32)
        kn = jnp.dot(kvn.astype(BF16), wuk_ref[...], preferred_element_type=F32)

        ang = invft_ref[...] * pos_ref[:, rs].astype(F32)
        cos_t = jnp.cos(ang)
        sin_t = jnp.sin(ang)
        zeros = jnp.zeros((LANES - ROPE, sub), F32)
        cos_k = jnp.concatenate([cos_t, cos_t, zeros], axis=0).T
        sin_k = jnp.concatenate([-sin_t, sin_t, zeros], axis=0).T

        kpe_g = kpe * khg[:, NOPE:]
        kpe_rot = kpe_g * cos_k + jnp.where(lane < half, pltpu.roll(kpe_g, LANES - half, 1),
                                            pltpu.roll(kpe_g, half, 1)) * sin_k
        kpe_ss = jnp.sum(kpe * kpe, axis=-1, keepdims=True)
        for h in range(N_HEADS):
            qh = qt[h * HEAD_PAD:h * HEAD_PAD + QK_DIM, :]
            r = lax.rsqrt(jnp.sum(qh * qh, axis=0, keepdims=True) * inv_d + EPS) * scale
            qh = qh * qhgt[:QK_DIM, :] * r
            x1 = qh[NOPE:NOPE + half, :]
            x2 = qh[NOPE + half:, :]
            qt_ref[h, :NOPE, rs] = qh[:NOPE, :].astype(BF16)
            qt_ref[h, NOPE:NOPE + half, rs] = (x1 * cos_t - x2 * sin_t).astype(BF16)
            qt_ref[h, NOPE + half:QK_DIM, rs] = (x2 * cos_t + x1 * sin_t).astype(BF16)
            qt_ref[h, QK_DIM:, rs] = q_tail
            ka = kn[:, h * NOPE:(h + 1) * NOPE]
            rk = lax.rsqrt((jnp.sum(ka * ka, axis=-1, keepdims=True) + kpe_ss) * inv_d + EPS)
            k_ref[h, rs, :NOPE] = (ka * rk * khg[:, :NOPE]).astype(BF16)
            k_ref[h, rs, NOPE:] = (kpe_rot * rk + one_col).astype(BF16)
            vt_ref[h, :V_DIM, rs] = vt[h * V_DIM:(h + 1) * V_DIM, :].astype(BF16)
            vt_ref[h, V_DIM:, rs] = ones_rows

    for r0 in range(0, x_ref.shape[0], sub):
        part(r0)


def _proj(x1, pos, mixg, win, wpe, qg, wuqt, kvg, wuk, wuvt, qhg, khg, inv_freq, wg, wu, *,
          rows):
    s, d = x1.shape
    f = wg.shape[1]
    fp = _padded_hidden(f)
    w_rows = d // (s // rows)
    assert w_rows % SUBLANES_BF16 == 0
    sub = min(PROJ_SUB, rows)
    qhgt = jnp.broadcast_to(qhg.reshape(HEAD_PAD, 1), (HEAD_PAD, sub))
    invft = jnp.broadcast_to(inv_freq.reshape(-1, 1), (ROPE // 2, sub))
    row = lambda w: pl.BlockSpec((rows, w), lambda i: (i, 0))
    in_specs = [row(d), pl.BlockSpec((1, rows), lambda i: (0, i)), _resident(mixg.shape),
                _resident(win.shape), _resident(wpe.shape), _resident(qg.shape),
                _resident(wuqt.shape),
                _resident(kvg.shape), _resident(wuk.shape), _resident(wuvt.shape),
                _resident(qhgt.shape), _resident(khg.shape), _resident(invft.shape),
                pl.BlockSpec((w_rows, f), lambda i: (i, 0)),
                pl.BlockSpec((w_rows, f), lambda i: (i, 0))]
    out_shape = [
        jax.ShapeDtypeStruct((s, CONV_CH), F32),
        jax.ShapeDtypeStruct((N_HEADS, HEAD_PAD, s), BF16),
        jax.ShapeDtypeStruct((N_HEADS, s, HEAD_PAD), BF16),
        jax.ShapeDtypeStruct((N_HEADS, V_PAD, s), BF16),
        jax.ShapeDtypeStruct((fp // FF_TILE, d, FF_TILE), BF16),
        jax.ShapeDtypeStruct((fp // FF_TILE, d, FF_TILE), BF16),
    ]
    out_specs = [
        row(CONV_CH),
        pl.BlockSpec((N_HEADS, HEAD_PAD, rows), lambda i: (0, 0, i)),
        pl.BlockSpec((N_HEADS, rows, HEAD_PAD), lambda i: (0, i, 0)),
        pl.BlockSpec((N_HEADS, V_PAD, rows), lambda i: (0, 0, i)),
        pl.BlockSpec((fp // FF_TILE, w_rows, FF_TILE), lambda i: (0, i, 0)),
        pl.BlockSpec((fp // FF_TILE, w_rows, FF_TILE), lambda i: (0, i, 0)),
    ]
    return pl.pallas_call(
        functools.partial(_proj_kernel, sub=sub),
        grid=(s // rows,),
        in_specs=in_specs,
        out_specs=out_specs,
        out_shape=out_shape,
        compiler_params=_params(("parallel",)),
        name="proj",
    )(x1, pos, mixg, win, wpe, qg, wuqt, kvg, wuk, wuvt, qhgt, khg, invft, wg, wu)


def _attn_kernel(qt_ref, k_ref, vt_ref, o_ref, qa_ref, p0_ref, p1_ref, r_ref, pm_ref, a_ref,
                 bad_ref, acc_ref, *, tk):
    nk = k_ref.shape[0] // tk
    tq = qt_ref.shape[1]
    p_refs = (p0_ref, p1_ref)
    first_row = lax.broadcasted_iota(jnp.int32, (SUBLANES_BF16, tq), 0) == 0

    def set_reference(r):
        qa_ref[QK_DIM:QK_DIM + SUBLANES_BF16, :] = jnp.where(first_row, -r, 0.0).astype(BF16)

    def scores(c):
        off = pl.multiple_of(c * tk, tk)
        return jnp.dot(k_ref[pl.ds(off, tk), :], qa_ref[...], preferred_element_type=F32)

    def values(c, slot, alpha):
        off = pl.multiple_of(c * tk, tk)
        pv = jnp.dot(vt_ref[:, pl.ds(off, tk)], p_refs[slot][...], preferred_element_type=F32)
        acc_ref[...] = alpha * acc_ref[...] + pv

    def emit(p, slot):
        pb = p.astype(BF16)
        p_refs[slot][...] = pb
        return jnp.max(pb, axis=0, keepdims=True).astype(F32)

    def step(c, slot):
        r_prev = r_ref[...]
        pm_prev = pm_ref[...]
        r = jnp.maximum(r_prev, r_prev + jnp.log2(pm_prev)).astype(BF16).astype(F32)
        alpha = jnp.exp2(r_prev - r)
        set_reference(r)
        pm_ref[...] = emit(jnp.exp2(scores(c)), slot)
        values(c - 1, 1 - slot, a_ref[...])
        bad_ref[...] = jnp.maximum(bad_ref[...], jnp.where(pm_prev <= 2.0 ** MAX_JUMP, 0.0, 1.0))
        r_ref[...] = r
        a_ref[...] = alpha

    qa_ref[...] = qt_ref[...]
    pm_ref[...] = emit(jnp.exp2(scores(0)), 0)
    r_ref[...] = jnp.zeros_like(r_ref)
    a_ref[...] = jnp.zeros_like(a_ref)
    bad_ref[...] = jnp.zeros_like(bad_ref)
    acc_ref[...] = jnp.zeros_like(acc_ref)

    def body(t, carry):
        c = 2 * t + 1
        step(c, 1)
        step(c + 1, 0)
        return carry

    lax.fori_loop(0, (nk - 2) // 2, body, 0)
    step(nk - 1, 1)
    values(nk - 1, 1, a_ref[...])
    bad = jnp.maximum(bad_ref[...], jnp.where(pm_ref[...] <= 2.0 ** MAX_JUMP, 0.0, 1.0))
    bad = jnp.maximum(bad, jnp.where(acc_ref[V_DIM:V_DIM + 1, :] >= 2.0 ** -MAX_JUMP, 0.0, 1.0))

    @pl.when(jnp.max(bad) > 0.0)
    def _():
        set_reference(jnp.zeros_like(bad))
        r_ref[...] = jnp.full(r_ref.shape, -jnp.inf, F32)
        acc_ref[...] = jnp.zeros_like(acc_ref)

        def exact(c, carry):
            st = scores(c)
            m_prev = r_ref[...]
            m_new = jnp.maximum(m_prev, jnp.max(st, axis=0, keepdims=True))
            p0_ref[...] = jnp.exp2(st - m_new).astype(BF16)
            values(c, 0, jnp.exp2(m_prev - m_new))
            r_ref[...] = m_new
            return carry

        lax.fori_loop(0, nk, exact, 0)

    o = acc_ref[:V_DIM, :] * (1.0 / acc_ref[V_DIM:V_DIM + 1, :])
    o_ref[...] = o.T.astype(o_ref.dtype)


def _attn(qt, k, vt, *, tq, tk):
    h, s, dp = k.shape
    assert (s // tk) % 2 == 0
    stat = pltpu.VMEM((1, tq), F32)
    return pl.pallas_call(
        functools.partial(_attn_kernel, tk=tk),
        grid=(h, s // tq),
        in_specs=[
            pl.BlockSpec((None, dp, tq), lambda hh, i: (hh, 0, i)),
            pl.BlockSpec((None, s, dp), lambda hh, i: (hh, 0, 0)),
            pl.BlockSpec((None, V_PAD, s), lambda hh, i: (hh, 0, 0)),
        ],
        out_specs=pl.BlockSpec((tq, V_DIM), lambda hh, i: (i, hh)),
        out_shape=jax.ShapeDtypeStruct((s, h * V_DIM), BF16),
        scratch_shapes=[pltpu.VMEM((dp, tq), BF16),
                        pltpu.VMEM((tk, tq), BF16), pltpu.VMEM((tk, tq), BF16),
                        stat, stat, stat, stat, pltpu.VMEM((V_PAD, tq), F32)],
        compiler_params=_params(("parallel", "arbitrary")),
        name="attn",
    )(qt, k, vt)


def _conv_kernel(prev_ref, u_ref, next_ref, w_ref, b_ref, g_ref, o_ref, sh_ref, y_ref, sq_ref, *,
                 sub):
    i = pl.program_id(0)
    rows = u_ref.shape[0]
    blocks = CONV_CH // LANES
    has_prev = i > 0
    has_next = i < pl.num_programs(0) - 1
    for c in range(blocks):
        cs = slice(c * LANES, (c + 1) * LANES)
        sh_ref[c, 0, :HALO, :] = jnp.where(has_prev, prev_ref[:, cs], 0.0)
        sh_ref[c, 0, HALO:HALO + rows, :] = u_ref[:, cs]
        sh_ref[c, 0, HALO + rows:, :] = jnp.where(has_next, next_ref[:, cs], 0.0)
    n = rows + 2 * HALO - SUBLANES
    base = HALO - CONV_PAD

    sq_ref[...] = jnp.zeros_like(sq_ref)

    def lane_block(ci, carry):
        cs = pl.ds(pl.multiple_of(ci * LANES, LANES), LANES)
        for b in range(1, SUBLANES):
            sh_ref[ci, b, :n, :] = sh_ref[ci, 0, b:b + n, :]
        for r0 in range(0, rows, sub):
            acc = jnp.broadcast_to(b_ref[:, cs], (sub, LANES))
            for t in range(CONV_WIDTH):
                a, b = divmod(base + t, SUBLANES)
                lo = r0 + a * SUBLANES
                acc = acc + sh_ref[ci, b, lo:lo + sub, :] * w_ref[t:t + 1, cs]
            y_ref[ci, r0:r0 + sub, :] = acc
            sq_ref[r0:r0 + sub, :] += acc * acc
        return carry

    lax.fori_loop(0, blocks, lane_block, 0)
    ssq = jnp.sum(sq_ref[...], axis=-1, keepdims=True)
    inv = lax.rsqrt(ssq * (1.0 / CONV_CH) + EPS)
    for c in range(blocks):
        cs = slice(c * LANES, (c + 1) * LANES)
        y = y_ref[c] * inv * g_ref[:, cs]
        o_ref[:, cs] = (y * jax.nn.sigmoid(y)).astype(o_ref.dtype)


def _conv(u, w, b, g, *, rows, sub):
    s, c = u.shape
    hb = rows // HALO
    last = s // HALO - 1
    return pl.pallas_call(
        functools.partial(_conv_kernel, sub=sub),
        grid=(s // rows,),
        in_specs=[
            pl.BlockSpec((HALO, c), lambda i: (jnp.maximum(i * hb - 1, 0), 0)),
            pl.BlockSpec((rows, c), lambda i: (i, 0)),
            pl.BlockSpec((HALO, c), lambda i: (jnp.minimum((i + 1) * hb, last), 0)),
            _resident(w.shape), _resident(b.shape), _resident(g.shape),
        ],
        out_specs=pl.BlockSpec((rows, c), lambda i: (i, 0)),
        out_shape=jax.ShapeDtypeStruct((s, c), BF16),
        scratch_shapes=[pltpu.VMEM((c // LANES, SUBLANES, rows + 2 * HALO, LANES), F32),
                        pltpu.VMEM((c // LANES, rows, LANES), F32), pltpu.VMEM((rows, LANES), F32)],
        compiler_params=_params(("parallel",)),
        name="conv",
    )(u, u, u, w, b, g)


def _outproj_kernel(x_ref, a_ref, b_ref, wa_ref, wb_ref, wd_ref, o_ref, wdo_ref):
    _cast_rows_kernel(wd_ref, wdo_ref)
    o_ref[...] = (x_ref[...]
                  + jnp.dot(a_ref[...], wa_ref[...], preferred_element_type=F32)
                  + jnp.dot(b_ref[...], wb_ref[...], preferred_element_type=F32))


def _outproj(x1, a, b, wa, wb, wd, *, rows):
    s, d = x1.shape
    steps = s // rows
    f, dw = wd.shape
    fp = _padded_hidden(f)
    strip = max(LANES, dw // steps)
    per_strip = steps // (dw // strip)
    assert steps == per_strip * (dw // strip)
    row = lambda w: pl.BlockSpec((rows, w), lambda i: (i, 0))
    return pl.pallas_call(
        _outproj_kernel,
        grid=(steps,),
        in_specs=[row(d), row(a.shape[1]), row(b.shape[1]), _resident(wa.shape), _resident(wb.shape),
                  pl.BlockSpec((f, strip), lambda i: (0, i // per_strip))],
        out_specs=[row(d), pl.BlockSpec((fp, strip), lambda i: (0, i // per_strip))],
        out_shape=[jax.ShapeDtypeStruct((s, d), F32), jax.ShapeDtypeStruct((fp, dw), BF16)],
        compiler_params=_params(("arbitrary",)),
        name="outproj",
    )(x1, a, b, wa, wb, wd)


def _pad_cols(w, n):
    return jnp.pad(w, ((0, 0), (0, n - w.shape[1])))


def _cast_cols_kernel(w_ref, o_ref):
    nb, rows, width = o_ref.shape
    pad = nb * width - w_ref.shape[1]
    for j in range(nb):
        lo = j * width - pad
        if lo + width <= 0:
            o_ref[j] = jnp.zeros((rows, width), BF16)
        elif lo < 0:
            o_ref[j, :, :-lo] = jnp.zeros((rows, -lo), BF16)
            o_ref[j, :, -lo:] = w_ref[:, :lo + width].astype(BF16)
        else:
            o_ref[j] = w_ref[:, lo:lo + width].astype(BF16)


def _cast_pad_cols(w, n_out):
    r, n = w.shape
    rows = min(CAST_ROWS, r)
    nb = n_out // FF_TILE
    return pl.pallas_call(
        _cast_cols_kernel,
        grid=(r // rows,),
        in_specs=[pl.BlockSpec((rows, n), lambda i: (i, 0))],
        out_specs=pl.BlockSpec((nb, rows, FF_TILE), lambda i: (0, i, 0)),
        out_shape=jax.ShapeDtypeStruct((nb, r, FF_TILE), BF16),
        compiler_params=_params(("parallel",)),
        name="cast_cols",
    )(w)


def _cast_rows_kernel(w_ref, o_ref):
    pad = o_ref.shape[0] - w_ref.shape[0]
    o_ref[:pad, :] = jnp.zeros((pad, o_ref.shape[1]), BF16)
    o_ref[pad:, :] = w_ref[...].astype(BF16)


def _cast_pad_rows(w, n_out):
    n, c = w.shape
    cols = min(CAST_COLS, c)
    return pl.pallas_call(
        _cast_rows_kernel,
        grid=(c // cols,),
        in_specs=[pl.BlockSpec((n, cols), lambda i: (0, i))],
        out_specs=pl.BlockSpec((n_out, cols), lambda i: (0, i)),
        out_shape=jax.ShapeDtypeStruct((n_out, c), BF16),
        compiler_params=_params(("parallel",)),
        name="cast_rows",
    )(w)


def _padded_hidden(f):
    assert f % LANES == 0
    return -(-f // FF_TILE) * FF_TILE


def _ffn_weights(wg, wu, wd):
    fp = _padded_hidden(wg.shape[1])
    return _cast_pad_cols(wg, fp), _cast_pad_cols(wu, fp), _cast_pad_rows(wd, fp)


def _head_gain(g):
    return jnp.pad(g, (0, HEAD_PAD - QK_DIM)).reshape(1, HEAD_PAD)


def _layer(x, positions, tiles, ffn1_norm, ffn1_w_gate, ffn1_w_up, ffn1_w_down, mix_norm, w_in,
           q_norm, w_uq, kv_norm, w_ukv, q_head_norm, k_head_norm, dw_kernel, dw_bias, conv_norm,
           w_out, ffn2_norm, ffn2_w_gate, ffn2_w_up, ffn2_w_down, final_norm):
    s, d = x.shape
    row = lambda g: g.reshape(1, -1)

    hidden = ffn1_w_gate.shape[1]
    x1 = _ffn(x, row(ffn1_norm), *_ffn_weights(ffn1_w_gate, ffn1_w_up, ffn1_w_down), None,
              rows=tiles["ffn"], hidden=hidden)

    c2 = 2 * CONV_CH + Q_RANK + KV_RANK
    win = w_in[:, :c2].astype(BF16)
    wpe = _pad_cols(w_in[:, c2:].astype(BF16), LANES)
    wuq = jnp.pad(w_uq.astype(BF16).reshape(Q_RANK, N_HEADS, QK_DIM),
                  ((0, 0), (0, 0), (0, HEAD_PAD - QK_DIM))).reshape(Q_RANK, N_HEADS * HEAD_PAD)
    wkv = w_ukv.astype(BF16).reshape(KV_RANK, N_HEADS, NOPE + V_DIM)
    wuk = wkv[:, :, :NOPE].reshape(KV_RANK, N_HEADS * NOPE)
    wuv = wkv[:, :, NOPE:].reshape(KV_RANK, N_HEADS * V_DIM)
    inv_freq = ROPE_THETA ** (-(jnp.arange(ROPE // 2, dtype=F32) * 2.0 / ROPE))
    u, qt, k, vt, wg2, wu2 = _proj(
        x1, positions.reshape(1, s), row(mix_norm), win, wpe, row(q_norm), wuq.T, row(kv_norm),
        wuk, wuv.T, _head_gain(q_head_norm), _head_gain(k_head_norm), inv_freq, ffn2_w_gate, ffn2_w_up,
        rows=tiles["proj"])

    b_out = _attn(qt, k, vt, tq=tiles["attn_q"], tk=tiles["attn_k"])
    a_out = _conv(u, jnp.pad(dw_kernel, ((0, 1), (0, 0))), row(dw_bias), row(conv_norm),
                  rows=tiles["conv"], sub=tiles["conv_sub"])
    wo = w_out.astype(BF16)
    x2, wd2 = _outproj(x1, a_out, b_out, wo[:CONV_CH], wo[CONV_CH:], ffn2_w_down, rows=tiles["out"])

    return _ffn(x2, row(ffn2_norm), wg2, wu2, wd2, row(final_norm), rows=tiles["ffn"],
                hidden=hidden)


def _tiles(s):
    return {
        "ffn": min(FFN_ROWS, s), "proj": min(PROJ_ROWS, s), "attn_q": min(ATTN_Q, s),
        "attn_k": min(ATTN_K, s), "conv": min(CONV_ROWS, s), "conv_sub": CONV_SUB,
        "out": min(OUT_ROWS, s),
    }


def kernel(x, positions, ffn1_norm, ffn1_w_gate, ffn1_w_up, ffn1_w_down, mix_norm, w_in, q_norm, w_uq, kv_norm, w_ukv, q_head_norm, k_head_norm, dw_kernel, dw_bias, conv_norm, w_out, ffn2_norm, ffn2_w_gate, ffn2_w_up, ffn2_w_down, final_norm):
    batch, s, d = x.shape
    assert ffn1_norm.shape[0] == 1, "single-layer stack expected"
    outs = []
    for bi in range(batch):
        outs.append(_layer(
            x[bi], positions[bi], _tiles(s), ffn1_norm[0], ffn1_w_gate[0], ffn1_w_up[0],
            ffn1_w_down[0], mix_norm[0], w_in[0], q_norm[0], w_uq[0], kv_norm[0], w_ukv[0],
            q_head_norm[0], k_head_norm[0], dw_kernel[0], dw_bias[0], conv_norm[0], w_out[0],
            ffn2_norm[0], ffn2_w_gate[0], ffn2_w_up[0], ffn2_w_down[0], final_norm[0]))
    return jnp.stack(outs)
```

```python
import functools
import math

import jax
import jax.numpy as jnp
from jax import lax
from jax.experimental import pallas as pl
from jax.experimental.pallas import tpu as pltpu

F32 = jnp.float32
BF16 = jnp.bfloat16

EPS = 1e-6
ROPE_THETA = 10000.0
CONV_CH = 1024
CONV_WIDTH = 31
CONV_PAD = CONV_WIDTH // 2
N_HEADS = 8
NOPE = 128
ROPE = 64
QK_DIM = NOPE + ROPE
V_DIM = 128
Q_RANK = 512
KV_RANK = 256

LANES = 128
SUBLANES = 8
SUBLANES_BF16 = 16
HEAD_PAD = 2 * LANES
VMEM_LIMIT = 56 * 1024 * 1024

FF_TILE = 1024
FF_UNIT = 512
FFN_ROWS = 512
PROJ_ROWS = 512
PROJ_SUB = 256
ATTN_Q = 2048
ATTN_K = 1024
CONV_ROWS = 256
CONV_SUB = 256
OUT_ROWS = 512
CAST_ROWS = 256
CAST_COLS = 256
HALO = 16
MAX_JUMP = 64.0


def _rms(x, g):
    ms = jnp.mean(x * x, axis=-1, keepdims=True)
    return x * lax.rsqrt(ms + EPS) * g


def _params(sem):
    return pltpu.CompilerParams(dimension_semantics=sem, vmem_limit_bytes=VMEM_LIMIT)


def _resident(shape):
    nd = len(shape)
    return pl.BlockSpec(shape, lambda *_: (0,) * nd, pipeline_mode=pl.Buffered(1))


def _ffn_kernel(*refs, final_norm, skip):
    if final_norm:
        x_ref, g_ref, wg_ref, wu_ref, wd_ref, fg_ref, o_ref, xn_ref = refs
    else:
        x_ref, g_ref, wg_ref, wu_ref, wd_ref, o_ref, xn_ref = refs
    f = pl.program_id(1)
    tf = wg_ref.shape[1]

    def hidden_tiles(start):
        xn = xn_ref[...]
        for c in range(start, tf, FF_UNIT):
            gate = jnp.dot(xn, wg_ref[:, c:c + FF_UNIT], preferred_element_type=F32)
            up = jnp.dot(xn, wu_ref[:, c:c + FF_UNIT], preferred_element_type=F32)
            h = (gate * jax.nn.sigmoid(gate) * up).astype(BF16)
            o_ref[...] += jnp.dot(h, wd_ref[c:c + FF_UNIT, :], preferred_element_type=F32)

    @pl.when(f == 0)
    def _():
        xn_ref[...] = _rms(x_ref[...], g_ref[...]).astype(BF16)
        o_ref[...] = jnp.zeros_like(o_ref)
        hidden_tiles(skip)

    @pl.when(f > 0)
    def _():
        hidden_tiles(0)

    @pl.when(f == pl.num_programs(1) - 1)
    def _():
        y = x_ref[...] + 0.5 * o_ref[...]
        if final_norm:
            y = _rms(y, fg_ref[...])
        o_ref[...] = y


def _ffn(x, g, wg, wu, wd, final_g, *, rows, hidden):
    s, d = x.shape
    fp = wd.shape[0]
    tf = FF_TILE
    skip = (fp - hidden) // FF_UNIT * FF_UNIT
    assert fp % tf == 0 and 0 <= skip < tf
    in_specs = [
        pl.BlockSpec((rows, d), lambda i, f: (i, 0)),
        pl.BlockSpec((1, d), lambda i, f: (0, 0)),
        pl.BlockSpec((None, d, tf), lambda i, f: (f, 0, 0)),
        pl.BlockSpec((None, d, tf), lambda i, f: (f, 0, 0)),
        pl.BlockSpec((tf, d), lambda i, f: (f, 0)),
    ]
    args = [x, g, wg, wu, wd]
    if final_g is not None:
        in_specs.append(pl.BlockSpec((1, d), lambda i, f: (0, 0)))
        args.append(final_g)
    return pl.pallas_call(
        functools.partial(_ffn_kernel, final_norm=final_g is not None, skip=skip),
        grid=(s // rows, fp // tf),
        in_specs=in_specs,
        out_specs=pl.BlockSpec((rows, d), lambda i, f: (i, 0)),
        out_shape=jax.ShapeDtypeStruct((s, d), F32),
        scratch_shapes=[pltpu.VMEM((rows, d), BF16)],
        compiler_params=_params(("parallel", "arbitrary")),
        name="ffn_final" if final_g is not None else "ffn",
    )(*args)


def _proj_kernel(x_ref, pos_ref, mixg_ref, win_ref, qg_ref, wuqt_ref, kvg_ref, wuk_ref, wuvt_ref,
                 qhgt_ref, khg_ref, invft_ref, wg_ref, wu_ref,
                 u_ref, qt_ref, k_ref, vt_ref, wgo_ref, wuo_ref, *, sub):
    _cast_cols_kernel(wg_ref, wgo_ref)
    _cast_cols_kernel(wu_ref, wuo_ref)

    c0 = 2 * CONV_CH
    c1 = c0 + Q_RANK
    c2 = c1 + KV_RANK
    half = ROPE // 2
    inv_d = 1.0 / QK_DIM
    scale = math.log2(math.e) / math.sqrt(QK_DIM)
    qhgt = qhgt_ref[...]
    khg = khg_ref[...]
    lane = lax.broadcasted_iota(jnp.int32, (sub, LANES), 1)
    one_col = jnp.where(lane == ROPE, 1.0, 0.0)
    q_tail = jnp.zeros((HEAD_PAD - QK_DIM, sub), BF16)

    def part(r0):
        rs = slice(r0, r0 + sub)
        hn = _rms(x_ref[rs, :], mixg_ref[...]).astype(BF16)
        z = jnp.dot(hn, win_ref[...], preferred_element_type=F32)
        u_ref[rs, :] = z[:, :CONV_CH] * jax.nn.sigmoid(z[:, CONV_CH:c0])
        qnt = _rms(z[:, c0:c1], qg_ref[...]).T.astype(BF16)
        kvn = _rms(z[:, c1:c2], kvg_ref[...])
        kvnt = kvn.T.astype(BF16)
        kpe = z[:, c2:c2 + LANES]
        qt = jnp.dot(wuqt_ref[...], qnt, preferred_element_type=F32)
        vt = jnp.dot(wuvt_ref[...], kvnt, preferred_element_type=F32)
        kn = jnp.dot(kvn.astype(BF16), wuk_ref[...], preferred_element_type=F32)

        ang = invft_ref[...] * pos_ref[:, rs].astype(F32)
        cos_t = jnp.cos(ang)
        sin_t = jnp.sin(ang)
        zeros = jnp.zeros((LANES - ROPE, sub), F32)
        cos_k = jnp.concatenate([cos_t, cos_t, zeros], axis=0).T
        sin_k = jnp.concatenate([-sin_t, sin_t, zeros], axis=0).T

        kpe_g = kpe * khg[:, NOPE:]
        kpe_rot = kpe_g * cos_k + jnp.where(lane < half, pltpu.roll(kpe_g, LANES - half, 1),
                                            pltpu.roll(kpe_g, half, 1)) * sin_k
        kpe_ss = jnp.sum(kpe * kpe, axis=-1, keepdims=True)
        for h in range(N_HEADS):
            qh = qt[h * HEAD_PAD:h * HEAD_PAD + QK_DIM, :]
            r = lax.rsqrt(jnp.sum(qh * qh, axis=0, keepdims=True) * inv_d + EPS) * scale
            qh = qh * qhgt[:QK_DIM, :] * r
            x1 = qh[NOPE:NOPE + half, :]
            x2 = qh[NOPE + half:, :]
            qt_ref[h, :NOPE, rs] = qh[:NOPE, :].astype(BF16)
            qt_ref[h, NOPE:NOPE + half, rs] = (x1 * cos_t - x2 * sin_t).astype(BF16)
            qt_ref[h, NOPE + half:QK_DIM, rs] = (x2 * cos_t + x1 * sin_t).astype(BF16)
            qt_ref[h, QK_DIM:, rs] = q_tail
            ka = kn[:, h * NOPE:(h + 1) * NOPE]
            rk = lax.rsqrt((jnp.sum(ka * ka, axis=-1, keepdims=True) + kpe_ss) * inv_d + EPS)
            k_ref[h, rs, :NOPE] = (ka * rk * khg[:, :NOPE]).astype(BF16)
            k_ref[h, rs, NOPE:] = (kpe_rot * rk + one_col).astype(BF16)
            vt_ref[h, :, rs] = vt[h * V_DIM:(h + 1) * V_DIM, :].astype(BF16)

    for r0 in range(0, x_ref.shape[0], sub):
        part(r0)


def _proj(x1, pos, mixg, win, qg, wuqt, kvg, wuk, wuvt, qhg, khg, inv_freq, wg, wu, *, rows):
    s, d = x1.shape
    f = wg.shape[1]
    fp = _padded_hidden(f)
    w_rows = d // (s // rows)
    assert w_rows % SUBLANES_BF16 == 0
    sub = min(PROJ_SUB, rows)
    qhgt = jnp.broadcast_to(qhg.reshape(HEAD_PAD, 1), (HEAD_PAD, sub))
    invft = jnp.broadcast_to(inv_freq.reshape(-1, 1), (ROPE // 2, sub))
    row = lambda w: pl.BlockSpec((rows, w), lambda i: (i, 0))
    in_specs = [row(d), pl.BlockSpec((1, rows), lambda i: (0, i)), _resident(mixg.shape),
                _resident(win.shape), _resident(qg.shape), _resident(wuqt.shape),
                _resident(kvg.shape), _resident(wuk.shape), _resident(wuvt.shape),
                _resident(qhgt.shape), _resident(khg.shape), _resident(invft.shape),
                pl.BlockSpec((w_rows, f), lambda i: (i, 0)),
                pl.BlockSpec((w_rows, f), lambda i: (i, 0))]
    out_shape = [
        jax.ShapeDtypeStruct((s, CONV_CH), F32),
        jax.ShapeDtypeStruct((N_HEADS, HEAD_PAD, s), BF16),
        jax.ShapeDtypeStruct((N_HEADS, s, HEAD_PAD), BF16),
        jax.ShapeDtypeStruct((N_HEADS, V_DIM, s), BF16),
        jax.ShapeDtypeStruct((fp // FF_TILE, d, FF_TILE), BF16),
        jax.ShapeDtypeStruct((fp // FF_TILE, d, FF_TILE), BF16),
    ]
    out_specs = [
        row(CONV_CH),
        pl.BlockSpec((N_HEADS, HEAD_PAD, rows), lambda i: (0, 0, i)),
        pl.BlockSpec((N_HEADS, rows, HEAD_PAD), lambda i: (0, i, 0)),
        pl.BlockSpec((N_HEADS, V_DIM, rows), lambda i: (0, 0, i)),
        pl.BlockSpec((fp // FF_TILE, w_rows, FF_TILE), lambda i: (0, i, 0)),
        pl.BlockSpec((fp // FF_TILE, w_rows, FF_TILE), lambda i: (0, i, 0)),
    ]
    return pl.pallas_call(
        functools.partial(_proj_kernel, sub=sub),
        grid=(s // rows,),
        in_specs=in_specs,
        out_specs=out_specs,
        out_shape=out_shape,
        compiler_params=_params(("parallel",)),
        name="proj",
    )(x1, pos, mixg, win, qg, wuqt, kvg, wuk, wuvt, qhgt, khg, invft, wg, wu)


def _attn_kernel(qt_ref, k_ref, vt_ref, o_ref, qa_ref, p0_ref, p1_ref, r_ref, pm_ref, a_ref,
                 bad_ref, l_ref, acc_ref, *, tk):
    nk = k_ref.shape[0] // tk
    tq = qt_ref.shape[1]
    p_refs = (p0_ref, p1_ref)
    first_row = lax.broadcasted_iota(jnp.int32, (SUBLANES_BF16, tq), 0) == 0

    def set_reference(r):
        qa_ref[QK_DIM:QK_DIM + SUBLANES_BF16, :] = jnp.where(first_row, -r, 0.0).astype(BF16)

    def scores(c):
        off = pl.multiple_of(c * tk, tk)
        return jnp.dot(k_ref[pl.ds(off, tk), :], qa_ref[...], preferred_element_type=F32)

    def values(c, slot, alpha):
        off = pl.multiple_of(c * tk, tk)
        pv = jnp.dot(vt_ref[:, pl.ds(off, tk)], p_refs[slot][...], preferred_element_type=F32)
        acc_ref[...] = alpha * acc_ref[...] + pv

    def emit(p, slot):
        pb = p.astype(BF16)
        p_refs[slot][...] = pb
        pm_ref[...] = jnp.max(pb, axis=0, keepdims=True).astype(F32)
        return jnp.sum(p, axis=0, keepdims=True)

    def step(c, slot):
        r_prev = r_ref[...]
        pm_prev = pm_ref[...]
        r = jnp.maximum(r_prev, r_prev + jnp.log2(pm_prev)).astype(BF16).astype(F32)
        alpha = jnp.exp2(r_prev - r)
        set_reference(r)
        l_ref[...] = alpha * l_ref[...] + emit(jnp.exp2(scores(c)), slot)
        values(c - 1, 1 - slot, a_ref[...])
        bad_ref[...] = jnp.maximum(bad_ref[...], jnp.where(pm_prev <= 2.0 ** MAX_JUMP, 0.0, 1.0))
        r_ref[...] = r
        a_ref[...] = alpha

    qa_ref[...] = qt_ref[...]
    l_ref[...] = emit(jnp.exp2(scores(0)), 0)
    r_ref[...] = jnp.zeros_like(r_ref)
    a_ref[...] = jnp.zeros_like(a_ref)
    bad_ref[...] = jnp.zeros_like(bad_ref)
    acc_ref[...] = jnp.zeros_like(acc_ref)

    def body(t, carry):
        c = 2 * t + 1
        step(c, 1)
        step(c + 1, 0)
        return carry

    lax.fori_loop(0, (nk - 2) // 2, body, 0)
    step(nk - 1, 1)
    values(nk - 1, 1, a_ref[...])
    bad = jnp.maximum(bad_ref[...], jnp.where(pm_ref[...] <= 2.0 ** MAX_JUMP, 0.0, 1.0))
    bad = jnp.maximum(bad, jnp.where(l_ref[...] >= 2.0 ** -MAX_JUMP, 0.0, 1.0))

    @pl.when(jnp.max(bad) > 0.0)
    def _():
        set_reference(jnp.zeros_like(bad))
        r_ref[...] = jnp.full(r_ref.shape, -jnp.inf, F32)
        l_ref[...] = jnp.zeros_like(l_ref)
        acc_ref[...] = jnp.zeros_like(acc_ref)

        def exact(c, carry):
            st = scores(c)
            m_prev = r_ref[...]
            m_new = jnp.maximum(m_prev, jnp.max(st, axis=0, keepdims=True))
            alpha = jnp.exp2(m_prev - m_new)
            p = jnp.exp2(st - m_new)
            p0_ref[...] = p.astype(BF16)
            l_ref[...] = alpha * l_ref[...] + jnp.sum(p, axis=0, keepdims=True)
            values(c, 0, alpha)
            r_ref[...] = m_new
            return carry

        lax.fori_loop(0, nk, exact, 0)

    o = acc_ref[...] * (1.0 / l_ref[...])
    o_ref[...] = o.T.astype(o_ref.dtype)


def _attn(qt, k, vt, *, tq, tk):
    h, s, dp = k.shape
    assert (s // tk) % 2 == 0
    stat = pltpu.VMEM((1, tq), F32)
    return pl.pallas_call(
        functools.partial(_attn_kernel, tk=tk),
        grid=(h, s // tq),
        in_specs=[
            pl.BlockSpec((None, dp, tq), lambda hh, i: (hh, 0, i)),
            pl.BlockSpec((None, s, dp), lambda hh, i: (hh, 0, 0)),
            pl.BlockSpec((None, V_DIM, s), lambda hh, i: (hh, 0, 0)),
        ],
        out_specs=pl.BlockSpec((tq, V_DIM), lambda hh, i: (i, hh)),
        out_shape=jax.ShapeDtypeStruct((s, h * V_DIM), BF16),
        scratch_shapes=[pltpu.VMEM((dp, tq), BF16),
                        pltpu.VMEM((tk, tq), BF16), pltpu.VMEM((tk, tq), BF16),
                        stat, stat, stat, stat, stat, pltpu.VMEM((V_DIM, tq), F32)],
        compiler_params=_params(("parallel", "arbitrary")),
        name="attn",
    )(qt, k, vt)


def _conv_kernel(prev_ref, u_ref, next_ref, w_ref, b_ref, g_ref, o_ref, sh_ref, y_ref, sq_ref, *,
                 sub):
    i = pl.program_id(0)
    rows = u_ref.shape[0]
    blocks = CONV_CH // LANES
    has_prev = i > 0
    has_next = i < pl.num_programs(0) - 1
    for c in range(blocks):
        cs = slice(c * LANES, (c + 1) * LANES)
        sh_ref[c, 0, :HALO, :] = jnp.where(has_prev, prev_ref[:, cs], 0.0)
        sh_ref[c, 0, HALO:HALO + rows, :] = u_ref[:, cs]
        sh_ref[c, 0, HALO + rows:, :] = jnp.where(has_next, next_ref[:, cs], 0.0)
    n = rows + 2 * HALO - SUBLANES
    base = HALO - CONV_PAD

    sq_ref[...] = jnp.zeros_like(sq_ref)

    def lane_block(ci, carry):
        cs = pl.ds(pl.multiple_of(ci * LANES, LANES), LANES)
        for b in range(1, SUBLANES):
            sh_ref[ci, b, :n, :] = sh_ref[ci, 0, b:b + n, :]
        for r0 in range(0, rows, sub):
            acc = jnp.broadcast_to(b_ref[:, cs], (sub, LANES))
            for t in range(CONV_WIDTH):
                a, b = divmod(base + t, SUBLANES)
                lo = r0 + a * SUBLANES
                acc = acc + sh_ref[ci, b, lo:lo + sub, :] * w_ref[t:t + 1, cs]
            y_ref[ci, r0:r0 + sub, :] = acc
            sq_ref[r0:r0 + sub, :] += acc * acc
        return carry

    lax.fori_loop(0, blocks, lane_block, 0)
    ssq = jnp.sum(sq_ref[...], axis=-1, keepdims=True)
    inv = lax.rsqrt(ssq * (1.0 / CONV_CH) + EPS)
    for c in range(blocks):
        cs = slice(c * LANES, (c + 1) * LANES)
        y = y_ref[c] * inv * g_ref[:, cs]
        o_ref[:, cs] = (y * jax.nn.sigmoid(y)).astype(o_ref.dtype)


def _conv(u, w, b, g, *, rows, sub):
    s, c = u.shape
    hb = rows // HALO
    last = s // HALO - 1
    return pl.pallas_call(
        functools.partial(_conv_kernel, sub=sub),
        grid=(s // rows,),
        in_specs=[
            pl.BlockSpec((HALO, c), lambda i: (jnp.maximum(i * hb - 1, 0), 0)),
            pl.BlockSpec((rows, c), lambda i: (i, 0)),
            pl.BlockSpec((HALO, c), lambda i: (jnp.minimum((i + 1) * hb, last), 0)),
            _resident(w.shape), _resident(b.shape), _resident(g.shape),
        ],
        out_specs=pl.BlockSpec((rows, c), lambda i: (i, 0)),
        out_shape=jax.ShapeDtypeStruct((s, c), BF16),
        scratch_shapes=[pltpu.VMEM((c // LANES, SUBLANES, rows + 2 * HALO, LANES), F32),
                        pltpu.VMEM((c // LANES, rows, LANES), F32), pltpu.VMEM((rows, LANES), F32)],
        compiler_params=_params(("parallel",)),
        name="conv",
    )(u, u, u, w, b, g)


def _outproj_kernel(x_ref, a_ref, b_ref, wa_ref, wb_ref, wd_ref, o_ref, wdo_ref):
    _cast_rows_kernel(wd_ref, wdo_ref)
    o_ref[...] = (x_ref[...]
                  + jnp.dot(a_ref[...], wa_ref[...], preferred_element_type=F32)
                  + jnp.dot(b_ref[...], wb_ref[...], preferred_element_type=F32))


def _outproj(x1, a, b, wa, wb, wd, *, rows):
    s, d = x1.shape
    steps = s // rows
    f, dw = wd.shape
    fp = _padded_hidden(f)
    strip = max(LANES, dw // steps)
    per_strip = steps // (dw // strip)
    assert steps == per_strip * (dw // strip)
    row = lambda w: pl.BlockSpec((rows, w), lambda i: (i, 0))
    return pl.pallas_call(
        _outproj_kernel,
        grid=(steps,),
        in_specs=[row(d), row(a.shape[1]), row(b.shape[1]), _resident(wa.shape), _resident(wb.shape),
                  pl.BlockSpec((f, strip), lambda i: (0, i // per_strip))],
        out_specs=[row(d), pl.BlockSpec((fp, strip), lambda i: (0, i // per_strip))],
        out_shape=[jax.ShapeDtypeStruct((s, d), F32), jax.ShapeDtypeStruct((fp, dw), BF16)],
        compiler_params=_params(("arbitrary",)),
        name="outproj",
    )(x1, a, b, wa, wb, wd)


def _pad_cols(w, n):
    return jnp.pad(w, ((0, 0), (0, n - w.shape[1])))


def _cast_cols_kernel(w_ref, o_ref):
    nb, rows, width = o_ref.shape
    pad = nb * width - w_ref.shape[1]
    for j in range(nb):
        lo = j * width - pad
        if lo + width <= 0:
            o_ref[j] = jnp.zeros((rows, width), BF16)
        elif lo < 0:
            o_ref[j, :, :-lo] = jnp.zeros((rows, -lo), BF16)
            o_ref[j, :, -lo:] = w_ref[:, :lo + width].astype(BF16)
        else:
            o_ref[j] = w_ref[:, lo:lo + width].astype(BF16)


def _cast_pad_cols(w, n_out):
    r, n = w.shape
    rows = min(CAST_ROWS, r)
    nb = n_out // FF_TILE
    return pl.pallas_call(
        _cast_cols_kernel,
        grid=(r // rows,),
        in_specs=[pl.BlockSpec((rows, n), lambda i: (i, 0))],
        out_specs=pl.BlockSpec((nb, rows, FF_TILE), lambda i: (0, i, 0)),
        out_shape=jax.ShapeDtypeStruct((nb, r, FF_TILE), BF16),
        compiler_params=_params(("parallel",)),
        name="cast_cols",
    )(w)


def _cast_rows_kernel(w_ref, o_ref):
    pad = o_ref.shape[0] - w_ref.shape[0]
    o_ref[:pad, :] = jnp.zeros((pad, o_ref.shape[1]), BF16)
    o_ref[pad:, :] = w_ref[...].astype(BF16)


def _cast_pad_rows(w, n_out):
    n, c = w.shape
    cols = min(CAST_COLS, c)
    return pl.pallas_call(
        _cast_rows_kernel,
        grid=(c // cols,),
        in_specs=[pl.BlockSpec((n, cols), lambda i: (0, i))],
        out_specs=pl.BlockSpec((n_out, cols), lambda i: (0, i)),
        out_shape=jax.ShapeDtypeStruct((n_out, c), BF16),
        compiler_params=_params(("parallel",)),
        name="cast_rows",
    )(w)


def _padded_hidden(f):
    assert f % LANES == 0
    return -(-f // FF_TILE) * FF_TILE


def _ffn_weights(wg, wu, wd):
    fp = _padded_hidden(wg.shape[1])
    return _cast_pad_cols(wg, fp), _cast_pad_cols(wu, fp), _cast_pad_rows(wd, fp)


def _head_gain(g):
    return jnp.pad(g, (0, HEAD_PAD - QK_DIM)).reshape(1, HEAD_PAD)


def _layer(x, positions, tiles, ffn1_norm, ffn1_w_gate, ffn1_w_up, ffn1_w_down, mix_norm, w_in,
           q_norm, w_uq, kv_norm, w_ukv, q_head_norm, k_head_norm, dw_kernel, dw_bias, conv_norm,
           w_out, ffn2_norm, ffn2_w_gate, ffn2_w_up, ffn2_w_down, final_norm):
    s, d = x.shape
    row = lambda g: g.reshape(1, -1)

    hidden = ffn1_w_gate.shape[1]
    x1 = _ffn(x, row(ffn1_norm), *_ffn_weights(ffn1_w_gate, ffn1_w_up, ffn1_w_down), None,
              rows=tiles["ffn"], hidden=hidden)

    c2 = 2 * CONV_CH + Q_RANK + KV_RANK
    win = _pad_cols(w_in.astype(BF16), c2 + LANES)
    wuq = jnp.pad(w_uq.astype(BF16).reshape(Q_RANK, N_HEADS, QK_DIM),
                  ((0, 0), (0, 0), (0, HEAD_PAD - QK_DIM))).reshape(Q_RANK, N_HEADS * HEAD_PAD)
    wkv = w_ukv.astype(BF16).reshape(KV_RANK, N_HEADS, NOPE + V_DIM)
    wuk = wkv[:, :, :NOPE].reshape(KV_RANK, N_HEADS * NOPE)
    wuv = wkv[:, :, NOPE:].reshape(KV_RANK, N_HEADS * V_DIM)
    inv_freq = ROPE_THETA ** (-(jnp.arange(ROPE // 2, dtype=F32) * 2.0 / ROPE))
    u, qt, k, vt, wg2, wu2 = _proj(
        x1, positions.reshape(1, s), row(mix_norm), win, row(q_norm), wuq.T, row(kv_norm), wuk,
        wuv.T, _head_gain(q_head_norm), _head_gain(k_head_norm), inv_freq, ffn2_w_gate, ffn2_w_up,
        rows=tiles["proj"])

    b_out = _attn(qt, k, vt, tq=tiles["attn_q"], tk=tiles["attn_k"])
    a_out = _conv(u, jnp.pad(dw_kernel, ((0, 1), (0, 0))), row(dw_bias), row(conv_norm),
                  rows=tiles["conv"], sub=tiles["conv_sub"])
    wo = w_out.astype(BF16)
    x2, wd2 = _outproj(x1, a_out, b_out, wo[:CONV_CH], wo[CONV_CH:], ffn2_w_down, rows=tiles["out"])

    return _ffn(x2, row(ffn2_norm), wg2, wu2, wd2, row(final_norm), rows=tiles["ffn"],
                hidden=hidden)


def _tiles(s):
    return {
        "ffn": min(FFN_ROWS, s), "proj": min(PROJ_ROWS, s), "attn_q": min(ATTN_Q, s),
        "attn_k": min(ATTN_K, s), "conv": min(CONV_ROWS, s), "conv_sub": CONV_SUB,
        "out": min(OUT_ROWS, s),
    }


def kernel(x, positions, ffn1_norm, ffn1_w_gate, ffn1_w_up, ffn1_w_down, mix_norm, w_in, q_norm, w_uq, kv_norm, w_ukv, q_head_norm, k_head_norm, dw_kernel, dw_bias, conv_norm, w_out, ffn2_norm, ffn2_w_gate, ffn2_w_up, ffn2_w_down, final_norm):
    batch, s, d = x.shape
    assert ffn1_norm.shape[0] == 1, "single-layer stack expected"
    outs = []
    for bi in range(batch):
        outs.append(_layer(
            x[bi], positions[bi], _tiles(s), ffn1_norm[0], ffn1_w_gate[0], ffn1_w_up[0],
            ffn1_w_down[0], mix_norm[0], w_in[0], q_norm[0], w_uq[0], kv_norm[0], w_ukv[0],
            q_head_norm[0], k_head_norm[0], dw_kernel[0], dw_bias[0], conv_norm[0], w_out[0],
            ffn2_norm[0], ffn2_w_gate[0], ffn2_w_up[0], ffn2_w_down[0], final_norm[0]))
    return jnp.stack(outs)
```

```python
import functools
import math

import jax
import jax.numpy as jnp
from jax import lax
from jax.experimental import pallas as pl
from jax.experimental.pallas import tpu as pltpu

F32 = jnp.float32
BF16 = jnp.bfloat16

EPS = 1e-6
ROPE_THETA = 10000.0
CONV_CH = 1024
CONV_WIDTH = 31
CONV_PAD = CONV_WIDTH // 2
N_HEADS = 8
NOPE = 128
ROPE = 64
QK_DIM = NOPE + ROPE
V_DIM = 128
Q_RANK = 512
KV_RANK = 256

LANES = 128
SUBLANES = 8
SUBLANES_BF16 = 16
HEAD_PAD = 2 * LANES
V_PAD = V_DIM + SUBLANES_BF16
VMEM_LIMIT = 56 * 1024 * 1024

FF_TILE = 1024
FF_UNIT = 256
FFN_ROWS = 512
PROJ_ROWS = 512
PROJ_SUB = 256
ATTN_Q = 2048
ATTN_K = 1024
CONV_ROWS = 256
CONV_SUB = 256
OUT_ROWS = 512
CAST_ROWS = 256
CAST_COLS = 256
HALO = 16
MAX_JUMP = 64.0


def _rms(x, g):
    ms = jnp.mean(x * x, axis=-1, keepdims=True)
    return x * lax.rsqrt(ms + EPS) * g


def _params(sem):
    return pltpu.CompilerParams(dimension_semantics=sem, vmem_limit_bytes=VMEM_LIMIT)


def _resident(shape):
    nd = len(shape)
    return pl.BlockSpec(shape, lambda *_: (0,) * nd, pipeline_mode=pl.Buffered(1))


def _ffn_kernel(*refs, final_norm, skip):
    if final_norm:
        x_ref, g_ref, wg_ref, wu_ref, wd_ref, fg_ref, o_ref, xn_ref = refs
    else:
        x_ref, g_ref, wg_ref, wu_ref, wd_ref, o_ref, xn_ref = refs
    f = pl.program_id(1)
    tf = wg_ref.shape[1]

    def hidden_tiles(start):
        xn = xn_ref[...]
        for c in range(start, tf, FF_UNIT):
            gate = jnp.dot(xn, wg_ref[:, c:c + FF_UNIT], preferred_element_type=F32)
            up = jnp.dot(xn, wu_ref[:, c:c + FF_UNIT], preferred_element_type=F32)
            h = (gate * jax.nn.sigmoid(gate) * up).astype(BF16)
            o_ref[...] += jnp.dot(h, wd_ref[c:c + FF_UNIT, :], preferred_element_type=F32)

    @pl.when(f == 0)
    def _():
        xn_ref[...] = _rms(x_ref[...], g_ref[...]).astype(BF16)
        o_ref[...] = jnp.zeros_like(o_ref)
        hidden_tiles(skip)

    @pl.when(f > 0)
    def _():
        hidden_tiles(0)

    @pl.when(f == pl.num_programs(1) - 1)
    def _():
        y = x_ref[...] + 0.5 * o_ref[...]
        if final_norm:
            y = _rms(y, fg_ref[...])
        o_ref[...] = y


def _ffn(x, g, wg, wu, wd, final_g, *, rows, hidden):
    s, d = x.shape
    fp = wd.shape[0]
    tf = FF_TILE
    skip = (fp - hidden) // FF_UNIT * FF_UNIT
    assert fp % tf == 0 and 0 <= skip < tf
    in_specs = [
        pl.BlockSpec((rows, d), lambda i, f: (i, 0)),
        pl.BlockSpec((1, d), lambda i, f: (0, 0)),
        pl.BlockSpec((None, d, tf), lambda i, f: (f, 0, 0)),
        pl.BlockSpec((None, d, tf), lambda i, f: (f, 0, 0)),
        pl.BlockSpec((tf, d), lambda i, f: (f, 0)),
    ]
    args = [x, g, wg, wu, wd]
    if final_g is not None:
        in_specs.append(pl.BlockSpec((1, d), lambda i, f: (0, 0)))
        args.append(final_g)
    return pl.pallas_call(
        functools.partial(_ffn_kernel, final_norm=final_g is not None, skip=skip),
        grid=(s // rows, fp // tf),
        in_specs=in_specs,
        out_specs=pl.BlockSpec((rows, d), lambda i, f: (i, 0)),
        out_shape=jax.ShapeDtypeStruct((s, d), F32),
        scratch_shapes=[pltpu.VMEM((rows, d), BF16)],
        compiler_params=_params(("parallel", "arbitrary")),
        name="ffn_final" if final_g is not None else "ffn",
    )(*args)


def _proj_kernel(x_ref, pos_ref, mixg_ref, win_ref, qg_ref, wuqt_ref, kvg_ref, wuk_ref, wuvt_ref,
                 qhgt_ref, khg_ref, invft_ref, wg_ref, wu_ref,
                 u_ref, qt_ref, k_ref, vt_ref, wgo_ref, wuo_ref, *, sub):
    _cast_cols_kernel(wg_ref, wgo_ref)
    _cast_cols_kernel(wu_ref, wuo_ref)

    c0 = 2 * CONV_CH
    c1 = c0 + Q_RANK
    c2 = c1 + KV_RANK
    half = ROPE // 2
    inv_d = 1.0 / QK_DIM
    scale = math.log2(math.e) / math.sqrt(QK_DIM)
    qhgt = qhgt_ref[...]
    khg = khg_ref[...]
    lane = lax.broadcasted_iota(jnp.int32, (sub, LANES), 1)
    one_col = jnp.where(lane == ROPE, 1.0, 0.0)
    ones_rows = jnp.where(
        lax.broadcasted_iota(jnp.int32, (V_PAD - V_DIM, sub), 0) == 0, 1.0, 0.0).astype(BF16)
    q_tail = jnp.zeros((HEAD_PAD - QK_DIM, sub), BF16)

    def part(r0):
        rs = slice(r0, r0 + sub)
        hn = _rms(x_ref[rs, :], mixg_ref[...]).astype(BF16)
        z = jnp.dot(hn, win_ref[...], preferred_element_type=F32)
        u_ref[rs, :] = z[:, :CONV_CH] * jax.nn.sigmoid(z[:, CONV_CH:c0])
        qnt = _rms(z[:, c0:c1], qg_ref[...]).T.astype(BF16)
        kvn = _rms(z[:, c1:c2], kvg_ref[...])
        kvnt = kvn.T.astype(BF16)
        kpe = z[:, c2:c2 + LANES]
        qt = jnp.dot(wuqt_ref[...], qnt, preferred_element_type=F32)
        vt = jnp.dot(wuvt_ref[...], kvnt, preferred_element_type=F32)
        kn = jnp.dot(kvn.astype(BF16), wuk_ref[...], preferred_element_type=F32)

        ang = invft_ref[...] * pos_ref[:, rs].astype(F32)
        cos_t = jnp.cos(ang)
        sin_t = jnp.sin(ang)
        zeros = jnp.zeros((LANES - ROPE, sub), F32)
        cos_k = jnp.concatenate([cos_t, cos_t, zeros], axis=0).T
        sin_k = jnp.concatenate([-sin_t, sin_t, zeros], axis=0).T

        kpe_g = kpe * khg[:, NOPE:]
        kpe_rot = kpe_g * cos_k + jnp.where(lane < half, pltpu.roll(kpe_g, LANES - half, 1),
                                            pltpu.roll(kpe_g, half, 1)) * sin_k
        kpe_ss = jnp.sum(kpe * kpe, axis=-1, keepdims=True)
        for h in range(N_HEADS):
            qh = qt[h * HEAD_PAD:h * HEAD_PAD + QK_DIM, :]
            r = lax.rsqrt(jnp.sum(qh * qh, axis=0, keepdims=True) * inv_d + EPS) * scale
            qh = qh * qhgt[:QK_DIM, :] * r
            x1 = qh[NOPE:NOPE + half, :]
            x2 = qh[NOPE + half:, :]
            qt_ref[h, :NOPE, rs] = qh[:NOPE, :].astype(BF16)
            qt_ref[h, NOPE:NOPE + half, rs] = (x1 * cos_t - x2 * sin_t).astype(BF16)
            qt_ref[h, NOPE + half:QK_DIM, rs] = (x2 * cos_t + x1 * sin_t).astype(BF16)
            qt_ref[h, QK_DIM:, rs] = q_tail
            ka = kn[:, h * NOPE:(h + 1) * NOPE]
            rk = lax.rsqrt((jnp.sum(ka * ka, axis=-1, keepdims=True) + kpe_ss) * inv_d + EPS)
            k_ref[h, rs, :NOPE] = (ka * rk * khg[:, :NOPE]).astype(BF16)
            k_ref[h, rs, NOPE:] = (kpe_rot * rk + one_col).astype(BF16)
            vt_ref[h, :V_DIM, rs] = vt[h * V_DIM:(h + 1) * V_DIM, :].astype(BF16)
            vt_ref[h, V_DIM:, rs] = ones_rows

    for r0 in range(0, x_ref.shape[0], sub):
        part(r0)


def _proj(x1, pos, mixg, win, qg, wuqt, kvg, wuk, wuvt, qhg, khg, inv_freq, wg, wu, *, rows):
    s, d = x1.shape
    f = wg.shape[1]
    fp = _padded_hidden(f)
    w_rows = d // (s // rows)
    assert w_rows % SUBLANES_BF16 == 0
    sub = min(PROJ_SUB, rows)
    qhgt = jnp.broadcast_to(qhg.reshape(HEAD_PAD, 1), (HEAD_PAD, sub))
    invft = jnp.broadcast_to(inv_freq.reshape(-1, 1), (ROPE // 2, sub))
    row = lambda w: pl.BlockSpec((rows, w), lambda i: (i, 0))
    in_specs = [row(d), pl.BlockSpec((1, rows), lambda i: (0, i)), _resident(mixg.shape),
                _resident(win.shape), _resident(qg.shape), _resident(wuqt.shape),
                _resident(kvg.shape), _resident(wuk.shape), _resident(wuvt.shape),
                _resident(qhgt.shape), _resident(khg.shape), _resident(invft.shape),
                pl.BlockSpec((w_rows, f), lambda i: (i, 0)),
                pl.BlockSpec((w_rows, f), lambda i: (i, 0))]
    out_shape = [
        jax.ShapeDtypeStruct((s, CONV_CH), F32),
        jax.ShapeDtypeStruct((N_HEADS, HEAD_PAD, s), BF16),
        jax.ShapeDtypeStruct((N_HEADS, s, HEAD_PAD), BF16),
        jax.ShapeDtypeStruct((N_HEADS, V_PAD, s), BF16),
        jax.ShapeDtypeStruct((fp // FF_TILE, d, FF_TILE), BF16),
        jax.ShapeDtypeStruct((fp // FF_TILE, d, FF_TILE), BF16),
    ]
    out_specs = [
        row(CONV_CH),
        pl.BlockSpec((N_HEADS, HEAD_PAD, rows), lambda i: (0, 0, i)),
        pl.BlockSpec((N_HEADS, rows, HEAD_PAD), lambda i: (0, i, 0)),
        pl.BlockSpec((N_HEADS, V_PAD, rows), lambda i: (0, 0, i)),
        pl.BlockSpec((fp // FF_TILE, w_rows, FF_TILE), lambda i: (0, i, 0)),
        pl.BlockSpec((fp // FF_TILE, w_rows, FF_TILE), lambda i: (0, i, 0)),
    ]
    return pl.pallas_call(
        functools.partial(_proj_kernel, sub=sub),
        grid=(s // rows,),
        in_specs=in_specs,
        out_specs=out_specs,
        out_shape=out_shape,
        compiler_params=_params(("parallel",)),
        name="proj",
    )(x1, pos, mixg, win, qg, wuqt, kvg, wuk, wuvt, qhgt, khg, invft, wg, wu)


def _attn_kernel(qt_ref, k_ref, vt_ref, o_ref, qa_ref, p0_ref, p1_ref, r_ref, pm_ref, a_ref,
                 bad_ref, acc_ref, *, tk):
    nk = k_ref.shape[0] // tk
    tq = qt_ref.shape[1]
    p_refs = (p0_ref, p1_ref)
    first_row = lax.broadcasted_iota(jnp.int32, (SUBLANES_BF16, tq), 0) == 0

    def set_reference(r):
        qa_ref[QK_DIM:QK_DIM + SUBLANES_BF16, :] = jnp.where(first_row, -r, 0.0).astype(BF16)

    def scores(c):
        off = pl.multiple_of(c * tk, tk)
        return jnp.dot(k_ref[pl.ds(off, tk), :], qa_ref[...], preferred_element_type=F32)

    def values(c, slot, alpha):
        off = pl.multiple_of(c * tk, tk)
        pv = jnp.dot(vt_ref[:, pl.ds(off, tk)], p_refs[slot][...], preferred_element_type=F32)
        acc_ref[...] = alpha * acc_ref[...] + pv

    def emit(p, slot):
        pb = p.astype(BF16)
        p_refs[slot][...] = pb
        return jnp.max(pb, axis=0, keepdims=True).astype(F32)

    def step(c, slot):
        r_prev = r_ref[...]
        pm_prev = pm_ref[...]
        r = jnp.maximum(r_prev, r_prev + jnp.log2(pm_prev)).astype(BF16).astype(F32)
        alpha = jnp.exp2(r_prev - r)
        set_reference(r)
        pm_ref[...] = emit(jnp.exp2(scores(c)), slot)
        values(c - 1, 1 - slot, a_ref[...])
        bad_ref[...] = jnp.maximum(bad_ref[...], jnp.where(pm_prev <= 2.0 ** MAX_JUMP, 0.0, 1.0))
        r_ref[...] = r
        a_ref[...] = alpha

    qa_ref[...] = qt_ref[...]
    pm_ref[...] = emit(jnp.exp2(scores(0)), 0)
    r_ref[...] = jnp.zeros_like(r_ref)
    a_ref[...] = jnp.zeros_like(a_ref)
    bad_ref[...] = jnp.zeros_like(bad_ref)
    acc_ref[...] = jnp.zeros_like(acc_ref)

    def body(t, carry):
        c = 2 * t + 1
        step(c, 1)
        step(c + 1, 0)
        return carry

    lax.fori_loop(0, (nk - 2) // 2, body, 0)
    step(nk - 1, 1)
    values(nk - 1, 1, a_ref[...])
    bad = jnp.maximum(bad_ref[...], jnp.where(pm_ref[...] <= 2.0 ** MAX_JUMP, 0.0, 1.0))
    bad = jnp.maximum(bad, jnp.where(acc_ref[V_DIM:V_DIM + 1, :] >= 2.0 ** -MAX_JUMP, 0.0, 1.0))

    @pl.when(jnp.max(bad) > 0.0)
    def _():
        set_reference(jnp.zeros_like(bad))
        r_ref[...] = jnp.full(r_ref.shape, -jnp.inf, F32)
        acc_ref[...] = jnp.zeros_like(acc_ref)

        def exact(c, carry):
            st = scores(c)
            m_prev = r_ref[...]
            m_new = jnp.maximum(m_prev, jnp.max(st, axis=0, keepdims=True))
            p0_ref[...] = jnp.exp2(st - m_new).astype(BF16)
            values(c, 0, jnp.exp2(m_prev - m_new))
            r_ref[...] = m_new
            return carry

        lax.fori_loop(0, nk, exact, 0)

    o = acc_ref[:V_DIM, :] * (1.0 / acc_ref[V_DIM:V_DIM + 1, :])
    o_ref[...] = o.T.astype(o_ref.dtype)


def _attn(qt, k, vt, *, tq, tk):
    h, s, dp = k.shape
    assert (s // tk) % 2 == 0
    stat = pltpu.VMEM((1, tq), F32)
    return pl.pallas_call(
        functools.partial(_attn_kernel, tk=tk),
        grid=(h, s // tq),
        in_specs=[
            pl.BlockSpec((None, dp, tq), lambda hh, i: (hh, 0, i)),
            pl.BlockSpec((None, s, dp), lambda hh, i: (hh, 0, 0)),
            pl.BlockSpec((None, V_PAD, s), lambda hh, i: (hh, 0, 0)),
        ],
        out_specs=pl.BlockSpec((tq, V_DIM), lambda hh, i: (i, hh)),
        out_shape=jax.ShapeDtypeStruct((s, h * V_DIM), BF16),
        scratch_shapes=[pltpu.VMEM((dp, tq), BF16),
                        pltpu.VMEM((tk, tq), BF16), pltpu.VMEM((tk, tq), BF16),
                        stat, stat, stat, stat, pltpu.VMEM((V_PAD, tq), F32)],
        compiler_params=_params(("parallel", "arbitrary")),
        name="attn",
    )(qt, k, vt)


def _conv_kernel(prev_ref, u_ref, next_ref, w_ref, b_ref, g_ref, o_ref, sh_ref, y_ref, sq_ref, *,
                 sub):
    i = pl.program_id(0)
    rows = u_ref.shape[0]
    blocks = CONV_CH // LANES
    has_prev = i > 0
    has_next = i < pl.num_programs(0) - 1
    for c in range(blocks):
        cs = slice(c * LANES, (c + 1) * LANES)
        sh_ref[c, 0, :HALO, :] = jnp.where(has_prev, prev_ref[:, cs], 0.0)
        sh_ref[c, 0, HALO:HALO + rows, :] = u_ref[:, cs]
        sh_ref[c, 0, HALO + rows:, :] = jnp.where(has_next, next_ref[:, cs], 0.0)
    n = rows + 2 * HALO - SUBLANES
    base = HALO - CONV_PAD

    sq_ref[...] = jnp.zeros_like(sq_ref)

    def lane_block(ci, carry):
        cs = pl.ds(pl.multiple_of(ci * LANES, LANES), LANES)
        for b in range(1, SUBLANES):
            sh_ref[ci, b, :n, :] = sh_ref[ci, 0, b:b + n, :]
        for r0 in range(0, rows, sub):
            acc = jnp.broadcast_to(b_ref[:, cs], (sub, LANES))
            for t in range(CONV_WIDTH):
                a, b = divmod(base + t, SUBLANES)
                lo = r0 + a * SUBLANES
                acc = acc + sh_ref[ci, b, lo:lo + sub, :] * w_ref[t:t + 1, cs]
            y_ref[ci, r0:r0 + sub, :] = acc
            sq_ref[r0:r0 + sub, :] += acc * acc
        return carry

    lax.fori_loop(0, blocks, lane_block, 0)
    ssq = jnp.sum(sq_ref[...], axis=-1, keepdims=True)
    inv = lax.rsqrt(ssq * (1.0 / CONV_CH) + EPS)
    for c in range(blocks):
        cs = slice(c * LANES, (c + 1) * LANES)
        y = y_ref[c] * inv * g_ref[:, cs]
        o_ref[:, cs] = (y * jax.nn.sigmoid(y)).astype(o_ref.dtype)


def _conv(u, w, b, g, *, rows, sub):
    s, c = u.shape
    hb = rows // HALO
    last = s // HALO - 1
    return pl.pallas_call(
        functools.partial(_conv_kernel, sub=sub),
        grid=(s // rows,),
        in_specs=[
            pl.BlockSpec((HALO, c), lambda i: (jnp.maximum(i * hb - 1, 0), 0)),
            pl.BlockSpec((rows, c), lambda i: (i, 0)),
            pl.BlockSpec((HALO, c), lambda i: (jnp.minimum((i + 1) * hb, last), 0)),
            _resident(w.shape), _resident(b.shape), _resident(g.shape),
        ],
        out_specs=pl.BlockSpec((rows, c), lambda i: (i, 0)),
        out_shape=jax.ShapeDtypeStruct((s, c), BF16),
        scratch_shapes=[pltpu.VMEM((c // LANES, SUBLANES, rows + 2 * HALO, LANES), F32),
                        pltpu.VMEM((c // LANES, rows, LANES), F32), pltpu.VMEM((rows, LANES), F32)],
        compiler_params=_params(("parallel",)),
        name="conv",
    )(u, u, u, w, b, g)


def _outproj_kernel(x_ref, a_ref, b_ref, wa_ref, wb_ref, wd_ref, o_ref, wdo_ref):
    _cast_rows_kernel(wd_ref, wdo_ref)
    o_ref[...] = (x_ref[...]
                  + jnp.dot(a_ref[...], wa_ref[...], preferred_element_type=F32)
                  + jnp.dot(b_ref[...], wb_ref[...], preferred_element_type=F32))


def _outproj(x1, a, b, wa, wb, wd, *, rows):
    s, d = x1.shape
    steps = s // rows
    f, dw = wd.shape
    fp = _padded_hidden(f)
    strip = max(LANES, dw // steps)
    per_strip = steps // (dw // strip)
    assert steps == per_strip * (dw // strip)
    row = lambda w: pl.BlockSpec((rows, w), lambda i: (i, 0))
    return pl.pallas_call(
        _outproj_kernel,
        grid=(steps,),
        in_specs=[row(d), row(a.shape[1]), row(b.shape[1]), _resident(wa.shape), _resident(wb.shape),
                  pl.BlockSpec((f, strip), lambda i: (0, i // per_strip))],
        out_specs=[row(d), pl.BlockSpec((fp, strip), lambda i: (0, i // per_strip))],
        out_shape=[jax.ShapeDtypeStruct((s, d), F32), jax.ShapeDtypeStruct((fp, dw), BF16)],
        compiler_params=_params(("arbitrary",)),
        name="outproj",
    )(x1, a, b, wa, wb, wd)


def _pad_cols(w, n):
    return jnp.pad(w, ((0, 0), (0, n - w.shape[1])))


def _cast_cols_kernel(w_ref, o_ref):
    nb, rows, width = o_ref.shape
    pad = nb * width - w_ref.shape[1]
    for j in range(nb):
        lo = j * width - pad
        if lo + width <= 0:
            o_ref[j] = jnp.zeros((rows, width), BF16)
        elif lo < 0:
            o_ref[j, :, :-lo] = jnp.zeros((rows, -lo), BF16)
            o_ref[j, :, -lo:] = w_ref[:, :lo + width].astype(BF16)
        else:
            o_ref[j] = w_ref[:, lo:lo + width].astype(BF16)


def _cast_pad_cols(w, n_out):
    r, n = w.shape
    rows = min(CAST_ROWS, r)
    nb = n_out // FF_TILE
    return pl.pallas_call(
        _cast_cols_kernel,
        grid=(r // rows,),
        in_specs=[pl.BlockSpec((rows, n), lambda i: (i, 0))],
        out_specs=pl.BlockSpec((nb, rows, FF_TILE), lambda i: (0, i, 0)),
        out_shape=jax.ShapeDtypeStruct((nb, r, FF_TILE), BF16),
        compiler_params=_params(("parallel",)),
        name="cast_cols",
    )(w)


def _cast_rows_kernel(w_ref, o_ref):
    pad = o_ref.shape[0] - w_ref.shape[0]
    o_ref[:pad, :] = jnp.zeros((pad, o_ref.shape[1]), BF16)
    o_ref[pad:, :] = w_ref[...].astype(BF16)


def _cast_pad_rows(w, n_out):
    n, c = w.shape
    cols = min(CAST_COLS, c)
    return pl.pallas_call(
        _cast_rows_kernel,
        grid=(c // cols,),
        in_specs=[pl.BlockSpec((n, cols), lambda i: (0, i))],
        out_specs=pl.BlockSpec((n_out, cols), lambda i: (0, i)),
        out_shape=jax.ShapeDtypeStruct((n_out, c), BF16),
        compiler_params=_params(("parallel",)),
        name="cast_rows",
    )(w)


def _padded_hidden(f):
    assert f % LANES == 0
    return -(-f // FF_TILE) * FF_TILE


def _ffn_weights(wg, wu, wd):
    fp = _padded_hidden(wg.shape[1])
    return _cast_pad_cols(wg, fp), _cast_pad_cols(wu, fp), _cast_pad_rows(wd, fp)


def _head_gain(g):
    return jnp.pad(g, (0, HEAD_PAD - QK_DIM)).reshape(1, HEAD_PAD)


def _layer(x, positions, tiles, ffn1_norm, ffn1_w_gate, ffn1_w_up, ffn1_w_down, mix_norm, w_in,
           q_norm, w_uq, kv_norm, w_ukv, q_head_norm, k_head_norm, dw_kernel, dw_bias, conv_norm,
           w_out, ffn2_norm, ffn2_w_gate, ffn2_w_up, ffn2_w_down, final_norm):
    s, d = x.shape
    row = lambda g: g.reshape(1, -1)

    hidden = ffn1_w_gate.shape[1]
    x1 = _ffn(x, row(ffn1_norm), *_ffn_weights(ffn1_w_gate, ffn1_w_up, ffn1_w_down), None,
              rows=tiles["ffn"], hidden=hidden)

    c2 = 2 * CONV_CH + Q_RANK + KV_RANK
    win = _pad_cols(w_in.astype(BF16), c2 + LANES)
    wuq = jnp.pad(w_uq.astype(BF16).reshape(Q_RANK, N_HEADS, QK_DIM),
                  ((0, 0), (0, 0), (0, HEAD_PAD - QK_DIM))).reshape(Q_RANK, N_HEADS * HEAD_PAD)
    wkv = w_ukv.astype(BF16).reshape(KV_RANK, N_HEADS, NOPE + V_DIM)
    wuk = wkv[:, :, :NOPE].reshape(KV_RANK, N_HEADS * NOPE)
    wuv = wkv[:, :, NOPE:].reshape(KV_RANK, N_HEADS * V_DIM)
    inv_freq = ROPE_THETA ** (-(jnp.arange(ROPE // 2, dtype=F32) * 2.0 / ROPE))
    u, qt, k, vt, wg2, wu2 = _proj(
        x1, positions.reshape(1, s), row(mix_norm), win, row(q_norm), wuq.T, row(kv_norm), wuk,
        wuv.T, _head_gain(q_head_norm), _head_gain(k_head_norm), inv_freq, ffn2_w_gate, ffn2_w_up,
        rows=tiles["proj"])

    b_out = _attn(qt, k, vt, tq=tiles["attn_q"], tk=tiles["attn_k"])
    a_out = _conv(u, jnp.pad(dw_kernel, ((0, 1), (0, 0))), row(dw_bias), row(conv_norm),
                  rows=tiles["conv"], sub=tiles["conv_sub"])
    wo = w_out.astype(BF16)
    x2, wd2 = _outproj(x1, a_out, b_out, wo[:CONV_CH], wo[CONV_CH:], ffn2_w_down, rows=tiles["out"])

    return _ffn(x2, row(ffn2_norm), wg2, wu2, wd2, row(final_norm), rows=tiles["ffn"],
                hidden=hidden)


def _tiles(s):
    return {
        "ffn": min(FFN_ROWS, s), "proj": min(PROJ_ROWS, s), "attn_q": min(ATTN_Q, s),
        "attn_k": min(ATTN_K, s), "conv": min(CONV_ROWS, s), "conv_sub": CONV_SUB,
        "out": min(OUT_ROWS, s),
    }


def kernel(x, positions, ffn1_norm, ffn1_w_gate, ffn1_w_up, ffn1_w_down, mix_norm, w_in, q_norm, w_uq, kv_norm, w_ukv, q_head_norm, k_head_norm, dw_kernel, dw_bias, conv_norm, w_out, ffn2_norm, ffn2_w_gate, ffn2_w_up, ffn2_w_down, final_norm):
    batch, s, d = x.shape
    assert ffn1_norm.shape[0] == 1, "single-layer stack expected"
    outs = []
    for bi in range(batch):
        outs.append(_layer(
            x[bi], positions[bi], _tiles(s), ffn1_norm[0], ffn1_w_gate[0], ffn1_w_up[0],
            ffn1_w_down[0], mix_norm[0], w_in[0], q_norm[0], w_uq[0], kv_norm[0], w_ukv[0],
            q_head_norm[0], k_head_norm[0], dw_kernel[0], dw_bias[0], conv_norm[0], w_out[0],
            ffn2_norm[0], ffn2_w_gate[0], ffn2_w_up[0], ffn2_w_down[0], final_norm[0]))
    return jnp.stack(outs)
```

```python
import functools
import math

import jax
import jax.numpy as jnp
from jax import lax
from jax.experimental import pallas as pl
from jax.experimental.pallas import tpu as pltpu

F32 = jnp.float32
BF16 = jnp.bfloat16

EPS = 1e-6
ROPE_THETA = 10000.0
CONV_CH = 1024
CONV_WIDTH = 31
CONV_PAD = CONV_WIDTH // 2
N_HEADS = 8
NOPE = 128
ROPE = 64
QK_DIM = NOPE + ROPE
V_DIM = 128
Q_RANK = 512
KV_RANK = 256

LANES = 128
SUBLANES = 8
SUBLANES_BF16 = 16
HEAD_PAD = 2 * LANES
V_PAD = V_DIM + SUBLANES_BF16
VMEM_LIMIT = 56 * 1024 * 1024

FF_TILE = 1024
FF_UNIT = 256
FFN_ROWS = 512
PROJ_ROWS = 512
PROJ_SUB = 256
ATTN_Q = 2048
ATTN_K = 1024
CONV_ROWS = 256
CONV_SUB = 256
OUT_ROWS = 512
OUT_COLS = 512
CAST_ROWS = 256
CAST_COLS = 256
HALO = 16
MAX_JUMP = 64.0


def _rms(x, g):
    ms = jnp.mean(x * x, axis=-1, keepdims=True)
    return x * lax.rsqrt(ms + EPS) * g


def _params(sem):
    return pltpu.CompilerParams(dimension_semantics=sem, vmem_limit_bytes=VMEM_LIMIT)


def _resident(shape):
    nd = len(shape)
    return pl.BlockSpec(shape, lambda *_: (0,) * nd, pipeline_mode=pl.Buffered(1))


def _ffn_kernel(*refs, final_norm, skip):
    if final_norm:
        x_ref, g_ref, wg_ref, wu_ref, wd_ref, fg_ref, o_ref, xn_ref = refs
    else:
        x_ref, g_ref, wg_ref, wu_ref, wd_ref, o_ref, xn_ref = refs
    f = pl.program_id(1)
    tf = wg_ref.shape[1]

    def hidden_tiles(start):
        xn = xn_ref[...]
        for c in range(start, tf, FF_UNIT):
            gate = jnp.dot(xn, wg_ref[:, c:c + FF_UNIT], preferred_element_type=F32)
            up = jnp.dot(xn, wu_ref[:, c:c + FF_UNIT], preferred_element_type=F32)
            h = (gate * jax.nn.sigmoid(gate) * up).astype(BF16)
            o_ref[...] += jnp.dot(h, wd_ref[c:c + FF_UNIT, :], preferred_element_type=F32)

    @pl.when(f == 0)
    def _():
        xn_ref[...] = _rms(x_ref[...], g_ref[...]).astype(BF16)
        o_ref[...] = jnp.zeros_like(o_ref)
        hidden_tiles(skip)

    @pl.when(f > 0)
    def _():
        hidden_tiles(0)

    @pl.when(f == pl.num_programs(1) - 1)
    def _():
        y = x_ref[...] + 0.5 * o_ref[...]
        if final_norm:
            y = _rms(y, fg_ref[...])
        o_ref[...] = y


def _ffn(x, g, wg, wu, wd, final_g, *, rows, hidden):
    s, d = x.shape
    fp = wd.shape[0]
    tf = FF_TILE
    skip = (fp - hidden) // FF_UNIT * FF_UNIT
    assert fp % tf == 0 and 0 <= skip < tf
    in_specs = [
        pl.BlockSpec((rows, d), lambda i, f: (i, 0)),
        pl.BlockSpec((1, d), lambda i, f: (0, 0)),
        pl.BlockSpec((None, d, tf), lambda i, f: (f, 0, 0)),
        pl.BlockSpec((None, d, tf), lambda i, f: (f, 0, 0)),
        pl.BlockSpec((tf, d), lambda i, f: (f, 0)),
    ]
    args = [x, g, wg, wu, wd]
    if final_g is not None:
        in_specs.append(pl.BlockSpec((1, d), lambda i, f: (0, 0)))
        args.append(final_g)
    return pl.pallas_call(
        functools.partial(_ffn_kernel, final_norm=final_g is not None, skip=skip),
        grid=(s // rows, fp // tf),
        in_specs=in_specs,
        out_specs=pl.BlockSpec((rows, d), lambda i, f: (i, 0)),
        out_shape=jax.ShapeDtypeStruct((s, d), F32),
        scratch_shapes=[pltpu.VMEM((rows, d), BF16)],
        compiler_params=_params(("parallel", "arbitrary")),
        name="ffn_final" if final_g is not None else "ffn",
    )(*args)


def _proj_kernel(x_ref, pos_ref, mixg_ref, win_ref, qg_ref, wuqt_ref, kvg_ref, wuk_ref, wuvt_ref,
                 qhgt_ref, khg_ref, invft_ref, wg_ref, wu_ref,
                 u_ref, qt_ref, k_ref, vt_ref, wgo_ref, wuo_ref, *, sub):
    _cast_cols_kernel(wg_ref, wgo_ref)
    _cast_cols_kernel(wu_ref, wuo_ref)

    c0 = 2 * CONV_CH
    c1 = c0 + Q_RANK
    c2 = c1 + KV_RANK
    half = ROPE // 2
    inv_d = 1.0 / QK_DIM
    scale = math.log2(math.e) / math.sqrt(QK_DIM)
    qhgt = qhgt_ref[...]
    khg = khg_ref[...]
    lane = lax.broadcasted_iota(jnp.int32, (sub, LANES), 1)
    one_col = jnp.where(lane == ROPE, 1.0, 0.0)
    ones_rows = jnp.where(
        lax.broadcasted_iota(jnp.int32, (V_PAD - V_DIM, sub), 0) == 0, 1.0, 0.0).astype(BF16)
    q_tail = jnp.zeros((HEAD_PAD - QK_DIM, sub), BF16)

    def part(r0):
        rs = slice(r0, r0 + sub)
        hn = _rms(x_ref[rs, :], mixg_ref[...]).astype(BF16)
        z = jnp.dot(hn, win_ref[...], preferred_element_type=F32)
        u_ref[rs, :] = z[:, :CONV_CH] * jax.nn.sigmoid(z[:, CONV_CH:c0])
        qnt = _rms(z[:, c0:c1], qg_ref[...]).T.astype(BF16)
        kvn = _rms(z[:, c1:c2], kvg_ref[...])
        kvnt = kvn.T.astype(BF16)
        kpe = z[:, c2:c2 + LANES]
        qt = jnp.dot(wuqt_ref[...], qnt, preferred_element_type=F32)
        vt = jnp.dot(wuvt_ref[...], kvnt, preferred_element_type=F32)
        kn = jnp.dot(kvn.astype(BF16), wuk_ref[...], preferred_element_type=F32)

        ang = invft_ref[...] * pos_ref[:, rs].astype(F32)
        cos_t = jnp.cos(ang)
        sin_t = jnp.sin(ang)
        zeros = jnp.zeros((LANES - ROPE, sub), F32)
        cos_k = jnp.concatenate([cos_t, cos_t, zeros], axis=0).T
        sin_k = jnp.concatenate([-sin_t, sin_t, zeros], axis=0).T

        kpe_g = kpe * khg[:, NOPE:]
        kpe_rot = kpe_g * cos_k + jnp.where(lane < half, pltpu.roll(kpe_g, LANES - half, 1),
                                            pltpu.roll(kpe_g, half, 1)) * sin_k
        kpe_ss = jnp.sum(kpe * kpe, axis=-1, keepdims=True)
        for h in range(N_HEADS):
            qh = qt[h * HEAD_PAD:h * HEAD_PAD + QK_DIM, :]
            r = lax.rsqrt(jnp.sum(qh * qh, axis=0, keepdims=True) * inv_d + EPS) * scale
            qh = qh * qhgt[:QK_DIM, :] * r
            x1 = qh[NOPE:NOPE + half, :]
            x2 = qh[NOPE + half:, :]
            qt_ref[h, :NOPE, rs] = qh[:NOPE, :].astype(BF16)
            qt_ref[h, NOPE:NOPE + half, rs] = (x1 * cos_t - x2 * sin_t).astype(BF16)
            qt_ref[h, NOPE + half:QK_DIM, rs] = (x2 * cos_t + x1 * sin_t).astype(BF16)
            qt_ref[h, QK_DIM:, rs] = q_tail
            ka = kn[:, h * NOPE:(h + 1) * NOPE]
            rk = lax.rsqrt((jnp.sum(ka * ka, axis=-1, keepdims=True) + kpe_ss) * inv_d + EPS)
            k_ref[h, rs, :NOPE] = (ka * rk * khg[:, :NOPE]).astype(BF16)
            k_ref[h, rs, NOPE:] = (kpe_rot * rk + one_col).astype(BF16)
            vt_ref[h, :V_DIM, rs] = vt[h * V_DIM:(h + 1) * V_DIM, :].astype(BF16)
            vt_ref[h, V_DIM:, rs] = ones_rows

    for r0 in range(0, x_ref.shape[0], sub):
        part(r0)


def _proj(x1, pos, mixg, win, qg, wuqt, kvg, wuk, wuvt, qhg, khg, inv_freq, wg, wu, *, rows):
    s, d = x1.shape
    f = wg.shape[1]
    fp = _padded_hidden(f)
    w_rows = d // (s // rows)
    assert w_rows % SUBLANES_BF16 == 0
    sub = min(PROJ_SUB, rows)
    qhgt = jnp.broadcast_to(qhg.reshape(HEAD_PAD, 1), (HEAD_PAD, sub))
    invft = jnp.broadcast_to(inv_freq.reshape(-1, 1), (ROPE // 2, sub))
    row = lambda w: pl.BlockSpec((rows, w), lambda i: (i, 0))
    in_specs = [row(d), pl.BlockSpec((1, rows), lambda i: (0, i)), _resident(mixg.shape),
                _resident(win.shape), _resident(qg.shape), _resident(wuqt.shape),
                _resident(kvg.shape), _resident(wuk.shape), _resident(wuvt.shape),
                _resident(qhgt.shape), _resident(khg.shape), _resident(invft.shape),
                pl.BlockSpec((w_rows, f), lambda i: (i, 0)),
                pl.BlockSpec((w_rows, f), lambda i: (i, 0))]
    out_shape = [
        jax.ShapeDtypeStruct((s, CONV_CH), F32),
        jax.ShapeDtypeStruct((N_HEADS, HEAD_PAD, s), BF16),
        jax.ShapeDtypeStruct((N_HEADS, s, HEAD_PAD), BF16),
        jax.ShapeDtypeStruct((N_HEADS, V_PAD, s), BF16),
        jax.ShapeDtypeStruct((fp // FF_TILE, d, FF_TILE), BF16),
        jax.ShapeDtypeStruct((fp // FF_TILE, d, FF_TILE), BF16),
    ]
    out_specs = [
        row(CONV_CH),
        pl.BlockSpec((N_HEADS, HEAD_PAD, rows), lambda i: (0, 0, i)),
        pl.BlockSpec((N_HEADS, rows, HEAD_PAD), lambda i: (0, i, 0)),
        pl.BlockSpec((N_HEADS, V_PAD, rows), lambda i: (0, 0, i)),
        pl.BlockSpec((fp // FF_TILE, w_rows, FF_TILE), lambda i: (0, i, 0)),
        pl.BlockSpec((fp // FF_TILE, w_rows, FF_TILE), lambda i: (0, i, 0)),
    ]
    return pl.pallas_call(
        functools.partial(_proj_kernel, sub=sub),
        grid=(s // rows,),
        in_specs=in_specs,
        out_specs=out_specs,
        out_shape=out_shape,
        compiler_params=_params(("parallel",)),
        name="proj",
    )(x1, pos, mixg, win, qg, wuqt, kvg, wuk, wuvt, qhgt, khg, invft, wg, wu)


def _attn_kernel(qt_ref, k_ref, vt_ref, o_ref, qa_ref, p0_ref, p1_ref, r_ref, pm_ref, a_ref,
                 bad_ref, acc_ref, *, tk):
    nk = k_ref.shape[0] // tk
    tq = qt_ref.shape[1]
    p_refs = (p0_ref, p1_ref)
    first_row = lax.broadcasted_iota(jnp.int32, (SUBLANES_BF16, tq), 0) == 0

    def set_reference(r):
        qa_ref[QK_DIM:QK_DIM + SUBLANES_BF16, :] = jnp.where(first_row, -r, 0.0).astype(BF16)

    def scores(c):
        off = pl.multiple_of(c * tk, tk)
        return jnp.dot(k_ref[pl.ds(off, tk), :], qa_ref[...], preferred_element_type=F32)

    def values(c, slot, alpha):
        off = pl.multiple_of(c * tk, tk)
        pv = jnp.dot(vt_ref[:, pl.ds(off, tk)], p_refs[slot][...], preferred_element_type=F32)
        acc_ref[...] = alpha * acc_ref[...] + pv

    def emit(p, slot):
        pb = p.astype(BF16)
        p_refs[slot][...] = pb
        return jnp.max(pb, axis=0, keepdims=True).astype(F32)

    def step(c, slot):
        r_prev = r_ref[...]
        pm_prev = pm_ref[...]
        r = jnp.maximum(r_prev, r_prev + jnp.log2(pm_prev)).astype(BF16).astype(F32)
        alpha = jnp.exp2(r_prev - r)
        set_reference(r)
        pm_ref[...] = emit(jnp.exp2(scores(c)), slot)
        values(c - 1, 1 - slot, a_ref[...])
        bad_ref[...] = jnp.maximum(bad_ref[...], jnp.where(pm_prev <= 2.0 ** MAX_JUMP, 0.0, 1.0))
        r_ref[...] = r
        a_ref[...] = alpha

    qa_ref[...] = qt_ref[...]
    pm_ref[...] = emit(jnp.exp2(scores(0)), 0)
    r_ref[...] = jnp.zeros_like(r_ref)
    a_ref[...] = jnp.zeros_like(a_ref)
    bad_ref[...] = jnp.zeros_like(bad_ref)
    acc_ref[...] = jnp.zeros_like(acc_ref)

    def body(t, carry):
        c = 2 * t + 1
        step(c, 1)
        step(c + 1, 0)
        return carry

    lax.fori_loop(0, (nk - 2) // 2, body, 0)
    step(nk - 1, 1)
    values(nk - 1, 1, a_ref[...])
    bad = jnp.maximum(bad_ref[...], jnp.where(pm_ref[...] <= 2.0 ** MAX_JUMP, 0.0, 1.0))
    bad = jnp.maximum(bad, jnp.where(acc_ref[V_DIM:V_DIM + 1, :] >= 2.0 ** -MAX_JUMP, 0.0, 1.0))

    @pl.when(jnp.max(bad) > 0.0)
    def _():
        set_reference(jnp.zeros_like(bad))
        r_ref[...] = jnp.full(r_ref.shape, -jnp.inf, F32)
        acc_ref[...] = jnp.zeros_like(acc_ref)

        def exact(c, carry):
            st = scores(c)
            m_prev = r_ref[...]
            m_new = jnp.maximum(m_prev, jnp.max(st, axis=0, keepdims=True))
            p0_ref[...] = jnp.exp2(st - m_new).astype(BF16)
            values(c, 0, jnp.exp2(m_prev - m_new))
            r_ref[...] = m_new
            return carry

        lax.fori_loop(0, nk, exact, 0)

    o = acc_ref[:V_DIM, :] * (1.0 / acc_ref[V_DIM:V_DIM + 1, :])
    o_ref[...] = o.T.astype(o_ref.dtype)


def _attn(qt, k, vt, *, tq, tk):
    h, s, dp = k.shape
    assert (s // tk) % 2 == 0
    stat = pltpu.VMEM((1, tq), F32)
    return pl.pallas_call(
        functools.partial(_attn_kernel, tk=tk),
        grid=(h, s // tq),
        in_specs=[
            pl.BlockSpec((None, dp, tq), lambda hh, i: (hh, 0, i)),
            pl.BlockSpec((None, s, dp), lambda hh, i: (hh, 0, 0)),
            pl.BlockSpec((None, V_PAD, s), lambda hh, i: (hh, 0, 0)),
        ],
        out_specs=pl.BlockSpec((tq, V_DIM), lambda hh, i: (i, hh)),
        out_shape=jax.ShapeDtypeStruct((s, h * V_DIM), BF16),
        scratch_shapes=[pltpu.VMEM((dp, tq), BF16),
                        pltpu.VMEM((tk, tq), BF16), pltpu.VMEM((tk, tq), BF16),
                        stat, stat, stat, stat, pltpu.VMEM((V_PAD, tq), F32)],
        compiler_params=_params(("parallel", "arbitrary")),
        name="attn",
    )(qt, k, vt)


def _conv_kernel(prev_ref, u_ref, next_ref, w_ref, b_ref, g_ref, o_ref, sh_ref, y_ref, sq_ref, *,
                 sub):
    i = pl.program_id(0)
    rows = u_ref.shape[0]
    blocks = CONV_CH // LANES
    has_prev = i > 0
    has_next = i < pl.num_programs(0) - 1
    for c in range(blocks):
        cs = slice(c * LANES, (c + 1) * LANES)
        sh_ref[c, 0, :HALO, :] = jnp.where(has_prev, prev_ref[:, cs], 0.0)
        sh_ref[c, 0, HALO:HALO + rows, :] = u_ref[:, cs]
        sh_ref[c, 0, HALO + rows:, :] = jnp.where(has_next, next_ref[:, cs], 0.0)
    n = rows + 2 * HALO - SUBLANES
    base = HALO - CONV_PAD

    sq_ref[...] = jnp.zeros_like(sq_ref)

    def lane_block(ci, carry):
        cs = pl.ds(pl.multiple_of(ci * LANES, LANES), LANES)
        for b in range(1, SUBLANES):
            sh_ref[ci, b, :n, :] = sh_ref[ci, 0, b:b + n, :]
        for r0 in range(0, rows, sub):
            acc = jnp.broadcast_to(b_ref[:, cs], (sub, LANES))
            for t in range(CONV_WIDTH):
                a, b = divmod(base + t, SUBLANES)
                lo = r0 + a * SUBLANES
                acc = acc + sh_ref[ci, b, lo:lo + sub, :] * w_ref[t:t + 1, cs]
            y_ref[ci, r0:r0 + sub, :] = acc
            sq_ref[r0:r0 + sub, :] += acc * acc
        return carry

    lax.fori_loop(0, blocks, lane_block, 0)
    ssq = jnp.sum(sq_ref[...], axis=-1, keepdims=True)
    inv = lax.rsqrt(ssq * (1.0 / CONV_CH) + EPS)
    for c in range(blocks):
        cs = slice(c * LANES, (c + 1) * LANES)
        y = y_ref[c] * inv * g_ref[:, cs]
        o_ref[:, cs] = (y * jax.nn.sigmoid(y)).astype(o_ref.dtype)


def _conv(u, w, b, g, *, rows, sub):
    s, c = u.shape
    hb = rows // HALO
    last = s // HALO - 1
    return pl.pallas_call(
        functools.partial(_conv_kernel, sub=sub),
        grid=(s // rows,),
        in_specs=[
            pl.BlockSpec((HALO, c), lambda i: (jnp.maximum(i * hb - 1, 0), 0)),
            pl.BlockSpec((rows, c), lambda i: (i, 0)),
            pl.BlockSpec((HALO, c), lambda i: (jnp.minimum((i + 1) * hb, last), 0)),
            _resident(w.shape), _resident(b.shape), _resident(g.shape),
        ],
        out_specs=pl.BlockSpec((rows, c), lambda i: (i, 0)),
        out_shape=jax.ShapeDtypeStruct((s, c), BF16),
        scratch_shapes=[pltpu.VMEM((c // LANES, SUBLANES, rows + 2 * HALO, LANES), F32),
                        pltpu.VMEM((c // LANES, rows, LANES), F32), pltpu.VMEM((rows, LANES), F32)],
        compiler_params=_params(("parallel",)),
        name="conv",
    )(u, u, u, w, b, g)


def _outproj_kernel(x_ref, a_ref, b_ref, wa_ref, wb_ref, wd_ref, o_ref, wdo_ref):
    _cast_rows_kernel(wd_ref, wdo_ref)
    a = a_ref[...]
    b = b_ref[...]
    for c in range(0, o_ref.shape[1], OUT_COLS):
        cs = slice(c, c + OUT_COLS)
        o_ref[:, cs] = (x_ref[:, cs]
                        + jnp.dot(a, wa_ref[:, cs], preferred_element_type=F32)
                        + jnp.dot(b, wb_ref[:, cs], preferred_element_type=F32))


def _outproj(x1, a, b, wa, wb, wd, *, rows):
    s, d = x1.shape
    steps = s // rows
    f, dw = wd.shape
    fp = _padded_hidden(f)
    strip = max(LANES, dw // steps)
    per_strip = steps // (dw // strip)
    assert steps == per_strip * (dw // strip)
    row = lambda w: pl.BlockSpec((rows, w), lambda i: (i, 0))
    return pl.pallas_call(
        _outproj_kernel,
        grid=(steps,),
        in_specs=[row(d), row(a.shape[1]), row(b.shape[1]), _resident(wa.shape), _resident(wb.shape),
                  pl.BlockSpec((f, strip), lambda i: (0, i // per_strip))],
        out_specs=[row(d), pl.BlockSpec((fp, strip), lambda i: (0, i // per_strip))],
        out_shape=[jax.ShapeDtypeStruct((s, d), F32), jax.ShapeDtypeStruct((fp, dw), BF16)],
        compiler_params=_params(("arbitrary",)),
        name="outproj",
    )(x1, a, b, wa, wb, wd)


def _pad_cols(w, n):
    return jnp.pad(w, ((0, 0), (0, n - w.shape[1])))


def _cast_cols_kernel(w_ref, o_ref):
    nb, rows, width = o_ref.shape
    pad = nb * width - w_ref.shape[1]
    for j in range(nb):
        lo = j * width - pad
        if lo + width <= 0:
            o_ref[j] = jnp.zeros((rows, width), BF16)
        elif lo < 0:
            o_ref[j, :, :-lo] = jnp.zeros((rows, -lo), BF16)
            o_ref[j, :, -lo:] = w_ref[:, :lo + width].astype(BF16)
        else:
            o_ref[j] = w_ref[:, lo:lo + width].astype(BF16)


def _cast_pad_cols(w, n_out):
    r, n = w.shape
    rows = min(CAST_ROWS, r)
    nb = n_out // FF_TILE
    return pl.pallas_call(
        _cast_cols_kernel,
        grid=(r // rows,),
        in_specs=[pl.BlockSpec((rows, n), lambda i: (i, 0))],
        out_specs=pl.BlockSpec((nb, rows, FF_TILE), lambda i: (0, i, 0)),
        out_shape=jax.ShapeDtypeStruct((nb, r, FF_TILE), BF16),
        compiler_params=_params(("parallel",)),
        name="cast_cols",
    )(w)


def _cast_rows_kernel(w_ref, o_ref):
    pad = o_ref.shape[0] - w_ref.shape[0]
    o_ref[:pad, :] = jnp.zeros((pad, o_ref.shape[1]), BF16)
    o_ref[pad:, :] = w_ref[...].astype(BF16)


def _cast_pad_rows(w, n_out):
    n, c = w.shape
    cols = min(CAST_COLS, c)
    return pl.pallas_call(
        _cast_rows_kernel,
        grid=(c // cols,),
        in_specs=[pl.BlockSpec((n, cols), lambda i: (0, i))],
        out_specs=pl.BlockSpec((n_out, cols), lambda i: (0, i)),
        out_shape=jax.ShapeDtypeStruct((n_out, c), BF16),
        compiler_params=_params(("parallel",)),
        name="cast_rows",
    )(w)


def _padded_hidden(f):
    assert f % LANES == 0
    return -(-f // FF_TILE) * FF_TILE


def _ffn_weights(wg, wu, wd):
    fp = _padded_hidden(wg.shape[1])
    return _cast_pad_cols(wg, fp), _cast_pad_cols(wu, fp), _cast_pad_rows(wd, fp)


def _head_gain(g):
    return jnp.pad(g, (0, HEAD_PAD - QK_DIM)).reshape(1, HEAD_PAD)


def _layer(x, positions, tiles, ffn1_norm, ffn1_w_gate, ffn1_w_up, ffn1_w_down, mix_norm, w_in,
           q_norm, w_uq, kv_norm, w_ukv, q_head_norm, k_head_norm, dw_kernel, dw_bias, conv_norm,
           w_out, ffn2_norm, ffn2_w_gate, ffn2_w_up, ffn2_w_down, final_norm):
    s, d = x.shape
    row = lambda g: g.reshape(1, -1)

    hidden = ffn1_w_gate.shape[1]
    x1 = _ffn(x, row(ffn1_norm), *_ffn_weights(ffn1_w_gate, ffn1_w_up, ffn1_w_down), None,
              rows=tiles["ffn"], hidden=hidden)

    c2 = 2 * CONV_CH + Q_RANK + KV_RANK
    win = _pad_cols(w_in.astype(BF16), c2 + LANES)
    wuq = jnp.pad(w_uq.astype(BF16).reshape(Q_RANK, N_HEADS, QK_DIM),
                  ((0, 0), (0, 0), (0, HEAD_PAD - QK_DIM))).reshape(Q_RANK, N_HEADS * HEAD_PAD)
    wkv = w_ukv.astype(BF16).reshape(KV_RANK, N_HEADS, NOPE + V_DIM)
    wuk = wkv[:, :, :NOPE].reshape(KV_RANK, N_HEADS * NOPE)
    wuv = wkv[:, :, NOPE:].reshape(KV_RANK, N_HEADS * V_DIM)
    inv_freq = ROPE_THETA ** (-(jnp.arange(ROPE // 2, dtype=F32) * 2.0 / ROPE))
    u, qt, k, vt, wg2, wu2 = _proj(
        x1, positions.reshape(1, s), row(mix_norm), win, row(q_norm), wuq.T, row(kv_norm), wuk,
        wuv.T, _head_gain(q_head_norm), _head_gain(k_head_norm), inv_freq, ffn2_w_gate, ffn2_w_up,
        rows=tiles["proj"])

    b_out = _attn(qt, k, vt, tq=tiles["attn_q"], tk=tiles["attn_k"])
    a_out = _conv(u, jnp.pad(dw_kernel, ((0, 1), (0, 0))), row(dw_bias), row(conv_norm),
                  rows=tiles["conv"], sub=tiles["conv_sub"])
    wo = w_out.astype(BF16)
    x2, wd2 = _outproj(x1, a_out, b_out, wo[:CONV_CH], wo[CONV_CH:], ffn2_w_down, rows=tiles["out"])

    return _ffn(x2, row(ffn2_norm), wg2, wu2, wd2, row(final_norm), rows=tiles["ffn"],
                hidden=hidden)


def _tiles(s):
    return {
        "ffn": min(FFN_ROWS, s), "proj": min(PROJ_ROWS, s), "attn_q": min(ATTN_Q, s),
        "attn_k": min(ATTN_K, s), "conv": min(CONV_ROWS, s), "conv_sub": CONV_SUB,
        "out": min(OUT_ROWS, s),
    }


def kernel(x, positions, ffn1_norm, ffn1_w_gate, ffn1_w_up, ffn1_w_down, mix_norm, w_in, q_norm, w_uq, kv_norm, w_ukv, q_head_norm, k_head_norm, dw_kernel, dw_bias, conv_norm, w_out, ffn2_norm, ffn2_w_gate, ffn2_w_up, ffn2_w_down, final_norm):
    batch, s, d = x.shape
    assert ffn1_norm.shape[0] == 1, "single-layer stack expected"
    outs = []
    for bi in range(batch):
        outs.append(_layer(
            x[bi], positions[bi], _tiles(s), ffn1_norm[0], ffn1_w_gate[0], ffn1_w_up[0],
            ffn1_w_down[0], mix_norm[0], w_in[0], q_norm[0], w_uq[0], kv_norm[0], w_ukv[0],
            q_head_norm[0], k_head_norm[0], dw_kernel[0], dw_bias[0], conv_norm[0], w_out[0],
            ffn2_norm[0], ffn2_w_gate[0], ffn2_w_up[0], ffn2_w_down[0], final_norm[0]))
    return jnp.stack(outs)
```

```python
import functools
import math

import jax
import jax.numpy as jnp
from jax import lax
from jax.experimental import pallas as pl
from jax.experimental.pallas import tpu as pltpu

F32 = jnp.float32
BF16 = jnp.bfloat16

EPS = 1e-6
ROPE_THETA = 10000.0
CONV_CH = 1024
CONV_WIDTH = 31
CONV_PAD = CONV_WIDTH // 2
N_HEADS = 8
NOPE = 128
ROPE = 64
QK_DIM = NOPE + ROPE
V_DIM = 128
Q_RANK = 512
KV_RANK = 256

LANES = 128
SUBLANES = 8
SUBLANES_BF16 = 16
HEAD_PAD = 2 * LANES
V_PAD = V_DIM + SUBLANES_BF16
VMEM_LIMIT = 56 * 1024 * 1024

FF_TILE = 1024
FF_UNIT = 256
FFN_ROWS = 512
PROJ_ROWS = 512
PROJ_SUB = 256
ATTN_Q = 2048
ATTN_K = 1024
CONV_ROWS = 512
CONV_SUB = 256
OUT_ROWS = 512
CAST_ROWS = 256
CAST_COLS = 256
HALO = 16
MAX_JUMP = 64.0


def _rms(x, g):
    ms = jnp.mean(x * x, axis=-1, keepdims=True)
    return x * lax.rsqrt(ms + EPS) * g


def _params(sem):
    return pltpu.CompilerParams(dimension_semantics=sem, vmem_limit_bytes=VMEM_LIMIT)


def _resident(shape):
    nd = len(shape)
    return pl.BlockSpec(shape, lambda *_: (0,) * nd, pipeline_mode=pl.Buffered(1))


def _ffn_kernel(*refs, final_norm, skip):
    if final_norm:
        x_ref, g_ref, wg_ref, wu_ref, wd_ref, fg_ref, o_ref, xn_ref = refs
    else:
        x_ref, g_ref, wg_ref, wu_ref, wd_ref, o_ref, xn_ref = refs
    f = pl.program_id(1)
    tf = wg_ref.shape[1]

    def hidden_tiles(start):
        xn = xn_ref[...]
        for c in range(start, tf, FF_UNIT):
            gate = jnp.dot(xn, wg_ref[:, c:c + FF_UNIT], preferred_element_type=F32)
            up = jnp.dot(xn, wu_ref[:, c:c + FF_UNIT], preferred_element_type=F32)
            h = (gate * jax.nn.sigmoid(gate) * up).astype(BF16)
            o_ref[...] += jnp.dot(h, wd_ref[c:c + FF_UNIT, :], preferred_element_type=F32)

    @pl.when(f == 0)
    def _():
        xn_ref[...] = _rms(x_ref[...], g_ref[...]).astype(BF16)
        o_ref[...] = jnp.zeros_like(o_ref)
        hidden_tiles(skip)

    @pl.when(f > 0)
    def _():
        hidden_tiles(0)

    @pl.when(f == pl.num_programs(1) - 1)
    def _():
        y = x_ref[...] + 0.5 * o_ref[...]
        if final_norm:
            y = _rms(y, fg_ref[...])
        o_ref[...] = y


def _ffn(x, g, wg, wu, wd, final_g, *, rows, hidden):
    s, d = x.shape
    fp = wd.shape[0]
    tf = FF_TILE
    skip = (fp - hidden) // FF_UNIT * FF_UNIT
    assert fp % tf == 0 and 0 <= skip < tf
    in_specs = [
        pl.BlockSpec((rows, d), lambda i, f: (i, 0)),
        pl.BlockSpec((1, d), lambda i, f: (0, 0)),
        pl.BlockSpec((None, d, tf), lambda i, f: (f, 0, 0)),
        pl.BlockSpec((None, d, tf), lambda i, f: (f, 0, 0)),
        pl.BlockSpec((tf, d), lambda i, f: (f, 0)),
    ]
    args = [x, g, wg, wu, wd]
    if final_g is not None:
        in_specs.append(pl.BlockSpec((1, d), lambda i, f: (0, 0)))
        args.append(final_g)
    return pl.pallas_call(
        functools.partial(_ffn_kernel, final_norm=final_g is not None, skip=skip),
        grid=(s // rows, fp // tf),
        in_specs=in_specs,
        out_specs=pl.BlockSpec((rows, d), lambda i, f: (i, 0)),
        out_shape=jax.ShapeDtypeStruct((s, d), F32),
        scratch_shapes=[pltpu.VMEM((rows, d), BF16)],
        compiler_params=_params(("parallel", "arbitrary")),
        name="ffn_final" if final_g is not None else "ffn",
    )(*args)


def _proj_kernel(x_ref, pos_ref, mixg_ref, win_ref, qg_ref, wuqt_ref, kvg_ref, wuk_ref, wuvt_ref,
                 qhgt_ref, khg_ref, invft_ref, wg_ref, wu_ref,
                 u_ref, qt_ref, k_ref, vt_ref, wgo_ref, wuo_ref, *, sub):
    _cast_cols_kernel(wg_ref, wgo_ref)
    _cast_cols_kernel(wu_ref, wuo_ref)

    c0 = 2 * CONV_CH
    c1 = c0 + Q_RANK
    c2 = c1 + KV_RANK
    half = ROPE // 2
    inv_d = 1.0 / QK_DIM
    scale = math.log2(math.e) / math.sqrt(QK_DIM)
    qhgt = qhgt_ref[...]
    khg = khg_ref[...]
    lane = lax.broadcasted_iota(jnp.int32, (sub, LANES), 1)
    one_col = jnp.where(lane == ROPE, 1.0, 0.0)
    ones_rows = jnp.where(
        lax.broadcasted_iota(jnp.int32, (V_PAD - V_DIM, sub), 0) == 0, 1.0, 0.0).astype(BF16)
    q_tail = jnp.zeros((HEAD_PAD - QK_DIM, sub), BF16)

    def part(r0):
        rs = slice(r0, r0 + sub)
        hn = _rms(x_ref[rs, :], mixg_ref[...]).astype(BF16)
        z = jnp.dot(hn, win_ref[...], preferred_element_type=F32)
        u_ref[rs, :] = z[:, :CONV_CH] * jax.nn.sigmoid(z[:, CONV_CH:c0])
        qnt = _rms(z[:, c0:c1], qg_ref[...]).T.astype(BF16)
        kvn = _rms(z[:, c1:c2], kvg_ref[...])
        kvnt = kvn.T.astype(BF16)
        kpe = z[:, c2:c2 + LANES]
        qt = jnp.dot(wuqt_ref[...], qnt, preferred_element_type=F32)
        vt = jnp.dot(wuvt_ref[...], kvnt, preferred_element_type=F32)
        kn = jnp.dot(kvn.astype(BF16), wuk_ref[...], preferred_element_type=F32)

        ang = invft_ref[...] * pos_ref[:, rs].astype(F32)
        cos_t = jnp.cos(ang)
        sin_t = jnp.sin(ang)
        zeros = jnp.zeros((LANES - ROPE, sub), F32)
        cos_k = jnp.concatenate([cos_t, cos_t, zeros], axis=0).T
        sin_k = jnp.concatenate([-sin_t, sin_t, zeros], axis=0).T

        kpe_g = kpe * khg[:, NOPE:]
        kpe_rot = kpe_g * cos_k + jnp.where(lane < half, pltpu.roll(kpe_g, LANES - half, 1),
                                            pltpu.roll(kpe_g, half, 1)) * sin_k
        kpe_ss = jnp.sum(kpe * kpe, axis=-1, keepdims=True)
        for h in range(N_HEADS):
            qh = qt[h * HEAD_PAD:h * HEAD_PAD + QK_DIM, :]
            r = lax.rsqrt(jnp.sum(qh * qh, axis=0, keepdims=True) * inv_d + EPS) * scale
            qh = qh * qhgt[:QK_DIM, :] * r
            x1 = qh[NOPE:NOPE + half, :]
            x2 = qh[NOPE + half:, :]
            qt_ref[h, :NOPE, rs] = qh[:NOPE, :].astype(BF16)
            qt_ref[h, NOPE:NOPE + half, rs] = (x1 * cos_t - x2 * sin_t).astype(BF16)
            qt_ref[h, NOPE + half:QK_DIM, rs] = (x2 * cos_t + x1 * sin_t).astype(BF16)
            qt_ref[h, QK_DIM:, rs] = q_tail
            ka = kn[:, h * NOPE:(h + 1) * NOPE]
            rk = lax.rsqrt((jnp.sum(ka * ka, axis=-1, keepdims=True) + kpe_ss) * inv_d + EPS)
            k_ref[h, rs, :NOPE] = (ka * rk * khg[:, :NOPE]).astype(BF16)
            k_ref[h, rs, NOPE:] = (kpe_rot * rk + one_col).astype(BF16)
            vt_ref[h, :V_DIM, rs] = vt[h * V_DIM:(h + 1) * V_DIM, :].astype(BF16)
            vt_ref[h, V_DIM:, rs] = ones_rows

    for r0 in range(0, x_ref.shape[0], sub):
        part(r0)


def _proj(x1, pos, mixg, win, qg, wuqt, kvg, wuk, wuvt, qhg, khg, inv_freq, wg, wu, *, rows):
    s, d = x1.shape
    f = wg.shape[1]
    fp = _padded_hidden(f)
    w_rows = d // (s // rows)
    assert w_rows % SUBLANES_BF16 == 0
    sub = min(PROJ_SUB, rows)
    qhgt = jnp.broadcast_to(qhg.reshape(HEAD_PAD, 1), (HEAD_PAD, sub))
    invft = jnp.broadcast_to(inv_freq.reshape(-1, 1), (ROPE // 2, sub))
    row = lambda w: pl.BlockSpec((rows, w), lambda i: (i, 0))
    in_specs = [row(d), pl.BlockSpec((1, rows), lambda i: (0, i)), _resident(mixg.shape),
                _resident(win.shape), _resident(qg.shape), _resident(wuqt.shape),
                _resident(kvg.shape), _resident(wuk.shape), _resident(wuvt.shape),
                _resident(qhgt.shape), _resident(khg.shape), _resident(invft.shape),
                pl.BlockSpec((w_rows, f), lambda i: (i, 0)),
                pl.BlockSpec((w_rows, f), lambda i: (i, 0))]
    out_shape = [
        jax.ShapeDtypeStruct((s, CONV_CH), F32),
        jax.ShapeDtypeStruct((N_HEADS, HEAD_PAD, s), BF16),
        jax.ShapeDtypeStruct((N_HEADS, s, HEAD_PAD), BF16),
        jax.ShapeDtypeStruct((N_HEADS, V_PAD, s), BF16),
        jax.ShapeDtypeStruct((fp // FF_TILE, d, FF_TILE), BF16),
        jax.ShapeDtypeStruct((fp // FF_TILE, d, FF_TILE), BF16),
    ]
    out_specs = [
        row(CONV_CH),
        pl.BlockSpec((N_HEADS, HEAD_PAD, rows), lambda i: (0, 0, i)),
        pl.BlockSpec((N_HEADS, rows, HEAD_PAD), lambda i: (0, i, 0)),
        pl.BlockSpec((N_HEADS, V_PAD, rows), lambda i: (0, 0, i)),
        pl.BlockSpec((fp // FF_TILE, w_rows, FF_TILE), lambda i: (0, i, 0)),
        pl.BlockSpec((fp // FF_TILE, w_rows, FF_TILE), lambda i: (0, i, 0)),
    ]
    return pl.pallas_call(
        functools.partial(_proj_kernel, sub=sub),
        grid=(s // rows,),
        in_specs=in_specs,
        out_specs=out_specs,
        out_shape=out_shape,
        compiler_params=_params(("parallel",)),
        name="proj",
    )(x1, pos, mixg, win, qg, wuqt, kvg, wuk, wuvt, qhgt, khg, invft, wg, wu)


def _attn_kernel(qt_ref, k_ref, vt_ref, o_ref, qa_ref, p0_ref, p1_ref, r_ref, pm_ref, a_ref,
                 bad_ref, acc_ref, *, tk):
    nk = k_ref.shape[0] // tk
    tq = qt_ref.shape[1]
    p_refs = (p0_ref, p1_ref)
    first_row = lax.broadcasted_iota(jnp.int32, (SUBLANES_BF16, tq), 0) == 0

    def set_reference(r):
        qa_ref[QK_DIM:QK_DIM + SUBLANES_BF16, :] = jnp.where(first_row, -r, 0.0).astype(BF16)

    def scores(c):
        off = pl.multiple_of(c * tk, tk)
        return jnp.dot(k_ref[pl.ds(off, tk), :], qa_ref[...], preferred_element_type=F32)

    def values(c, slot, alpha):
        off = pl.multiple_of(c * tk, tk)
        pv = jnp.dot(vt_ref[:, pl.ds(off, tk)], p_refs[slot][...], preferred_element_type=F32)
        acc_ref[...] = alpha * acc_ref[...] + pv

    def emit(p, slot):
        pb = p.astype(BF16)
        p_refs[slot][...] = pb
        return jnp.max(pb, axis=0, keepdims=True).astype(F32)

    def step(c, slot):
        r_prev = r_ref[...]
        pm_prev = pm_ref[...]
        r = jnp.maximum(r_prev, r_prev + jnp.log2(pm_prev)).astype(BF16).astype(F32)
        alpha = jnp.exp2(r_prev - r)
        set_reference(r)
        pm_ref[...] = emit(jnp.exp2(scores(c)), slot)
        values(c - 1, 1 - slot, a_ref[...])
        bad_ref[...] = jnp.maximum(bad_ref[...], jnp.where(pm_prev <= 2.0 ** MAX_JUMP, 0.0, 1.0))
        r_ref[...] = r
        a_ref[...] = alpha

    qa_ref[...] = qt_ref[...]
    pm_ref[...] = emit(jnp.exp2(scores(0)), 0)
    r_ref[...] = jnp.zeros_like(r_ref)
    a_ref[...] = jnp.zeros_like(a_ref)
    bad_ref[...] = jnp.zeros_like(bad_ref)
    acc_ref[...] = jnp.zeros_like(acc_ref)

    def body(t, carry):
        c = 2 * t + 1
        step(c, 1)
        step(c + 1, 0)
        return carry

    lax.fori_loop(0, (nk - 2) // 2, body, 0)
    step(nk - 1, 1)
    values(nk - 1, 1, a_ref[...])
    bad = jnp.maximum(bad_ref[...], jnp.where(pm_ref[...] <= 2.0 ** MAX_JUMP, 0.0, 1.0))
    bad = jnp.maximum(bad, jnp.where(acc_ref[V_DIM:V_DIM + 1, :] >= 2.0 ** -MAX_JUMP, 0.0, 1.0))

    @pl.when(jnp.max(bad) > 0.0)
    def _():
        set_reference(jnp.zeros_like(bad))
        r_ref[...] = jnp.full(r_ref.shape, -jnp.inf, F32)
        acc_ref[...] = jnp.zeros_like(acc_ref)

        def exact(c, carry):
            st = scores(c)
            m_prev = r_ref[...]
            m_new = jnp.maximum(m_prev, jnp.max(st, axis=0, keepdims=True))
            p0_ref[...] = jnp.exp2(st - m_new).astype(BF16)
            values(c, 0, jnp.exp2(m_prev - m_new))
            r_ref[...] = m_new
            return carry

        lax.fori_loop(0, nk, exact, 0)

    o = acc_ref[:V_DIM, :] * (1.0 / acc_ref[V_DIM:V_DIM + 1, :])
    o_ref[...] = o.T.astype(o_ref.dtype)


def _attn(qt, k, vt, *, tq, tk):
    h, s, dp = k.shape
    assert (s // tk) % 2 == 0
    stat = pltpu.VMEM((1, tq), F32)
    return pl.pallas_call(
        functools.partial(_attn_kernel, tk=tk),
        grid=(h, s // tq),
        in_specs=[
            pl.BlockSpec((None, dp, tq), lambda hh, i: (hh, 0, i)),
            pl.BlockSpec((None, s, dp), lambda hh, i: (hh, 0, 0)),
            pl.BlockSpec((None, V_PAD, s), lambda hh, i: (hh, 0, 0)),
        ],
        out_specs=pl.BlockSpec((tq, V_DIM), lambda hh, i: (i, hh)),
        out_shape=jax.ShapeDtypeStruct((s, h * V_DIM), BF16),
        scratch_shapes=[pltpu.VMEM((dp, tq), BF16),
                        pltpu.VMEM((tk, tq), BF16), pltpu.VMEM((tk, tq), BF16),
                        stat, stat, stat, stat, pltpu.VMEM((V_PAD, tq), F32)],
        compiler_params=_params(("parallel", "arbitrary")),
        name="attn",
    )(qt, k, vt)


def _conv_kernel(prev_ref, u_ref, next_ref, w_ref, b_ref, g_ref, o_ref, sh_ref, y_ref, sq_ref, *,
                 sub):
    i = pl.program_id(0)
    rows = u_ref.shape[0]
    blocks = CONV_CH // LANES
    has_prev = i > 0
    has_next = i < pl.num_programs(0) - 1
    for c in range(blocks):
        cs = slice(c * LANES, (c + 1) * LANES)
        sh_ref[c, 0, :HALO, :] = jnp.where(has_prev, prev_ref[:, cs], 0.0)
        sh_ref[c, 0, HALO:HALO + rows, :] = u_ref[:, cs]
        sh_ref[c, 0, HALO + rows:, :] = jnp.where(has_next, next_ref[:, cs], 0.0)
    n = rows + 2 * HALO - SUBLANES
    base = HALO - CONV_PAD

    sq_ref[...] = jnp.zeros_like(sq_ref)

    def lane_block(ci, carry):
        cs = pl.ds(pl.multiple_of(ci * LANES, LANES), LANES)
        for b in range(1, SUBLANES):
            sh_ref[ci, b, :n, :] = sh_ref[ci, 0, b:b + n, :]
        for r0 in range(0, rows, sub):
            acc = jnp.broadcast_to(b_ref[:, cs], (sub, LANES))
            for t in range(CONV_WIDTH):
                a, b = divmod(base + t, SUBLANES)
                lo = r0 + a * SUBLANES
                acc = acc + sh_ref[ci, b, lo:lo + sub, :] * w_ref[t:t + 1, cs]
            y_ref[ci, r0:r0 + sub, :] = acc
            sq_ref[r0:r0 + sub, :] += acc * acc
        return carry

    lax.fori_loop(0, blocks, lane_block, 0)
    ssq = jnp.sum(sq_ref[...], axis=-1, keepdims=True)
    inv = lax.rsqrt(ssq * (1.0 / CONV_CH) + EPS)
    for c in range(blocks):
        cs = slice(c * LANES, (c + 1) * LANES)
        y = y_ref[c] * inv * g_ref[:, cs]
        o_ref[:, cs] = (y * jax.nn.sigmoid(y)).astype(o_ref.dtype)


def _conv(u, w, b, g, *, rows, sub):
    s, c = u.shape
    hb = rows // HALO
    last = s // HALO - 1
    return pl.pallas_call(
        functools.partial(_conv_kernel, sub=sub),
        grid=(s // rows,),
        in_specs=[
            pl.BlockSpec((HALO, c), lambda i: (jnp.maximum(i * hb - 1, 0), 0)),
            pl.BlockSpec((rows, c), lambda i: (i, 0)),
            pl.BlockSpec((HALO, c), lambda i: (jnp.minimum((i + 1) * hb, last), 0)),
            _resident(w.shape), _resident(b.shape), _resident(g.shape),
        ],
        out_specs=pl.BlockSpec((rows, c), lambda i: (i, 0)),
        out_shape=jax.ShapeDtypeStruct((s, c), BF16),
        scratch_shapes=[pltpu.VMEM((c // LANES, SUBLANES, rows + 2 * HALO, LANES), F32),
                        pltpu.VMEM((c // LANES, rows, LANES), F32), pltpu.VMEM((rows, LANES), F32)],
        compiler_params=_params(("parallel",)),
        name="conv",
    )(u, u, u, w, b, g)


def _outproj_kernel(x_ref, a_ref, b_ref, wa_ref, wb_ref, wd_ref, o_ref, wdo_ref):
    _cast_rows_kernel(wd_ref, wdo_ref)
    o_ref[...] = (x_ref[...]
                  + jnp.dot(a_ref[...], wa_ref[...], preferred_element_type=F32)
                  + jnp.dot(b_ref[...], wb_ref[...], preferred_element_type=F32))


def _outproj(x1, a, b, wa, wb, wd, *, rows):
    s, d = x1.shape
    steps = s // rows
    f, dw = wd.shape
    fp = _padded_hidden(f)
    strip = max(LANES, dw // steps)
    per_strip = steps // (dw // strip)
    assert steps == per_strip * (dw // strip)
    row = lambda w: pl.BlockSpec((rows, w), lambda i: (i, 0))
    return pl.pallas_call(
        _outproj_kernel,
        grid=(steps,),
        in_specs=[row(d), row(a.shape[1]), row(b.shape[1]), _resident(wa.shape), _resident(wb.shape),
                  pl.BlockSpec((f, strip), lambda i: (0, i // per_strip))],
        out_specs=[row(d), pl.BlockSpec((fp, strip), lambda i: (0, i // per_strip))],
        out_shape=[jax.ShapeDtypeStruct((s, d), F32), jax.ShapeDtypeStruct((fp, dw), BF16)],
        compiler_params=_params(("arbitrary",)),
        name="outproj",
    )(x1, a, b, wa, wb, wd)


def _pad_cols(w, n):
    return jnp.pad(w, ((0, 0), (0, n - w.shape[1])))


def _cast_cols_kernel(w_ref, o_ref):
    nb, rows, width = o_ref.shape
    pad = nb * width - w_ref.shape[1]
    for j in range(nb):
        lo = j * width - pad
        if lo + width <= 0:
            o_ref[j] = jnp.zeros((rows, width), BF16)
        elif lo < 0:
            o_ref[j, :, :-lo] = jnp.zeros((rows, -lo), BF16)
            o_ref[j, :, -lo:] = w_ref[:, :lo + width].astype(BF16)
        else:
            o_ref[j] = w_ref[:, lo:lo + width].astype(BF16)


def _cast_pad_cols(w, n_out):
    r, n = w.shape
    rows = min(CAST_ROWS, r)
    nb = n_out // FF_TILE
    return pl.pallas_call(
        _cast_cols_kernel,
        grid=(r // rows,),
        in_specs=[pl.BlockSpec((rows, n), lambda i: (i, 0))],
        out_specs=pl.BlockSpec((nb, rows, FF_TILE), lambda i: (0, i, 0)),
        out_shape=jax.ShapeDtypeStruct((nb, r, FF_TILE), BF16),
        compiler_params=_params(("parallel",)),
        name="cast_cols",
    )(w)


def _cast_rows_kernel(w_ref, o_ref):
    pad = o_ref.shape[0] - w_ref.shape[0]
    o_ref[:pad, :] = jnp.zeros((pad, o_ref.shape[1]), BF16)
    o_ref[pad:, :] = w_ref[...].astype(BF16)


def _cast_pad_rows(w, n_out):
    n, c = w.shape
    cols = min(CAST_COLS, c)
    return pl.pallas_call(
        _cast_rows_kernel,
        grid=(c // cols,),
        in_specs=[pl.BlockSpec((n, cols), lambda i: (0, i))],
        out_specs=pl.BlockSpec((n_out, cols), lambda i: (0, i)),
        out_shape=jax.ShapeDtypeStruct((n_out, c), BF16),
        compiler_params=_params(("parallel",)),
        name="cast_rows",
    )(w)


def _padded_hidden(f):
    assert f % LANES == 0
    return -(-f // FF_TILE) * FF_TILE


def _ffn_weights(wg, wu, wd):
    fp = _padded_hidden(wg.shape[1])
    return _cast_pad_cols(wg, fp), _cast_pad_cols(wu, fp), _cast_pad_rows(wd, fp)


def _head_gain(g):
    return jnp.pad(g, (0, HEAD_PAD - QK_DIM)).reshape(1, HEAD_PAD)


def _layer(x, positions, tiles, ffn1_norm, ffn1_w_gate, ffn1_w_up, ffn1_w_down, mix_norm, w_in,
           q_norm, w_uq, kv_norm, w_ukv, q_head_norm, k_head_norm, dw_kernel, dw_bias, conv_norm,
           w_out, ffn2_norm, ffn2_w_gate, ffn2_w_up, ffn2_w_down, final_norm):
    s, d = x.shape
    row = lambda g: g.reshape(1, -1)

    hidden = ffn1_w_gate.shape[1]
    x1 = _ffn(x, row(ffn1_norm), *_ffn_weights(ffn1_w_gate, ffn1_w_up, ffn1_w_down), None,
              rows=tiles["ffn"], hidden=hidden)

    c2 = 2 * CONV_CH + Q_RANK + KV_RANK
    win = _pad_cols(w_in.astype(BF16), c2 + LANES)
    wuq = jnp.pad(w_uq.astype(BF16).reshape(Q_RANK, N_HEADS, QK_DIM),
                  ((0, 0), (0, 0), (0, HEAD_PAD - QK_DIM))).reshape(Q_RANK, N_HEADS * HEAD_PAD)
    wkv = w_ukv.astype(BF16).reshape(KV_RANK, N_HEADS, NOPE + V_DIM)
    wuk = wkv[:, :, :NOPE].reshape(KV_RANK, N_HEADS * NOPE)
    wuv = wkv[:, :, NOPE:].reshape(KV_RANK, N_HEADS * V_DIM)
    inv_freq = ROPE_THETA ** (-(jnp.arange(ROPE // 2, dtype=F32) * 2.0 / ROPE))
    u, qt, k, vt, wg2, wu2 = _proj(
        x1, positions.reshape(1, s), row(mix_norm), win, row(q_norm), wuq.T, row(kv_norm), wuk,
        wuv.T, _head_gain(q_head_norm), _head_gain(k_head_norm), inv_freq, ffn2_w_gate, ffn2_w_up,
        rows=tiles["proj"])

    b_out = _attn(qt, k, vt, tq=tiles["attn_q"], tk=tiles["attn_k"])
    a_out = _conv(u, jnp.pad(dw_kernel, ((0, 1), (0, 0))), row(dw_bias), row(conv_norm),
                  rows=tiles["conv"], sub=tiles["conv_sub"])
    wo = w_out.astype(BF16)
    x2, wd2 = _outproj(x1, a_out, b_out, wo[:CONV_CH], wo[CONV_CH:], ffn2_w_down, rows=tiles["out"])

    return _ffn(x2, row(ffn2_norm), wg2, wu2, wd2, row(final_norm), rows=tiles["ffn"],
                hidden=hidden)


def _tiles(s):
    return {
        "ffn": min(FFN_ROWS, s), "proj": min(PROJ_ROWS, s), "attn_q": min(ATTN_Q, s),
        "attn_k": min(ATTN_K, s), "conv": min(CONV_ROWS, s), "conv_sub": CONV_SUB,
        "out": min(OUT_ROWS, s),
    }


def kernel(x, positions, ffn1_norm, ffn1_w_gate, ffn1_w_up, ffn1_w_down, mix_norm, w_in, q_norm, w_uq, kv_norm, w_ukv, q_head_norm, k_head_norm, dw_kernel, dw_bias, conv_norm, w_out, ffn2_norm, ffn2_w_gate, ffn2_w_up, ffn2_w_down, final_norm):
    batch, s, d = x.shape
    assert ffn1_norm.shape[0] == 1, "single-layer stack expected"
    outs = []
    for bi in range(batch):
        outs.append(_layer(
            x[bi], positions[bi], _tiles(s), ffn1_norm[0], ffn1_w_gate[0], ffn1_w_up[0],
            ffn1_w_down[0], mix_norm[0], w_in[0], q_norm[0], w_uq[0], kv_norm[0], w_ukv[0],
            q_head_norm[0], k_head_norm[0], dw_kernel[0], dw_bias[0], conv_norm[0], w_out[0],
            ffn2_norm[0], ffn2_w_gate[0], ffn2_w_up[0], ffn2_w_down[0], final_norm[0]))
    return jnp.stack(outs)
```

```python
import functools
import math

import jax
import jax.numpy as jnp
from jax import lax
from jax.experimental import pallas as pl
from jax.experimental.pallas import tpu as pltpu

F32 = jnp.float32
BF16 = jnp.bfloat16

EPS = 1e-6
ROPE_THETA = 10000.0
CONV_CH = 1024
CONV_WIDTH = 31
CONV_PAD = CONV_WIDTH // 2
N_HEADS = 8
NOPE = 128
ROPE = 64
QK_DIM = NOPE + ROPE
V_DIM = 128
Q_RANK = 512
KV_RANK = 256

LANES = 128
SUBLANES = 8
SUBLANES_BF16 = 16
HEAD_PAD = 2 * LANES
V_PAD = V_DIM + SUBLANES_BF16
VMEM_LIMIT = 56 * 1024 * 1024

FF_TILE = 1024
FF_UNIT = 256
FFN_ROWS = 512
PROJ_ROWS = 512
PROJ_SUB = 256
ATTN_Q = 2048
ATTN_K = 1024
CONV_ROWS = 512
CONV_SUB = 256
OUT_ROWS = 512
CAST_ROWS = 256
CAST_COLS = 256
HALO = 16
MAX_JUMP = 64.0


def _rms(x, g):
    ms = jnp.mean(x * x, axis=-1, keepdims=True)
    return x * lax.rsqrt(ms + EPS) * g


def _params(sem):
    return pltpu.CompilerParams(dimension_semantics=sem, vmem_limit_bytes=VMEM_LIMIT)


def _resident(shape):
    nd = len(shape)
    return pl.BlockSpec(shape, lambda *_: (0,) * nd, pipeline_mode=pl.Buffered(1))


def _ffn_kernel(*refs, final_norm, skip):
    if final_norm:
        x_ref, g_ref, wg_ref, wu_ref, wd_ref, fg_ref, o_ref, xn_ref, h_ref = refs
    else:
        x_ref, g_ref, wg_ref, wu_ref, wd_ref, o_ref, xn_ref, h_ref = refs
    f = pl.program_id(1)
    tf = wg_ref.shape[1]

    def hidden_tiles(start):
        xn = xn_ref[...]
        for c in range(start, tf, FF_UNIT):
            gate = jnp.dot(xn, wg_ref[:, c:c + FF_UNIT], preferred_element_type=F32)
            up = jnp.dot(xn, wu_ref[:, c:c + FF_UNIT], preferred_element_type=F32)
            h_ref[:, c:c + FF_UNIT] = (gate * jax.nn.sigmoid(gate) * up).astype(BF16)
        o_ref[...] += jnp.dot(h_ref[:, start:], wd_ref[start:, :], preferred_element_type=F32)

    @pl.when(f == 0)
    def _():
        xn_ref[...] = _rms(x_ref[...], g_ref[...]).astype(BF16)
        o_ref[...] = jnp.zeros_like(o_ref)
        hidden_tiles(skip)

    @pl.when(f > 0)
    def _():
        hidden_tiles(0)

    @pl.when(f == pl.num_programs(1) - 1)
    def _():
        y = x_ref[...] + 0.5 * o_ref[...]
        if final_norm:
            y = _rms(y, fg_ref[...])
        o_ref[...] = y


def _ffn(x, g, wg, wu, wd, final_g, *, rows, hidden):
    s, d = x.shape
    fp = wd.shape[0]
    tf = FF_TILE
    skip = (fp - hidden) // FF_UNIT * FF_UNIT
    assert fp % tf == 0 and 0 <= skip < tf
    in_specs = [
        pl.BlockSpec((rows, d), lambda i, f: (i, 0)),
        pl.BlockSpec((1, d), lambda i, f: (0, 0)),
        pl.BlockSpec((None, d, tf), lambda i, f: (f, 0, 0)),
        pl.BlockSpec((None, d, tf), lambda i, f: (f, 0, 0)),
        pl.BlockSpec((tf, d), lambda i, f: (f, 0)),
    ]
    args = [x, g, wg, wu, wd]
    if final_g is not None:
        in_specs.append(pl.BlockSpec((1, d), lambda i, f: (0, 0)))
        args.append(final_g)
    return pl.pallas_call(
        functools.partial(_ffn_kernel, final_norm=final_g is not None, skip=skip),
        grid=(s // rows, fp // tf),
        in_specs=in_specs,
        out_specs=pl.BlockSpec((rows, d), lambda i, f: (i, 0)),
        out_shape=jax.ShapeDtypeStruct((s, d), F32),
        scratch_shapes=[pltpu.VMEM((rows, d), BF16), pltpu.VMEM((rows, tf), BF16)],
        compiler_params=_params(("parallel", "arbitrary")),
        name="ffn_final" if final_g is not None else "ffn",
    )(*args)


def _proj_kernel(x_ref, pos_ref, mixg_ref, win_ref, qg_ref, wuqt_ref, kvg_ref, wuk_ref, wuvt_ref,
                 qhgt_ref, khg_ref, invft_ref, wg_ref, wu_ref,
                 u_ref, qt_ref, k_ref, vt_ref, wgo_ref, wuo_ref, *, sub):
    _cast_cols_kernel(wg_ref, wgo_ref)
    _cast_cols_kernel(wu_ref, wuo_ref)

    c0 = 2 * CONV_CH
    c1 = c0 + Q_RANK
    c2 = c1 + KV_RANK
    half = ROPE // 2
    inv_d = 1.0 / QK_DIM
    scale = math.log2(math.e) / math.sqrt(QK_DIM)
    qhgt = qhgt_ref[...]
    khg = khg_ref[...]
    lane = lax.broadcasted_iota(jnp.int32, (sub, LANES), 1)
    one_col = jnp.where(lane == ROPE, 1.0, 0.0)
    ones_rows = jnp.where(
        lax.broadcasted_iota(jnp.int32, (V_PAD - V_DIM, sub), 0) == 0, 1.0, 0.0).astype(BF16)
    q_tail = jnp.zeros((HEAD_PAD - QK_DIM, sub), BF16)

    def part(r0):
        rs = slice(r0, r0 + sub)
        hn = _rms(x_ref[rs, :], mixg_ref[...]).astype(BF16)
        z = jnp.dot(hn, win_ref[...], preferred_element_type=F32)
        u_ref[rs, :] = z[:, :CONV_CH] * jax.nn.sigmoid(z[:, CONV_CH:c0])
        qnt = _rms(z[:, c0:c1], qg_ref[...]).T.astype(BF16)
        kvn = _rms(z[:, c1:c2], kvg_ref[...])
        kvnt = kvn.T.astype(BF16)
        kpe = z[:, c2:c2 + LANES]
        qt = jnp.dot(wuqt_ref[...], qnt, preferred_element_type=F32)
        vt = jnp.dot(wuvt_ref[...], kvnt, preferred_element_type=F32)
        kn = jnp.dot(kvn.astype(BF16), wuk_ref[...], preferred_element_type=F32)

        ang = invft_ref[...] * pos_ref[:, rs].astype(F32)
        cos_t = jnp.cos(ang)
        sin_t = jnp.sin(ang)
        zeros = jnp.zeros((LANES - ROPE, sub), F32)
        cos_k = jnp.concatenate([cos_t, cos_t, zeros], axis=0).T
        sin_k = jnp.concatenate([-sin_t, sin_t, zeros], axis=0).T

        kpe_g = kpe * khg[:, NOPE:]
        kpe_rot = kpe_g * cos_k + jnp.where(lane < half, pltpu.roll(kpe_g, LANES - half, 1),
                                            pltpu.roll(kpe_g, half, 1)) * sin_k
        kpe_ss = jnp.sum(kpe * kpe, axis=-1, keepdims=True)
        for h in range(N_HEADS):
            qh = qt[h * HEAD_PAD:h * HEAD_PAD + QK_DIM, :]
            r = lax.rsqrt(jnp.sum(qh * qh, axis=0, keepdims=True) * inv_d + EPS) * scale
            qh = qh * qhgt[:QK_DIM, :] * r
            x1 = qh[NOPE:NOPE + half, :]
            x2 = qh[NOPE + half:, :]
            qt_ref[h, :NOPE, rs] = qh[:NOPE, :].astype(BF16)
            qt_ref[h, NOPE:NOPE + half, rs] = (x1 * cos_t - x2 * sin_t).astype(BF16)
            qt_ref[h, NOPE + half:QK_DIM, rs] = (x2 * cos_t + x1 * sin_t).astype(BF16)
            qt_ref[h, QK_DIM:, rs] = q_tail
            ka = kn[:, h * NOPE:(h + 1) * NOPE]
            rk = lax.rsqrt((jnp.sum(ka * ka, axis=-1, keepdims=True) + kpe_ss) * inv_d + EPS)
            k_ref[h, rs, :NOPE] = (ka * rk * khg[:, :NOPE]).astype(BF16)
            k_ref[h, rs, NOPE:] = (kpe_rot * rk + one_col).astype(BF16)
            vt_ref[h, :V_DIM, rs] = vt[h * V_DIM:(h + 1) * V_DIM, :].astype(BF16)
            vt_ref[h, V_DIM:, rs] = ones_rows

    for r0 in range(0, x_ref.shape[0], sub):
        part(r0)


def _proj(x1, pos, mixg, win, qg, wuqt, kvg, wuk, wuvt, qhg, khg, inv_freq, wg, wu, *, rows):
    s, d = x1.shape
    f = wg.shape[1]
    fp = _padded_hidden(f)
    w_rows = d // (s // rows)
    assert w_rows % SUBLANES_BF16 == 0
    sub = min(PROJ_SUB, rows)
    qhgt = jnp.broadcast_to(qhg.reshape(HEAD_PAD, 1), (HEAD_PAD, sub))
    invft = jnp.broadcast_to(inv_freq.reshape(-1, 1), (ROPE // 2, sub))
    row = lambda w: pl.BlockSpec((rows, w), lambda i: (i, 0))
    in_specs = [row(d), pl.BlockSpec((1, rows), lambda i: (0, i)), _resident(mixg.shape),
                _resident(win.shape), _resident(qg.shape), _resident(wuqt.shape),
                _resident(kvg.shape), _resident(wuk.shape), _resident(wuvt.shape),
                _resident(qhgt.shape), _resident(khg.shape), _resident(invft.shape),
                pl.BlockSpec((w_rows, f), lambda i: (i, 0)),
                pl.BlockSpec((w_rows, f), lambda i: (i, 0))]
    out_shape = [
        jax.ShapeDtypeStruct((s, CONV_CH), F32),
        jax.ShapeDtypeStruct((N_HEADS, HEAD_PAD, s), BF16),
        jax.ShapeDtypeStruct((N_HEADS, s, HEAD_PAD), BF16),
        jax.ShapeDtypeStruct((N_HEADS, V_PAD, s), BF16),
        jax.ShapeDtypeStruct((fp // FF_TILE, d, FF_TILE), BF16),
        jax.ShapeDtypeStruct((fp // FF_TILE, d, FF_TILE), BF16),
    ]
    out_specs = [
        row(CONV_CH),
        pl.BlockSpec((N_HEADS, HEAD_PAD, rows), lambda i: (0, 0, i)),
        pl.BlockSpec((N_HEADS, rows, HEAD_PAD), lambda i: (0, i, 0)),
        pl.BlockSpec((N_HEADS, V_PAD, rows), lambda i: (0, 0, i)),
        pl.BlockSpec((fp // FF_TILE, w_rows, FF_TILE), lambda i: (0, i, 0)),
        pl.BlockSpec((fp // FF_TILE, w_rows, FF_TILE), lambda i: (0, i, 0)),
    ]
    return pl.pallas_call(
        functools.partial(_proj_kernel, sub=sub),
        grid=(s // rows,),
        in_specs=in_specs,
        out_specs=out_specs,
        out_shape=out_shape,
        compiler_params=_params(("parallel",)),
        name="proj",
    )(x1, pos, mixg, win, qg, wuqt, kvg, wuk, wuvt, qhgt, khg, invft, wg, wu)


def _attn_kernel(qt_ref, k_ref, vt_ref, o_ref, qa_ref, p0_ref, p1_ref, r_ref, pm_ref, a_ref,
                 bad_ref, acc_ref, *, tk):
    nk = k_ref.shape[0] // tk
    tq = qt_ref.shape[1]
    p_refs = (p0_ref, p1_ref)
    first_row = lax.broadcasted_iota(jnp.int32, (SUBLANES_BF16, tq), 0) == 0

    def set_reference(r):
        qa_ref[QK_DIM:QK_DIM + SUBLANES_BF16, :] = jnp.where(first_row, -r, 0.0).astype(BF16)

    def scores(c):
        off = pl.multiple_of(c * tk, tk)
        return jnp.dot(k_ref[pl.ds(off, tk), :], qa_ref[...], preferred_element_type=F32)

    def values(c, slot, alpha):
        off = pl.multiple_of(c * tk, tk)
        pv = jnp.dot(vt_ref[:, pl.ds(off, tk)], p_refs[slot][...], preferred_element_type=F32)
        acc_ref[...] = alpha * acc_ref[...] + pv

    def emit(p, slot):
        pb = p.astype(BF16)
        p_refs[slot][...] = pb
        return jnp.max(pb, axis=0, keepdims=True).astype(F32)

    def step(c, slot):
        r_prev = r_ref[...]
        pm_prev = pm_ref[...]
        r = jnp.maximum(r_prev, r_prev + jnp.log2(pm_prev)).astype(BF16).astype(F32)
        alpha = jnp.exp2(r_prev - r)
        set_reference(r)
        pm_ref[...] = emit(jnp.exp2(scores(c)), slot)
        values(c - 1, 1 - slot, a_ref[...])
        bad_ref[...] = jnp.maximum(bad_ref[...], jnp.where(pm_prev <= 2.0 ** MAX_JUMP, 0.0, 1.0))
        r_ref[...] = r
        a_ref[...] = alpha

    qa_ref[...] = qt_ref[...]
    pm_ref[...] = emit(jnp.exp2(scores(0)), 0)
    r_ref[...] = jnp.zeros_like(r_ref)
    a_ref[...] = jnp.zeros_like(a_ref)
    bad_ref[...] = jnp.zeros_like(bad_ref)
    acc_ref[...] = jnp.zeros_like(acc_ref)

    def body(t, carry):
        c = 2 * t + 1
        step(c, 1)
        step(c + 1, 0)
        return carry

    lax.fori_loop(0, (nk - 2) // 2, body, 0)
    step(nk - 1, 1)
    values(nk - 1, 1, a_ref[...])
    bad = jnp.maximum(bad_ref[...], jnp.where(pm_ref[...] <= 2.0 ** MAX_JUMP, 0.0, 1.0))
    bad = jnp.maximum(bad, jnp.where(acc_ref[V_DIM:V_DIM + 1, :] >= 2.0 ** -MAX_JUMP, 0.0, 1.0))

    @pl.when(jnp.max(bad) > 0.0)
    def _():
        set_reference(jnp.zeros_like(bad))
        r_ref[...] = jnp.full(r_ref.shape, -jnp.inf, F32)
        acc_ref[...] = jnp.zeros_like(acc_ref)

        def exact(c, carry):
            st = scores(c)
            m_prev = r_ref[...]
            m_new = jnp.maximum(m_prev, jnp.max(st, axis=0, keepdims=True))
            p0_ref[...] = jnp.exp2(st - m_new).astype(BF16)
            values(c, 0, jnp.exp2(m_prev - m_new))
            r_ref[...] = m_new
            return carry

        lax.fori_loop(0, nk, exact, 0)

    o = acc_ref[:V_DIM, :] * (1.0 / acc_ref[V_DIM:V_DIM + 1, :])
    o_ref[...] = o.T.astype(o_ref.dtype)


def _attn(qt, k, vt, *, tq, tk):
    h, s, dp = k.shape
    assert (s // tk) % 2 == 0
    stat = pltpu.VMEM((1, tq), F32)
    return pl.pallas_call(
        functools.partial(_attn_kernel, tk=tk),
        grid=(h, s // tq),
        in_specs=[
            pl.BlockSpec((None, dp, tq), lambda hh, i: (hh, 0, i)),
            pl.BlockSpec((None, s, dp), lambda hh, i: (hh, 0, 0)),
            pl.BlockSpec((None, V_PAD, s), lambda hh, i: (hh, 0, 0)),
        ],
        out_specs=pl.BlockSpec((tq, V_DIM), lambda hh, i: (i, hh)),
        out_shape=jax.ShapeDtypeStruct((s, h * V_DIM), BF16),
        scratch_shapes=[pltpu.VMEM((dp, tq), BF16),
                        pltpu.VMEM((tk, tq), BF16), pltpu.VMEM((tk, tq), BF16),
                        stat, stat, stat, stat, pltpu.VMEM((V_PAD, tq), F32)],
        compiler_params=_params(("parallel", "arbitrary")),
        name="attn",
    )(qt, k, vt)


def _conv_kernel(prev_ref, u_ref, next_ref, w_ref, b_ref, g_ref, o_ref, sh_ref, y_ref, sq_ref, *,
                 sub):
    i = pl.program_id(0)
    rows = u_ref.shape[0]
    blocks = CONV_CH // LANES
    has_prev = i > 0
    has_next = i < pl.num_programs(0) - 1
    for c in range(blocks):
        cs = slice(c * LANES, (c + 1) * LANES)
        sh_ref[c, 0, :HALO, :] = jnp.where(has_prev, prev_ref[:, cs], 0.0)
        sh_ref[c, 0, HALO:HALO + rows, :] = u_ref[:, cs]
        sh_ref[c, 0, HALO + rows:, :] = jnp.where(has_next, next_ref[:, cs], 0.0)
    n = rows + 2 * HALO - SUBLANES
    base = HALO - CONV_PAD

    sq_ref[...] = jnp.zeros_like(sq_ref)

    def lane_block(ci, carry):
        cs = pl.ds(pl.multiple_of(ci * LANES, LANES), LANES)
        for b in range(1, SUBLANES):
            sh_ref[ci, b, :n, :] = sh_ref[ci, 0, b:b + n, :]
        for r0 in range(0, rows, sub):
            acc = jnp.broadcast_to(b_ref[:, cs], (sub, LANES))
            for t in range(CONV_WIDTH):
                a, b = divmod(base + t, SUBLANES)
                lo = r0 + a * SUBLANES
                acc = acc + sh_ref[ci, b, lo:lo + sub, :] * w_ref[t:t + 1, cs]
            y_ref[ci, r0:r0 + sub, :] = acc
            sq_ref[r0:r0 + sub, :] += acc * acc
        return carry

    lax.fori_loop(0, blocks, lane_block, 0)
    ssq = jnp.sum(sq_ref[...], axis=-1, keepdims=True)
    inv = lax.rsqrt(ssq * (1.0 / CONV_CH) + EPS)
    for c in range(blocks):
        cs = slice(c * LANES, (c + 1) * LANES)
        y = y_ref[c] * inv * g_ref[:, cs]
        o_ref[:, cs] = (y * jax.nn.sigmoid(y)).astype(o_ref.dtype)


def _conv(u, w, b, g, *, rows, sub):
    s, c = u.shape
    hb = rows // HALO
    last = s // HALO - 1
    return pl.pallas_call(
        functools.partial(_conv_kernel, sub=sub),
        grid=(s // rows,),
        in_specs=[
            pl.BlockSpec((HALO, c), lambda i: (jnp.maximum(i * hb - 1, 0), 0)),
            pl.BlockSpec((rows, c), lambda i: (i, 0)),
            pl.BlockSpec((HALO, c), lambda i: (jnp.minimum((i + 1) * hb, last), 0)),
            _resident(w.shape), _resident(b.shape), _resident(g.shape),
        ],
        out_specs=pl.BlockSpec((rows, c), lambda i: (i, 0)),
        out_shape=jax.ShapeDtypeStruct((s, c), BF16),
        scratch_shapes=[pltpu.VMEM((c // LANES, SUBLANES, rows + 2 * HALO, LANES), F32),
                        pltpu.VMEM((c // LANES, rows, LANES), F32), pltpu.VMEM((rows, LANES), F32)],
        compiler_params=_params(("parallel",)),
        name="conv",
    )(u, u, u, w, b, g)


def _outproj_kernel(x_ref, a_ref, b_ref, wa_ref, wb_ref, wd_ref, o_ref, wdo_ref):
    _cast_rows_kernel(wd_ref, wdo_ref)
    o_ref[...] = (x_ref[...]
                  + jnp.dot(a_ref[...], wa_ref[...], preferred_element_type=F32)
                  + jnp.dot(b_ref[...], wb_ref[...], preferred_element_type=F32))


def _outproj(x1, a, b, wa, wb, wd, *, rows):
    s, d = x1.shape
    steps = s // rows
    f, dw = wd.shape
    fp = _padded_hidden(f)
    strip = max(LANES, dw // steps)
    per_strip = steps // (dw // strip)
    assert steps == per_strip * (dw // strip)
    row = lambda w: pl.BlockSpec((rows, w), lambda i: (i, 0))
    return pl.pallas_call(
        _outproj_kernel,
        grid=(steps,),
        in_specs=[row(d), row(a.shape[1]), row(b.shape[1]), _resident(wa.shape), _resident(wb.shape),
                  pl.BlockSpec((f, strip), lambda i: (0, i // per_strip))],
        out_specs=[row(d), pl.BlockSpec((fp, strip), lambda i: (0, i // per_strip))],
        out_shape=[jax.ShapeDtypeStruct((s, d), F32), jax.ShapeDtypeStruct((fp, dw), BF16)],
        compiler_params=_params(("arbitrary",)),
        name="outproj",
    )(x1, a, b, wa, wb, wd)


def _pad_cols(w, n):
    return jnp.pad(w, ((0, 0), (0, n - w.shape[1])))


def _cast_cols_kernel(w_ref, o_ref):
    nb, rows, width = o_ref.shape
    pad = nb * width - w_ref.shape[1]
    for j in range(nb):
        lo = j * width - pad
        if lo + width <= 0:
            o_ref[j] = jnp.zeros((rows, width), BF16)
        elif lo < 0:
            o_ref[j, :, :-lo] = jnp.zeros((rows, -lo), BF16)
            o_ref[j, :, -lo:] = w_ref[:, :lo + width].astype(BF16)
        else:
            o_ref[j] = w_ref[:, lo:lo + width].astype(BF16)


def _cast_pad_cols(w, n_out):
    r, n = w.shape
    rows = min(CAST_ROWS, r)
    nb = n_out // FF_TILE
    return pl.pallas_call(
        _cast_cols_kernel,
        grid=(r // rows,),
        in_specs=[pl.BlockSpec((rows, n), lambda i: (i, 0))],
        out_specs=pl.BlockSpec((nb, rows, FF_TILE), lambda i: (0, i, 0)),
        out_shape=jax.ShapeDtypeStruct((nb, r, FF_TILE), BF16),
        compiler_params=_params(("parallel",)),
        name="cast_cols",
    )(w)


def _cast_rows_kernel(w_ref, o_ref):
    pad = o_ref.shape[0] - w_ref.shape[0]
    o_ref[:pad, :] = jnp.zeros((pad, o_ref.shape[1]), BF16)
    o_ref[pad:, :] = w_ref[...].astype(BF16)


def _cast_pad_rows(w, n_out):
    n, c = w.shape
    cols = min(CAST_COLS, c)
    return pl.pallas_call(
        _cast_rows_kernel,
        grid=(c // cols,),
        in_specs=[pl.BlockSpec((n, cols), lambda i: (0, i))],
        out_specs=pl.BlockSpec((n_out, cols), lambda i: (0, i)),
        out_shape=jax.ShapeDtypeStruct((n_out, c), BF16),
        compiler_params=_params(("parallel",)),
        name="cast_rows",
    )(w)


def _padded_hidden(f):
    assert f % LANES == 0
    return -(-f // FF_TILE) * FF_TILE


def _ffn_weights(wg, wu, wd):
    fp = _padded_hidden(wg.shape[1])
    return _cast_pad_cols(wg, fp), _cast_pad_cols(wu, fp), _cast_pad_rows(wd, fp)


def _head_gain(g):
    return jnp.pad(g, (0, HEAD_PAD - QK_DIM)).reshape(1, HEAD_PAD)


def _layer(x, positions, tiles, ffn1_norm, ffn1_w_gate, ffn1_w_up, ffn1_w_down, mix_norm, w_in,
           q_norm, w_uq, kv_norm, w_ukv, q_head_norm, k_head_norm, dw_kernel, dw_bias, conv_norm,
           w_out, ffn2_norm, ffn2_w_gate, ffn2_w_up, ffn2_w_down, final_norm):
    s, d = x.shape
    row = lambda g: g.reshape(1, -1)

    hidden = ffn1_w_gate.shape[1]
    x1 = _ffn(x, row(ffn1_norm), *_ffn_weights(ffn1_w_gate, ffn1_w_up, ffn1_w_down), None,
              rows=tiles["ffn"], hidden=hidden)

    c2 = 2 * CONV_CH + Q_RANK + KV_RANK
    win = _pad_cols(w_in.astype(BF16), c2 + LANES)
    wuq = jnp.pad(w_uq.astype(BF16).reshape(Q_RANK, N_HEADS, QK_DIM),
                  ((0, 0), (0, 0), (0, HEAD_PAD - QK_DIM))).reshape(Q_RANK, N_HEADS * HEAD_PAD)
    wkv = w_ukv.astype(BF16).reshape(KV_RANK, N_HEADS, NOPE + V_DIM)
    wuk = wkv[:, :, :NOPE].reshape(KV_RANK, N_HEADS * NOPE)
    wuv = wkv[:, :, NOPE:].reshape(KV_RANK, N_HEADS * V_DIM)
    inv_freq = ROPE_THETA ** (-(jnp.arange(ROPE // 2, dtype=F32) * 2.0 / ROPE))
    u, qt, k, vt, wg2, wu2 = _proj(
        x1, positions.reshape(1, s), row(mix_norm), win, row(q_norm), wuq.T, row(kv_norm), wuk,
        wuv.T, _head_gain(q_head_norm), _head_gain(k_head_norm), inv_freq, ffn2_w_gate, ffn2_w_up,
        rows=tiles["proj"])

    b_out = _attn(qt, k, vt, tq=tiles["attn_q"], tk=tiles["attn_k"])
    a_out = _conv(u, jnp.pad(dw_kernel, ((0, 1), (0, 0))), row(dw_bias), row(conv_norm),
                  rows=tiles["conv"], sub=tiles["conv_sub"])
    wo = w_out.astype(BF16)
    x2, wd2 = _outproj(x1, a_out, b_out, wo[:CONV_CH], wo[CONV_CH:], ffn2_w_down, rows=tiles["out"])

    return _ffn(x2, row(ffn2_norm), wg2, wu2, wd2, row(final_norm), rows=tiles["ffn"],
                hidden=hidden)


def _tiles(s):
    return {
        "ffn": min(FFN_ROWS, s), "proj": min(PROJ_ROWS, s), "attn_q": min(ATTN_Q, s),
        "attn_k": min(ATTN_K, s), "conv": min(CONV_ROWS, s), "conv_sub": CONV_SUB,
        "out": min(OUT_ROWS, s),
    }


def kernel(x, positions, ffn1_norm, ffn1_w_gate, ffn1_w_up, ffn1_w_down, mix_norm, w_in, q_norm, w_uq, kv_norm, w_ukv, q_head_norm, k_head_norm, dw_kernel, dw_bias, conv_norm, w_out, ffn2_norm, ffn2_w_gate, ffn2_w_up, ffn2_w_down, final_norm):
    batch, s, d = x.shape
    assert ffn1_norm.shape[0] == 1, "single-layer stack expected"
    outs = []
    for bi in range(batch):
        outs.append(_layer(
            x[bi], positions[bi], _tiles(s), ffn1_norm[0], ffn1_w_gate[0], ffn1_w_up[0],
            ffn1_w_down[0], mix_norm[0], w_in[0], q_norm[0], w_uq[0], kv_norm[0], w_ukv[0],
            q_head_norm[0], k_head_norm[0], dw_kernel[0], dw_bias[0], conv_norm[0], w_out[0],
            ffn2_norm[0], ffn2_w_gate[0], ffn2_w_up[0], ffn2_w_down[0], final_norm[0]))
    return jnp.stack(outs)
```
